```python
import math
import jax
import jax.numpy as jnp
from jax import lax
import numpy as np


D_MODEL = 2048
BATCH = 2
SEQ = 8192
DEPTH = 4

CTX_LEN = 256
GRID_W = 64
GROUP_W = D_MODEL // 4
HG_DK = 128
HG_HEADS = GROUP_W // HG_DK
HG_W = HG_HEADS * HG_DK
SSD_HEADDIM = 64
SSD_HEADS = GROUP_W // SSD_HEADDIM
SSD_W = SSD_HEADS * SSD_HEADDIM
SSD_GROUPS = 2
SSD_STATE = 128
SSD_CONV = 4
SSD_CONV_DIM = SSD_W + 2 * SSD_GROUPS * SSD_STATE
LRU_W = GROUP_W
LRU_BW = 64
LRU_BLOCKS = LRU_W // LRU_BW
LRU_C = 8.0
LRU_CONV = 4
ML_DH = 128
ML_HEADS = GROUP_W // ML_DH
ML_W = ML_HEADS * ML_DH
MIX_W = HG_W + SSD_W + LRU_W + ML_W
HG_COLS = 5 * HG_W
SSD_COLS = SSD_W + SSD_CONV_DIM + 2 * SSD_HEADS
LRU_COLS = 2 * LRU_W
ML_COLS = 4 * ML_W + 4 * ML_HEADS
IN_COLS = HG_COLS + SSD_COLS + LRU_COLS + ML_COLS
CHUNK = 64
N_EXPERT_GROUPS = 4
EXPERTS_PER_GROUP = 8
N_EXPERTS = N_EXPERT_GROUPS * EXPERTS_PER_GROUP
TOP_K = 2
D_EXPERT = D_MODEL // 4
MOE_BLOCK = 256
NORM_EPS = 1e-6
MASK_NEG = -1e30
STATE_NEG = -1e30

kernel_name = 'hybrid_parallel_heads_diffusion_block'


def rmsnorm(x, g):
    xf = x.astype(jnp.float32)
    y = xf * lax.rsqrt(jnp.mean(xf * xf, axis=-1, keepdims=True) + NORM_EPS)
    return (y * g.astype(jnp.float32)).astype(x.dtype)


def head_rms(x):
    return x * lax.rsqrt(jnp.mean(x * x, axis=-1, keepdims=True) + NORM_EPS)


def to_heads(t, h):
    b, l, w = t.shape
    return t.reshape(b, l, h, w // h).transpose(0, 2, 1, 3)


def conv_centred(x, w):
    k, ch = w.shape
    lo = k // 2
    return lax.conv_general_dilated(x, w[:, None, :].astype(x.dtype), window_strides=(1,),
                                    padding=[(lo, k - 1 - lo)],
                                    dimension_numbers=('NWC', 'WIO', 'NWC'),
                                    feature_group_count=ch)


def chunked_gla(q, k, v, log_f, s0):
    bn, h, l, kd = q.shape
    vd = v.shape[-1]
    n = l // CHUNK
    vec = log_f.shape[-1] != 1
    causal = jnp.tril(jnp.ones((CHUNK, CHUNK), bool))

    def chunks(t):
        return jnp.moveaxis(t.reshape(bn, h, n, CHUNK, t.shape[-1]), 2, 0)

    def step(s, inp):
        qc, kc, vc, lc = inp
        b = jnp.cumsum(lc, axis=2)
        diff = b[:, :, :, None, :] - b[:, :, None, :, :]
        dec = jnp.exp(jnp.where(causal[:, :, None], diff, MASK_NEG))
        if vec:
            att = jnp.einsum('bhtk,bhtsk,bhsk->bhts', qc, dec, kc)
        else:
            att = jnp.einsum('bhtk,bhsk->bhts', qc, kc) * dec[..., 0]
        o = (jnp.einsum('bhts,bhsv->bhtv', att, vc)
             + jnp.einsum('bhtk,bhkv->bhtv', qc * jnp.exp(b), s))
        bl = b[:, :, -1:, :]
        s = (s * jnp.exp(bl)[:, :, 0, :, None]
             + jnp.einsum('bhsk,bhsv->bhkv', kc * jnp.exp(bl - b), vc))
        return s, o

    s, o = lax.scan(step, s0, (chunks(q), chunks(k), chunks(v), chunks(log_f)))
    return jnp.moveaxis(o, 0, 2).reshape(bn, h, l, vd), s


def chunked_mlstm(q, k, v, log_i, log_f, state):
    bn, h, l, kd = q.shape
    n = l // CHUNK
    causal = jnp.tril(jnp.ones((CHUNK, CHUNK), bool))

    def chunks(t):
        return jnp.moveaxis(t.reshape(bn, h, n, CHUNK, *t.shape[3:]), 2, 0)

    def step(carry, inp):
        cm, nv, m = carry
        qc, kc, vc, ic, fc = inp
        b = jnp.cumsum(fc, axis=-1)
        dmat = jnp.where(causal, b[..., :, None] - b[..., None, :] + ic[..., None, :], MASK_NEG)
        g = b + m[..., None]
        mt = jnp.maximum(g, jnp.max(dmat, axis=-1))
        w_intra = jnp.exp(dmat - mt[..., None])
        w_inter = jnp.exp(g - mt)
        s = jnp.einsum('bhtk,bhsk->bhts', qc, kc) * w_intra
        num = (jnp.einsum('bhts,bhsv->bhtv', s, vc)
               + w_inter[..., None] * jnp.einsum('bhtk,bhkv->bhtv', qc, cm))
        den = jnp.sum(s, axis=-1) + w_inter * jnp.einsum('bhtk,bhk->bht', qc, nv)
        hout = num / jnp.maximum(jnp.abs(den), jnp.exp(-mt))[..., None]
        bl = b[..., -1]
        dl = bl[..., None] - b + ic
        m_new = jnp.maximum(bl + m, jnp.max(dl, axis=-1))
        ws = jnp.exp(dl - m_new[..., None])
        wc = jnp.exp(bl + m - m_new)
        cm = wc[..., None, None] * cm + jnp.einsum('bhs,bhsk,bhsv->bhkv', ws, kc, vc)
        nv = wc[..., None] * nv + jnp.einsum('bhs,bhsk->bhk', ws, kc)
        return (cm, nv, m_new), hout

    state, hs = lax.scan(step, state, tuple(chunks(t) for t in (q, k, v, log_i, log_f)))
    return jnp.moveaxis(hs, 0, 2).reshape(bn, h, l, v.shape[-1]), state


def lru_run(a, b, h0):
    def combine(lhs, rhs):
        return (lhs[0] * rhs[0], rhs[0] * lhs[1] + rhs[1])
    a_cum, hs = lax.associative_scan(combine, (a, b), axis=1)
    hs = hs + a_cum * h0[:, None, :]
    return hs, hs[:, -1]


def two_segment(run, ctx_args, lat_args, state0, reverse, axis):
    flip = (lambda t: jnp.flip(t, axis)) if reverse else (lambda t: t)
    y_c, s_c = run(*[flip(t) for t in ctx_args], state0)
    y_l, _ = run(*[flip(t) for t in lat_args], s_c)
    return flip(y_c), flip(y_l)


def hgrn2_mixer(u_c, u_l, lb):
    def prep(u):
        q, i, g, z_f, z_b = jnp.split(u, 5, axis=-1)
        dirs = []
        for z, lbd in ((z_f, lb[0]), (z_b, lb[1])):
            f = lbd + (1.0 - lbd) * jax.nn.sigmoid(z)
            log_f = jnp.log(f)
            k = (1.0 - lbd) * jax.nn.sigmoid(-z)
            dirs.append((to_heads(k, HG_HEADS), to_heads(log_f, HG_HEADS)))
        return to_heads(q, HG_HEADS), to_heads(i, HG_HEADS), g, dirs

    qc, ic, gc, dc = prep(u_c)
    ql, il, gl, dl = prep(u_l)
    s0 = jnp.zeros((u_c.shape[0], HG_HEADS, HG_DK, HG_DK), jnp.float32)
    oc, ol = 0.0, 0.0
    for d, rev in enumerate((False, True)):
        yc, yl = two_segment(chunked_gla, (qc, dc[d][0], ic, dc[d][1]),
                             (ql, dl[d][0], il, dl[d][1]), s0, rev, 2)
        oc, ol = oc + yc, ol + yl

    def finish(o, g):
        o = head_rms(jnp.moveaxis(o, 1, 2))
        return o.reshape(g.shape) * jax.nn.silu(g)
    return finish(oc, gc), finish(ol, gl)


def ssd_mixer(u_c, u_l, conv_w, a_log, dt_bias, d_skip, norm_g):
    rep = SSD_HEADS // SSD_GROUPS

    def prep(u):
        z, xbc, dt = jnp.split(u, [SSD_W, SSD_W + SSD_CONV_DIM], axis=-1)
        xbc = jax.nn.silu(conv_centred(xbc, conv_w))
        xs, bm, cmat = jnp.split(xbc, [SSD_W, SSD_W + SSD_GROUPS * SSD_STATE], axis=-1)
        bh = jnp.repeat(to_heads(bm, SSD_GROUPS), rep, axis=1)
        ch = jnp.repeat(to_heads(cmat, SSD_GROUPS), rep, axis=1)
        dirs = []
        for d in range(2):
            delta = jax.nn.softplus(dt[..., d * SSD_HEADS:(d + 1) * SSD_HEADS] + dt_bias[d])
            delta = jnp.moveaxis(delta, 1, 2)[..., None]
            dirs.append((bh * delta, delta * (-jnp.exp(a_log[d]))[None, :, None, None]))
        return z, to_heads(xs, SSD_HEADS), ch, dirs

    zc, xc, cc, dc = prep(u_c)
    zl, xl, cl, dl = prep(u_l)
    s0 = jnp.zeros((u_c.shape[0], SSD_HEADS, SSD_STATE, SSD_HEADDIM), jnp.float32)
    yc = d_skip[None, :, None, None] * xc
    yl = d_skip[None, :, None, None] * xl
    for d, rev in enumerate((False, True)):
        oc, ol = two_segment(chunked_gla, (cc, dc[d][0], xc, dc[d][1]),
                             (cl, dl[d][0], xl, dl[d][1]), s0, rev, 2)
        yc, yl = yc + oc, yl + ol

    def finish(y, z):
        y = jnp.moveaxis(y, 1, 2).reshape(z.shape) * jax.nn.silu(z)
        y = head_rms(y.reshape(*z.shape[:-1], SSD_GROUPS, -1)).reshape(z.shape)
        return y * norm_g
    return finish(yc, zc), finish(yl, zl)


def blockdiag(x, w):
    bn, l, _ = x.shape
    return jnp.einsum('blni,nij->blnj', x.reshape(bn, l, LRU_BLOCKS, LRU_BW), w).reshape(bn, l, LRU_W)


def lru_mixer(u_c, u_l, conv_w, wa, ba, wx, bx, lam):
    bn, l, w = u_l.shape
    rows = l // GRID_W
    u_l = u_l.reshape(bn, rows, GRID_W, w).transpose(0, 2, 1, 3).reshape(bn, l, w)

    def prep(u):
        gate, xr = jnp.split(u, 2, axis=-1)
        xr = conv_centred(xr, conv_w)
        dirs = []
        for d in range(2):
            r = jax.nn.sigmoid(blockdiag(xr, wa[d]) + ba[d])
            i = jax.nn.sigmoid(blockdiag(xr, wx[d]) + bx[d])
            log_a = -LRU_C * jax.nn.softplus(-lam[d]) * r
            dirs.append((jnp.exp(log_a), jnp.sqrt(-jnp.expm1(2.0 * log_a)) * (i * xr)))
        return gate, dirs

    gc, dc = prep(u_c)
    gl, dl = prep(u_l)
    h0 = jnp.zeros((bn, LRU_W), jnp.float32)
    yc, yl = 0.0, 0.0
    for d, rev in enumerate((False, True)):
        oc, ol = two_segment(lru_run, dc[d], dl[d], h0, rev, 1)
        yc, yl = yc + oc, yl + ol
    yc = yc * jax.nn.gelu(gc)
    yl = yl * jax.nn.gelu(gl)
    yl = yl.reshape(bn, GRID_W, rows, LRU_W).transpose(0, 2, 1, 3).reshape(bn, l, LRU_W)
    return yc, yl


def mlstm_mixer(u_c, u_l, b_i, b_f):
    h = ML_HEADS

    def prep(u):
        q, k, v, o, ig, fg = jnp.split(u, [ML_W, 2 * ML_W, 3 * ML_W, 4 * ML_W, 4 * ML_W + 2 * h], axis=-1)
        dirs = []
        for d in range(2):
            li = jnp.moveaxis(ig[..., d * h:(d + 1) * h] + b_i[d], 1, 2)
            lf = jax.nn.log_sigmoid(jnp.moveaxis(fg[..., d * h:(d + 1) * h] + b_f[d], 1, 2))
            dirs.append((li, lf))
        return to_heads(q, h), to_heads(k, h) * ML_DH ** -0.5, to_heads(v, h), o, dirs

    qc, kc, vc, oc_g, dc = prep(u_c)
    ql, kl, vl, ol_g, dl = prep(u_l)
    bn = u_c.shape[0]
    s0 = (jnp.zeros((bn, h, ML_DH, ML_DH), jnp.float32), jnp.zeros((bn, h, ML_DH), jnp.float32),
          jnp.full((bn, h), STATE_NEG, jnp.float32))
    yc, yl = 0.0, 0.0
    for d, rev in enumerate((False, True)):
        oc, ol = two_segment(chunked_mlstm, (qc, kc, vc, *dc[d]), (ql, kl, vl, *dl[d]), s0, rev, 2)
        yc, yl = yc + oc, yl + ol

    def finish(y, o):
        return head_rms(jnp.moveaxis(y, 1, 2)).reshape(o.shape) * jax.nn.sigmoid(o)
    return finish(yc, oc_g), finish(yl, ol_g)


def token_mixers(h_c, h_l, w_in, w_out, lb, ssd_conv_w, ssd_a_log, ssd_dt_bias, ssd_d, ssd_norm_g,
                 lru_conv_w, lru_wa, lru_ba, lru_wx, lru_bx, lru_lambda, ml_b_i, ml_b_f):
    u_c = (h_c @ w_in).astype(jnp.float32)
    u_l = (h_l @ w_in).astype(jnp.float32)
    cuts = [HG_COLS, HG_COLS + SSD_COLS, HG_COLS + SSD_COLS + LRU_COLS]
    pc = jnp.split(u_c, cuts, axis=-1)
    pl = jnp.split(u_l, cuts, axis=-1)
    outs = [hgrn2_mixer(pc[0], pl[0], lb),
            ssd_mixer(pc[1], pl[1], ssd_conv_w, ssd_a_log, ssd_dt_bias, ssd_d, ssd_norm_g),
            lru_mixer(pc[2], pl[2], lru_conv_w, lru_wa, lru_ba, lru_wx, lru_bx, lru_lambda),
            mlstm_mixer(pc[3], pl[3], ml_b_i, ml_b_f)]
    y_c = jnp.concatenate([o[0] for o in outs], axis=-1)
    y_l = jnp.concatenate([o[1] for o in outs], axis=-1)
    return ((y_c.astype(w_out.dtype) @ w_out).astype(h_c.dtype),
            (y_l.astype(w_out.dtype) @ w_out).astype(h_l.dtype))


def hier_moe(t, w_gr, w_er, w_gate, w_up, w_down):
    n_tok, d = t.shape
    g_logits = (t @ w_gr).astype(jnp.float32)
    g_prob = jax.nn.softmax(g_logits, axis=-1)
    g_sel = jnp.argmax(g_logits, axis=-1)
    e_logits = (t @ w_er).astype(jnp.float32).reshape(n_tok, N_EXPERT_GROUPS, EXPERTS_PER_GROUP)
    within = jnp.take_along_axis(e_logits, g_sel[:, None, None], axis=1)[:, 0]
    top_v, top_i = lax.top_k(within, TOP_K)
    wts = jax.nn.softmax(top_v, axis=-1) * jnp.take_along_axis(g_prob, g_sel[:, None], axis=1)
    eid = g_sel[:, None] * EXPERTS_PER_GROUP + top_i

    n_assign = n_tok * TOP_K
    flat_e = eid.reshape(-1)
    order = jnp.argsort(flat_e)
    se = flat_e[order]
    tok = order // TOP_K
    counts = jnp.zeros((N_EXPERTS,), jnp.int32).at[flat_e].add(1)
    padded = (counts + MOE_BLOCK - 1) // MOE_BLOCK * MOE_BLOCK
    pend = jnp.cumsum(padded)
    pstart = pend - padded
    start = jnp.cumsum(counts) - counts
    dest = pstart[se] + jnp.arange(n_assign, dtype=jnp.int32) - start[se]
    n_blk = -(-n_assign // MOE_BLOCK) + N_EXPERTS
    slot_tok = jnp.zeros((n_blk * MOE_BLOCK,), jnp.int32).at[dest].set(tok.astype(jnp.int32))
    blk_start = jnp.arange(n_blk, dtype=jnp.int32) * MOE_BLOCK
    blk_exp = jnp.minimum(jnp.sum(blk_start[:, None] >= pend[None, :], axis=1), N_EXPERTS - 1)

    def run_block(args):
        toks, e = args
        xb = t[toks]
        hid = jax.nn.silu(xb @ w_gate[e]) * (xb @ w_up[e])
        return hid @ w_down[e]

    yb = lax.map(run_block, (slot_tok.reshape(n_blk, MOE_BLOCK), blk_exp)).reshape(-1, d)
    contrib = (yb[dest].astype(jnp.float32) * wts.reshape(-1)[order][:, None]).astype(t.dtype)
    return jnp.zeros_like(t).at[tok].add(contrib)


def setup_inputs(seed: int = 0) -> dict:
    key = jax.random.key(seed)
    ks = iter(jax.random.split(key, 40))
    f32 = jnp.float32
    d = D_MODEL

    def nrm(shape, scale):
        return jax.random.normal(next(ks), shape, f32) * scale

    def unif(shape, lo, hi):
        return jax.random.uniform(next(ks), shape, f32, minval=lo, maxval=hi)

    x = nrm((BATCH, SEQ, d), 1.0)
    c = nrm((BATCH, d), 1.0)
    ctx = nrm((BATCH, CTX_LEN, d), 1.0)
    c_ctx = nrm((d,), 1.0)
    ada_w = nrm((DEPTH, d, 6 * d), 0.5 * d ** -0.5)
    ada_b = nrm((DEPTH, 6 * d), 0.02)
    norm1_g = 1.0 + nrm((DEPTH, d), 0.05)
    norm2_g = 1.0 + nrm((DEPTH, d), 0.05)
    w_in = nrm((DEPTH, d, IN_COLS), d ** -0.5)
    w_out = nrm((DEPTH, MIX_W, d), MIX_W ** -0.5)
    hg_lb_logits = nrm((DEPTH, 2, HG_W), 0.1)
    ssd_conv_w = nrm((DEPTH, SSD_CONV, SSD_CONV_DIM), SSD_CONV ** -0.5)
    ssd_a_log = jnp.log(unif((DEPTH, 2, SSD_HEADS), 1.0, 16.0))
    dt0 = jnp.exp(unif((DEPTH, 2, SSD_HEADS), math.log(1e-3), math.log(1e-1)))
    ssd_dt_bias = dt0 + jnp.log(-jnp.expm1(-dt0))
    ssd_d = 1.0 + nrm((DEPTH, SSD_HEADS), 0.1)
    ssd_norm_g = 1.0 + nrm((DEPTH, SSD_W), 0.05)
    lru_conv_w = nrm((DEPTH, LRU_CONV, LRU_W), LRU_CONV ** -0.5)
    lru_wa = nrm((DEPTH, 2, LRU_BLOCKS, LRU_BW, LRU_BW), LRU_BW ** -0.5)
    lru_ba = nrm((DEPTH, 2, LRU_W), 0.02)
    lru_wx = nrm((DEPTH, 2, LRU_BLOCKS, LRU_BW, LRU_BW), LRU_BW ** -0.5)
    lru_bx = nrm((DEPTH, 2, LRU_W), 0.02)
    a0 = unif((DEPTH, 2, LRU_W), 0.9, 0.999)
    p = a0 ** (1.0 / LRU_C)
    lru_lambda = jnp.log(p) - jnp.log1p(-p)
    ml_b_i = nrm((DEPTH, 2, ML_HEADS), 0.1)
    ml_b_f = unif((DEPTH, 2, ML_HEADS), 3.0, 6.0)
    router_group_w = nrm((DEPTH, d, N_EXPERT_GROUPS), d ** -0.5)
    router_expert_w = nrm((DEPTH, d, N_EXPERTS), d ** -0.5)
    moe_w_gate = nrm((DEPTH, N_EXPERTS, d, D_EXPERT), d ** -0.5)
    moe_w_up = nrm((DEPTH, N_EXPERTS, d, D_EXPERT), d ** -0.5)
    moe_w_down = nrm((DEPTH, N_EXPERTS, D_EXPERT, d), D_EXPERT ** -0.5)
    final_norm_g = 1.0 + nrm((d,), 0.05)
    return {'x': x, 'c': c, 'ctx': ctx, 'c_ctx': c_ctx, 'ada_w': ada_w, 'ada_b': ada_b,
            'norm1_g': norm1_g, 'norm2_g': norm2_g, 'w_in': w_in, 'w_out': w_out,
            'hg_lb_logits': hg_lb_logits, 'ssd_conv_w': ssd_conv_w, 'ssd_a_log': ssd_a_log,
            'ssd_dt_bias': ssd_dt_bias, 'ssd_d': ssd_d, 'ssd_norm_g': ssd_norm_g,
            'lru_conv_w': lru_conv_w, 'lru_wa': lru_wa, 'lru_ba': lru_ba, 'lru_wx': lru_wx,
            'lru_bx': lru_bx, 'lru_lambda': lru_lambda, 'ml_b_i': ml_b_i, 'ml_b_f': ml_b_f,
            'router_group_w': router_group_w, 'router_expert_w': router_expert_w,
            'moe_w_gate': moe_w_gate, 'moe_w_up': moe_w_up, 'moe_w_down': moe_w_down,
            'final_norm_g': final_norm_g}


def reference(x, c, ctx, c_ctx, ada_w, ada_b, norm1_g, norm2_g, w_in, w_out, hg_lb_logits,
              ssd_conv_w, ssd_a_log, ssd_dt_bias, ssd_d, ssd_norm_g, lru_conv_w, lru_wa, lru_ba,
              lru_wx, lru_bx, lru_lambda, ml_b_i, ml_b_f, router_group_w, router_expert_w,
              moe_w_gate, moe_w_up, moe_w_down, final_norm_g):
    d = x.shape[-1]
    p = jax.nn.softmax(hg_lb_logits.astype(jnp.float32), axis=0)
    lb_all = jnp.cumsum(p, axis=0) - p[:1]
    s_lat = jax.nn.silu(c)
    s_ctx = jax.nn.silu(c_ctx)
    xl, xc = x, ctx
    for l in range(DEPTH):
        mod_l = jnp.split((s_lat @ ada_w[l] + ada_b[l])[:, None, :], 6, axis=-1)
        mod_c = jnp.split(s_ctx @ ada_w[l] + ada_b[l], 6, axis=-1)
        hl = rmsnorm(xl, norm1_g[l]) * (1.0 + mod_l[1]) + mod_l[0]
        hc = rmsnorm(xc, norm1_g[l]) * (1.0 + mod_c[1]) + mod_c[0]
        yc, yl = token_mixers(hc, hl, w_in[l], w_out[l], lb_all[l], ssd_conv_w[l], ssd_a_log[l],
                              ssd_dt_bias[l], ssd_d[l], ssd_norm_g[l], lru_conv_w[l], lru_wa[l],
                              lru_ba[l], lru_wx[l], lru_bx[l], lru_lambda[l], ml_b_i[l], ml_b_f[l])
        xl = xl + mod_l[2] * yl
        hl = rmsnorm(xl, norm2_g[l]) * (1.0 + mod_l[4]) + mod_l[3]
        moe_args = (router_group_w[l], router_expert_w[l], moe_w_gate[l], moe_w_up[l], moe_w_down[l])
        if l < DEPTH - 1:
            xc = xc + mod_c[2] * yc
            hc = rmsnorm(xc, norm2_g[l]) * (1.0 + mod_c[4]) + mod_c[3]
            n_ctx = xc.shape[0] * xc.shape[1]
            y = hier_moe(jnp.concatenate([hc.reshape(-1, d), hl.reshape(-1, d)], axis=0), *moe_args)
            xc = xc + mod_c[5] * y[:n_ctx].reshape(xc.shape)
            xl = xl + mod_l[5] * y[n_ctx:].reshape(xl.shape)
        else:
            xl = xl + mod_l[5] * hier_moe(hl.reshape(-1, d), *moe_args).reshape(xl.shape)
    return rmsnorm(xl, final_norm_g)
```

```python
import functools
import math

import jax
import jax.numpy as jnp
from jax import lax
from jax.experimental import pallas as pl
from jax.experimental.pallas import tpu as pltpu

F32 = jnp.float32
BF16 = jnp.bfloat16

GRID_W = 64
HG_HEADS, HG_DK = 4, 128
SSD_HEADS, SSD_HEADDIM, SSD_GROUPS, SSD_STATE = 8, 64, 2, 128
LRU_C = 8.0
ML_HEADS, ML_DH = 4, 128
N_EXPERT_GROUPS, EXPERTS_PER_GROUP, TOP_K = 4, 8, 2
MOE_BLOCK = 256
NORM_EPS = 1e-6
NEG = -1e30

GW = 512
C_HG_Q, C_HG_I, C_HG_G, C_HG_ZF, C_HG_ZB = 0, 512, 1024, 1536, 2048
C_SSD_Z, C_SSD_XBC = 2560, 3072
C_LRU_GATE, C_LRU_XR = 4096, 4608
C_ML_Q, C_ML_K, C_ML_V, C_ML_O = 5120, 5632, 6144, 6656
C_SMALL = 7168
SM_DT, SM_IG, SM_FG = 0, 16, 24
NCOLS = 7680

LANE = 128
SUB = 8
VMEM_LIMIT = 48 * 1024 * 1024

T_CHUNK = 128


def _cparams(sem):
    return pltpu.CompilerParams(dimension_semantics=sem, vmem_limit_bytes=VMEM_LIMIT)


def _dot(a, b):
    return jnp.dot(a, b, preferred_element_type=F32)


def _dot_nt(a, b):
    return lax.dot_general(a, b, (((1,), (1,)), ((), ())), preferred_element_type=F32)


def _dot_tn(a, b):
    return lax.dot_general(a, b, (((0,), (0,)), ((), ())), preferred_element_type=F32)


def _bf(x):
    return x.astype(BF16)


def _sigmoid(x):
    return 1.0 / (1.0 + jnp.exp(-x))


def _softplus(x):
    return jnp.maximum(x, 0.0) + jnp.log1p(jnp.exp(-jnp.abs(x)))


def _split_dot(mask_bf, x):
    hi = _bf(x)
    r1 = x - hi.astype(F32)
    mid = _bf(r1)
    lo = _bf(r1 - mid.astype(F32))
    return _dot(mask_bf, hi) + _dot(mask_bf, mid) + _dot(mask_bf, lo)


def _rollb(x, n, rev):
    if n == 0:
        return x
    t = x.shape[0]
    return pltpu.roll(x, (t - n) if rev else n, 0)


def _iota2(t):
    return (lax.broadcasted_iota(jnp.int32, (t, t), 0), lax.broadcasted_iota(jnp.int32, (t, t), 1))


def _mod_kernel(c_ref, w_ref, b_ref, o_ref):
    c = c_ref[...]
    s = c * _sigmoid(c)
    o_ref[0] = jnp.dot(s, w_ref[0], precision=lax.Precision.HIGHEST,
                       preferred_element_type=F32) + b_ref[0]


def _modulation(c_all, ada_w, ada_b):
    depth, d, n6 = ada_w.shape
    tn = 1024
    return pl.pallas_call(
        _mod_kernel,
        grid=(depth, n6 // tn),
        in_specs=[pl.BlockSpec((SUB, d), lambda l, j: (0, 0)),
                  pl.BlockSpec((1, d, tn), lambda l, j: (l, 0, j)),
                  pl.BlockSpec((1, 1, tn), lambda l, j: (l, 0, j))],
        out_specs=pl.BlockSpec((1, SUB, tn), lambda l, j: (l, 0, j)),
        out_shape=jax.ShapeDtypeStruct((depth, SUB, n6), F32),
        compiler_params=_cparams(("parallel", "parallel")),
        name="adaln_modulation",
    )(c_all, ada_w, ada_b.reshape(depth, 1, n6))


def _rms_mod(x, g, shift, scale):
    y = x * lax.rsqrt(jnp.mean(x * x, axis=-1, keepdims=True) + NORM_EPS) * g
    return y * (1.0 + scale) + shift


def _inproj_kernel(*refs, residual):
    if residual:
        x_ref, y_ref, pmod_ref, g_ref, mod_ref, w_ref, u_ref, xn_ref, h_scr = refs
    else:
        x_ref, g_ref, mod_ref, w_ref, u_ref, h_scr = refs

    @pl.when(pl.program_id(1) == 0)
    def _():
        x = x_ref[...]
        if residual:
            x = x + pmod_ref[0, 5:6, :] * y_ref[...]
            xn_ref[...] = x
        h_scr[...] = _bf(_rms_mod(x, g_ref[...], mod_ref[0, 0:1, :], mod_ref[0, 1:2, :]))

    u_ref[...] = _dot(h_scr[...], w_ref[...])


def _mod_row_map(tm, n_lat, seq, b):
    def f(i, *_):
        return (jnp.where(i * tm < n_lat, (i * tm) // seq, b), 0, 0)
    return f


def _in_proj(x, y, pmod, g1, mod, w, *, n_lat, seq, b):
    ntot, d = x.shape
    ncols = w.shape[1]
    tm, tn = 512, 512
    residual = y is not None
    row = lambda i, j: (i, 0)
    mrow = _mod_row_map(tm, n_lat, seq, b)
    in_specs = [pl.BlockSpec((tm, d), row)]
    args = [x]
    if residual:
        in_specs += [pl.BlockSpec((tm, d), row), pl.BlockSpec((1, 6, d), mrow)]
        args += [y, pmod]
    in_specs += [pl.BlockSpec((1, d), lambda i, j: (0, 0)),
                 pl.BlockSpec((1, 6, d), mrow),
                 pl.BlockSpec((d, tn), lambda i, j: (0, j))]
    args += [g1, mod, w]
    out_specs = [pl.BlockSpec((tm, tn), lambda i, j: (i, j))]
    out_shape = [jax.ShapeDtypeStruct((ntot, ncols), F32)]
    if residual:
        out_specs.append(pl.BlockSpec((tm, d), row))
        out_shape.append(jax.ShapeDtypeStruct((ntot, d), F32))
    res = pl.pallas_call(
        functools.partial(_inproj_kernel, residual=residual),
        grid=(ntot // tm, ncols // tn),
        in_specs=in_specs, out_specs=out_specs, out_shape=out_shape,
        scratch_shapes=[pltpu.VMEM((tm, d), BF16)],
        compiler_params=_cparams(("parallel", "arbitrary")),
        name="norm1_in_proj",
    )(*args)
    return (res[0], res[1]) if residual else (res[0], x)


def _chunk_nat(i, rev, nc, nl):
    is_ctx = i < nc
    if rev:
        c = jnp.where(is_ctx, nc - 1 - i, nl - 1 - (i - nc))
    else:
        c = jnp.where(is_ctx, i, i - nc)
    return is_ctx, c


def _row_block(b, i, *, rev, nc, nl, nb):
    is_ctx, c = _chunk_nat(i, rev, nc, nl)
    return jnp.where(is_ctx, nb * nl + b * nc + c, b * nl + c)


def _col_spec(t, width, col, rb):
    cb = col // width
    return pl.BlockSpec((t, width), lambda b, i: (rb(b, i), cb))


def _hg_kernel(*refs, rev, finish):
    if finish:
        q_ref, v_ref, z_ref, lb_ref, g_ref, of_ref, o_ref, st_scr = refs
    else:
        q_ref, v_ref, z_ref, lb_ref, o_ref, st_scr = refs
    t = q_ref.shape[0]
    dk = HG_DK

    @pl.when(pl.program_id(1) == 0)
    def _():
        st_scr[...] = jnp.zeros_like(st_scr)

    lb = lb_ref[0]
    sig = _sigmoid(z_ref[...])
    f = lb + (1.0 - lb) * sig
    lf = jnp.log(f)
    k = (1.0 - lb) * (1.0 - sig)
    q = q_ref[...]
    v = v_ref[...]

    row, col = _iota2(t)
    valid = (col >= row) if rev else (col <= row)
    x = row ^ col
    b = _split_dot(_bf(valid.astype(F32)), lf)
    total = jnp.sum(lf, axis=0, keepdims=True)

    levels = []
    h = t // 2
    while h >= SUB:
        blk = 2 * h
        b3 = b.reshape(t // blk, blk, b.shape[-1])
        ref = b3[:, h:h + 1, :] if rev else b3[:, h - 1:h, :]
        ref = jnp.broadcast_to(ref, b3.shape).reshape(b.shape)
        eq = q * jnp.exp(jnp.minimum(b - ref, 0.0))
        ek = k * jnp.exp(jnp.minimum(ref - b, 0.0))
        levels.append((int(math.log2(h)), _bf(eq), _bf(ek)))
        h //= 2

    sgn = -1 if rev else 1
    band = []
    d = None
    for dlt in range(SUB):
        if dlt == 0:
            prod = q * k
        else:
            fs = _rollb(f, dlt - 1, rev)
            d = fs if d is None else d * fs
            prod = q * d * _rollb(k, dlt, rev)
        band.append(prod)
    diffcode = jnp.where(x < SUB, row - col, 4 * t)

    qb = _bf(q * jnp.exp(b))
    kw = _bf(k * jnp.exp(total - b))
    vb = _bf(v)
    dec_tot = jnp.exp(total)

    outs = []
    for hd in range(HG_HEADS):
        sl = slice(hd * dk, (hd + 1) * dk)
        a = jnp.zeros((t, t), F32)
        for lg, eq, ek in levels:
            p = _dot_nt(eq[:, sl], ek[:, sl])
            a = jnp.where((x >> lg) == 1, p, a)
        for dlt in range(SUB):
            w = jnp.sum(band[dlt][:, sl], axis=-1, keepdims=True)
            a = jnp.where(diffcode == sgn * dlt, w, a)
        a = jnp.where(valid, a, 0.0)
        st = st_scr[hd]
        o = _dot(_bf(a), vb[:, sl]) + _dot_nt(qb[:, sl], _bf(st))
        st_scr[hd] = st * dec_tot[:, sl] + _dot_tn(vb[:, sl], kw[:, sl])
        outs.append(o)

    if finish:
        for hd in range(HG_HEADS):
            sl = slice(hd * dk, (hd + 1) * dk)
            tot = outs[hd] + of_ref[:, sl]
            y = tot * lax.rsqrt(jnp.mean(tot * tot, axis=-1, keepdims=True) + NORM_EPS)
            g = g_ref[:, sl]
            o_ref[:, sl] = (y * (g * _sigmoid(g))).astype(o_ref.dtype)
    else:
        for hd in range(HG_HEADS):
            o_ref[:, hd * dk:(hd + 1) * dk] = outs[hd]


def _hg_pass(u, lb, o_f, *, rev, nb, nc, nl):
    t = T_CHUNK
    ntot = u.shape[0]
    finish = o_f is not None
    rb = functools.partial(_row_block, rev=rev, nc=nc, nl=nl, nb=nb)
    d = 1 if rev else 0
    in_specs = [_col_spec(t, GW, C_HG_Q, rb), _col_spec(t, GW, C_HG_I, rb),
                _col_spec(t, GW, C_HG_ZB if rev else C_HG_ZF, rb),
                pl.BlockSpec((1, 1, GW), lambda b, i: (d, 0, 0))]
    args = [u, u, u, lb]
    if finish:
        in_specs += [_col_spec(t, GW, C_HG_G, rb), _col_spec(t, GW, 0, rb)]
        args += [u, o_f]
    return pl.pallas_call(
        functools.partial(_hg_kernel, rev=rev, finish=finish),
        grid=(nb, nc + nl),
        in_specs=in_specs,
        out_specs=_col_spec(t, GW, 0, rb),
        out_shape=jax.ShapeDtypeStruct((ntot, GW), BF16 if finish else F32),
        scratch_shapes=[pltpu.VMEM((HG_HEADS, HG_DK, HG_DK), F32)],
        compiler_params=_cparams(("parallel", "arbitrary")),
        name="hgrn2_bwd_finish" if rev else "hgrn2_fwd",
    )(*args)


def _ml_kernel(*refs, rev, finish):
    if finish:
        q_ref, k_ref, v_ref, sm_ref, sb_ref, og_ref, of_ref, o_ref, ct_scr, n_scr, m_scr = refs
    else:
        q_ref, k_ref, v_ref, sm_ref, sb_ref, o_ref, ct_scr, n_scr, m_scr = refs
    t = q_ref.shape[0]
    dh = ML_DH
    d = 1 if rev else 0

    @pl.when(pl.program_id(1) == 0)
    def _():
        ct_scr[...] = jnp.zeros_like(ct_scr)
        n_scr[...] = jnp.zeros_like(n_scr)
        m_scr[...] = jnp.full(m_scr.shape, NEG, F32)

    sm = sm_ref[...] + sb_ref[...]
    lsg = jnp.minimum(sm, 0.0) - jnp.log1p(jnp.exp(-jnp.abs(sm)))
    row, col = _iota2(t)
    valid = (col >= row) if rev else (col <= row)
    bcol = _split_dot(_bf(valid.astype(F32)), lsg)
    tot_row = jnp.sum(lsg, axis=0, keepdims=True)
    b_t = bcol.T
    sm_t = sm.T

    q = q_ref[...]
    ks = k_ref[...] * (dh ** -0.5)
    qb, kb, vb = _bf(q), _bf(ks), _bf(v_ref[...])

    outs = []
    for hd in range(ML_HEADS):
        sl = slice(hd * dh, (hd + 1) * dh)
        ci = SM_IG + ML_HEADS * d + hd
        cf = SM_FG + ML_HEADS * d + hd
        bc = bcol[:, cf:cf + 1]
        br = b_t[cf:cf + 1, :]
        lir = sm_t[ci:ci + 1, :]
        lic = sm[:, ci:ci + 1]
        m_prev = m_scr[hd:hd + 1, 0:1]
        dmat = jnp.where(valid, bc - br + lir, NEG)
        g = bc + m_prev
        mt = jnp.maximum(g, jnp.max(dmat, axis=-1, keepdims=True))
        w_intra = jnp.exp(dmat - mt)
        w_inter = jnp.exp(g - mt)
        s = _dot_nt(qb[:, sl], kb[:, sl]) * w_intra
        ct = ct_scr[hd]
        nrow = n_scr[hd:hd + 1, :]
        num = _dot(_bf(s), vb[:, sl]) + w_inter * _dot_nt(qb[:, sl], _bf(ct))
        den = (jnp.sum(s, axis=-1, keepdims=True)
               + w_inter * jnp.sum(q[:, sl] * nrow, axis=-1, keepdims=True))
        outs.append(num / jnp.maximum(jnp.abs(den), jnp.exp(-mt)))
        tot = tot_row[:, cf:cf + 1]
        dl = tot - bc + lic
        m_new = jnp.maximum(tot + m_prev, jnp.max(dl, axis=0, keepdims=True))
        ws = jnp.exp(dl - m_new)
        wc = jnp.exp(tot + m_prev - m_new)
        kws = ks[:, sl] * ws
        ct_scr[hd] = wc * ct + _dot_tn(vb[:, sl], _bf(kws))
        n_scr[hd:hd + 1, :] = wc * nrow + jnp.sum(kws, axis=0, keepdims=True)
        m_scr[hd:hd + 1, :] = jnp.broadcast_to(m_new, (1, m_scr.shape[1]))

    for hd in range(ML_HEADS):
        sl = slice(hd * dh, (hd + 1) * dh)
        if finish:
            tot = outs[hd] + of_ref[:, sl]
            y = tot * lax.rsqrt(jnp.mean(tot * tot, axis=-1, keepdims=True) + NORM_EPS)
            o_ref[:, sl] = (y * _sigmoid(og_ref[:, sl])).astype(o_ref.dtype)
        else:
            o_ref[:, sl] = outs[hd]


def _ml_pass(u, sbias, o_f, *, rev, nb, nc, nl):
    t = T_CHUNK
    ntot = u.shape[0]
    finish = o_f is not None
    rb = functools.partial(_row_block, rev=rev, nc=nc, nl=nl, nb=nb)
    in_specs = [_col_spec(t, GW, C_ML_Q, rb), _col_spec(t, GW, C_ML_K, rb),
                _col_spec(t, GW, C_ML_V, rb), _col_spec(t, LANE, C_SMALL, rb),
                pl.BlockSpec((1, LANE), lambda b, i: (0, 0))]
    args = [u, u, u, u, sbias]
    if finish:
        in_specs += [_col_spec(t, GW, C_ML_O, rb), _col_spec(t, GW, 0, rb)]
        args += [u, o_f]
    return pl.pallas_call(
        functools.partial(_ml_kernel, rev=rev, finish=finish),
        grid=(nb, nc + nl),
        in_specs=in_specs,
        out_specs=_col_spec(t, GW, 0, rb),
        out_shape=jax.ShapeDtypeStruct((ntot, GW), BF16 if finish else F32),
        scratch_shapes=[pltpu.VMEM((ML_HEADS, ML_DH, ML_DH), F32),
                        pltpu.VMEM((SUB, ML_DH), F32),
                        pltpu.VMEM((SUB, LANE), F32)],
        compiler_params=_cparams(("parallel", "arbitrary")),
        name="mlstm_bwd_finish" if rev else "mlstm_fwd",
    )(*args)


def _conv4(x, prev8, next8, w, first, last):
    t = x.shape[0]
    prev8 = jnp.where(first, 0.0, prev8)
    next8 = jnp.where(last, 0.0, next8)
    xe = jnp.concatenate([prev8, x, next8], axis=0)
    out = w[0:1, :] * xe[SUB - 2:SUB - 2 + t]
    for j in range(1, 4):
        out = out + w[j:j + 1, :] * xe[SUB - 2 + j:SUB - 2 + j + t]
    return out


def _ssd_kernel(*refs, rev, finish, nc, nl):
    if finish:
        (x_ref, xp_ref, xn_ref, sm_ref, sb_ref, arow_ref, cw_ref,
         z_ref, of_ref, dsk_ref, ng_ref, o_ref, st_scr) = refs
    else:
        x_ref, xp_ref, xn_ref, sm_ref, sb_ref, arow_ref, cw_ref, o_ref, st_scr = refs
    t = x_ref.shape[0]
    p = SSD_HEADDIM
    n = SSD_STATE
    d = 1 if rev else 0
    i = pl.program_id(1)

    @pl.when(i == 0)
    def _():
        st_scr[...] = jnp.zeros_like(st_scr)

    is_ctx, c = _chunk_nat(i, rev, nc, nl)
    first = c == 0
    last = c == jnp.where(is_ctx, nc, nl) - 1
    xc = _conv4(x_ref[...], xp_ref[...], xn_ref[...], cw_ref[...], first, last)
    xc = xc * _sigmoid(xc)
    xs = xc[:, :SSD_HEADS * p]
    bm = xc[:, SSD_HEADS * p:SSD_HEADS * p + SSD_GROUPS * n]
    cm = xc[:, SSD_HEADS * p + SSD_GROUPS * n:]

    dl = _softplus(sm_ref[...] + sb_ref[...])
    lfm = dl * arow_ref[...]
    row, col = _iota2(t)
    valid = (col >= row) if rev else (col <= row)
    bcol = _split_dot(_bf(valid.astype(F32)), lfm)
    tot_row = jnp.sum(lfm, axis=0, keepdims=True)
    b_t = bcol.T
    dl_t = dl.T

    xsb = _bf(xs)
    rep = SSD_HEADS // SSD_GROUPS
    outs = []
    for gi in range(SSD_GROUPS):
        bg = bm[:, gi * n:(gi + 1) * n]
        cgb = _bf(cm[:, gi * n:(gi + 1) * n])
        gmat = _dot_nt(cgb, _bf(bg))
        for hh in range(rep):
            hd = gi * rep + hh
            cc = SM_DT + SSD_HEADS * d + hd
            bc = bcol[:, cc:cc + 1]
            br = b_t[cc:cc + 1, :]
            dr = dl_t[cc:cc + 1, :]
            dcol = dl[:, cc:cc + 1]
            tot = tot_row[:, cc:cc + 1]
            att = gmat * jnp.exp(jnp.where(valid, bc - br, NEG)) * dr
            xh = xsb[:, hd * p:(hd + 1) * p]
            st = st_scr[hd]
            o = _dot(_bf(att), xh) + jnp.exp(bc) * _dot_nt(cgb, _bf(st))
            wk = dcol * jnp.exp(tot - bc)
            st_scr[hd] = jnp.exp(tot) * st + _dot_tn(xh, _bf(bg * wk))
            outs.append(o)
    o_all = jnp.concatenate(outs, axis=-1)

    if finish:
        y = (o_all + of_ref[...] + dsk_ref[...] * xs)
        z = z_ref[...]
        y = y * (z * _sigmoid(z))
        gw = SSD_HEADS * p // SSD_GROUPS
        for gi in range(SSD_GROUPS):
            sl = slice(gi * gw, (gi + 1) * gw)
            yg = y[:, sl]
            yg = yg * lax.rsqrt(jnp.mean(yg * yg, axis=-1, keepdims=True) + NORM_EPS)
            o_ref[:, sl] = (yg * ng_ref[:, sl]).astype(o_ref.dtype)
    else:
        o_ref[...] = o_all


def _ssd_pass(u, sbias, arow, conv_w, z_extra, o_f, *, rev, nb, nc, nl):
    t = T_CHUNK
    ntot = u.shape[0]
    finish = o_f is not None
    rb = functools.partial(_row_block, rev=rev, nc=nc, nl=nl, nb=nb)
    r8 = t // SUB
    nblk8 = ntot // SUB
    cb = C_SSD_XBC // 1024

    def prev_map(b, i):
        return (jnp.maximum(rb(b, i) * r8 - 1, 0), cb)

    def next_map(b, i):
        return (jnp.minimum((rb(b, i) + 1) * r8, nblk8 - 1), cb)

    const2 = lambda b, i: (0, 0)
    in_specs = [_col_spec(t, 1024, C_SSD_XBC, rb),
                pl.BlockSpec((SUB, 1024), prev_map), pl.BlockSpec((SUB, 1024), next_map),
                _col_spec(t, LANE, C_SMALL, rb),
                pl.BlockSpec((1, LANE), const2), pl.BlockSpec((1, LANE), const2),
                pl.BlockSpec((4, 1024), const2)]
    args = [u, u, u, u, sbias, arow, conv_w]
    if finish:
        dsk, ng = z_extra
        in_specs += [_col_spec(t, GW, C_SSD_Z, rb), _col_spec(t, GW, 0, rb),
                     pl.BlockSpec((1, GW), const2), pl.BlockSpec((1, GW), const2)]
        args += [u, o_f, dsk, ng]
    return pl.pallas_call(
        functools.partial(_ssd_kernel, rev=rev, finish=finish, nc=nc, nl=nl),
        grid=(nb, nc + nl),
        in_specs=in_specs,
        out_specs=_col_spec(t, GW, 0, rb),
        out_shape=jax.ShapeDtypeStruct((ntot, GW), BF16 if finish else F32),
        scratch_shapes=[pltpu.VMEM((SSD_HEADS, SSD_HEADDIM, SSD_STATE), F32)],
        compiler_params=_cparams(("parallel", "arbitrary")),
        name="ssd_bwd_finish" if rev else "ssd_fwd",
    )(*args)


def _lru_kernel(*refs, rev, finish, nchunks):
    if finish:
        (x_ref, xp_ref, xn_ref, cw_ref, w2_ref, b2_ref, crow_ref, h0_ref,
         gate_ref, of_ref, o_ref, ht_ref, h_scr) = refs
    else:
        x_ref, xp_ref, xn_ref, cw_ref, w2_ref, b2_ref, crow_ref, h0_ref, o_ref, ht_ref, h_scr = refs
    t = x_ref.shape[0]
    w = x_ref.shape[1]
    i = pl.program_id(1)

    @pl.when(i == 0)
    def _():
        h_scr[...] = h0_ref[0]

    c = (nchunks - 1 - i) if rev else i
    xr = _conv4(x_ref[...], xp_ref[...], xn_ref[...], cw_ref[...], c == 0, c == nchunks - 1)
    pre = _dot(_bf(xr), w2_ref[0]) + b2_ref[0]
    r = _sigmoid(pre[:, :w])
    ig = _sigmoid(pre[:, w:])
    log_a = crow_ref[0] * r
    a = jnp.exp(log_a)
    bb = jnp.sqrt(jnp.tanh(-log_a) * (1.0 + a * a)) * (ig * xr)

    tpos = lax.broadcasted_iota(jnp.int32, (t, w), 0)
    pos = (t - 1 - tpos) if rev else tpos
    step = 1
    while step < t:
        a_s = _rollb(a, step, rev)
        b_s = _rollb(bb, step, rev)
        ok = pos >= step
        bb = jnp.where(ok, a * b_s + bb, bb)
        a = jnp.where(ok, a * a_s, a)
        step *= 2
    hs = bb + a * h_scr[...]
    h_last = hs[0:1, :] if rev else hs[t - 1:t, :]
    h_scr[...] = h_last

    @pl.when(i == nchunks - 1)
    def _():
        ht_ref[0] = h_last

    if finish:
        g = gate_ref[...]
        gelu = 0.5 * g * (1.0 + jnp.tanh(math.sqrt(2.0 / math.pi) * (g + 0.044715 * (g * g * g))))
        o_ref[...] = ((hs + of_ref[...]) * gelu).astype(o_ref.dtype)
    else:
        o_ref[...] = hs


def _lru_pass(views, consts, h0, o_f_view, *, rev, nb, nchunks, out_view):
    t = T_CHUNK
    finish = o_f_view is not None
    d = 1 if rev else 0
    arr, main_map, prev_map, next_map, gate_map = views
    cw, w2, b2, crow = consts
    const2 = lambda b, i: (0, 0)
    dsel = lambda b, i: (d, 0, 0)
    in_specs = [pl.BlockSpec((t, GW), main_map), pl.BlockSpec((SUB, GW), prev_map),
                pl.BlockSpec((SUB, GW), next_map), pl.BlockSpec((4, GW), const2),
                pl.BlockSpec((1, GW, 2 * GW), dsel), pl.BlockSpec((1, 1, 2 * GW), dsel),
                pl.BlockSpec((1, 1, GW), dsel), pl.BlockSpec((1, 1, GW), lambda b, i: (b, 0, 0))]
    args = [arr, arr, arr, cw, w2, b2, crow, h0]
    out_arr_shape, out_map = out_view
    if finish:
        in_specs += [pl.BlockSpec((t, GW), gate_map), pl.BlockSpec((t, GW), out_map)]
        args += [arr, o_f_view]
    o, ht = pl.pallas_call(
        functools.partial(_lru_kernel, rev=rev, finish=finish, nchunks=nchunks),
        grid=(nb, nchunks),
        in_specs=in_specs,
        out_specs=[pl.BlockSpec((t, GW), out_map), pl.BlockSpec((1, 1, GW), lambda b, i: (b, 0, 0))],
        out_shape=[jax.ShapeDtypeStruct(out_arr_shape, BF16 if finish else F32),
                   jax.ShapeDtypeStruct((nb, 1, GW), F32)],
        scratch_shapes=[pltpu.VMEM((1, GW), F32)],
        compiler_params=_cparams(("parallel", "arbitrary")),
        name="rglru_bwd_finish" if rev else "rglru_fwd",
    )(*args)
    return o, ht


def _lru_mixer(u, consts, *, nb, seq, ctx):
    t = T_CHUNK
    ntot, ncols = u.shape
    n_lat = nb * seq
    rows = seq // GRID_W
    assert rows == t, "one scan chunk must be one latent grid column"
    ncc = ctx // t
    cpb = ncols // GW
    r8 = t // SUB
    u3 = u.reshape(ntot // GRID_W, GRID_W * ncols)
    xr_cb, gate_cb = C_LRU_XR // GW, C_LRU_GATE // GW

    outs = []
    h0 = jnp.zeros((nb, 1, GW), F32)
    o_f_c = o_f_l = None
    for rev in (False, True):
        jn = (lambda i: GRID_W - 1 - i) if rev else (lambda i: i)
        cn = (lambda i: ncc - 1 - i) if rev else (lambda i: i)
        base8 = n_lat // SUB
        cmain = lambda b, i, cb=xr_cb: ((n_lat + b * ctx) // t + cn(i), cb)
        cgate = lambda b, i: ((n_lat + b * ctx) // t + cn(i), gate_cb)
        cprev = lambda b, i: (jnp.maximum(base8 + (b * ctx) // SUB + cn(i) * r8 - 1, 0), xr_cb)
        cnext = lambda b, i: (jnp.minimum(base8 + (b * ctx) // SUB + (cn(i) + 1) * r8, ntot // SUB - 1), xr_cb)
        cout = lambda b, i: ((b * ctx) // t + cn(i), 0)
        o_c, h_c = _lru_pass((u, cmain, cprev, cnext, cgate), consts, h0, o_f_c, rev=rev, nb=nb,
                             nchunks=ncc, out_view=((nb * ctx, GW), cout))
        lmain = lambda b, i: (b, jn(i) * cpb + xr_cb)
        lgate = lambda b, i: (b, jn(i) * cpb + gate_cb)
        lprev = lambda b, i: (b * r8 + r8 - 1, jnp.maximum(jn(i) - 1, 0) * cpb + xr_cb)
        lnext = lambda b, i: (b * r8, jnp.minimum(jn(i) + 1, GRID_W - 1) * cpb + xr_cb)
        lout = lambda b, i: (b, jn(i))
        o_l, _ = _lru_pass((u3, lmain, lprev, lnext, lgate), consts, h_c, o_f_l, rev=rev, nb=nb,
                           nchunks=GRID_W, out_view=((n_lat // GRID_W, GRID_W * GW), lout))
        o_f_c, o_f_l = o_c, o_l
    return jnp.concatenate([o_f_l.reshape(n_lat, GW), o_f_c], axis=0)


def _outproj_kernel(yh_ref, ys_ref, yl_ref, ym_ref, w_ref, x_ref, mod_ref, g_ref, wr_ref,
                    xn_ref, h_ref, lg_ref):
    acc = _dot(yh_ref[...], w_ref[0])
    acc = acc + _dot(ys_ref[...], w_ref[1])
    acc = acc + _dot(yl_ref[...], w_ref[2])
    acc = acc + _dot(ym_ref[...], w_ref[3])
    xn = x_ref[...] + mod_ref[0, 2:3, :] * acc
    xn_ref[...] = xn
    h = _rms_mod(xn, g_ref[...], mod_ref[0, 3:4, :], mod_ref[0, 4:5, :])
    h_ref[...] = _bf(h)
    lg_ref[...] = jnp.dot(h, wr_ref[...], precision=lax.Precision.HIGHEST, preferred_element_type=F32)


def _out_proj(ys, w_out4, x, mod, g2, w_router, *, n_lat, seq, b):
    ntot, d = x.shape
    tm = 256
    row = lambda i: (i, 0)
    mrow = _mod_row_map(tm, n_lat, seq, b)
    in_specs = [pl.BlockSpec((tm, GW), row)] * 4 + [
        pl.BlockSpec((4, GW, d), lambda i: (0, 0, 0)),
        pl.BlockSpec((tm, d), row), pl.BlockSpec((1, 6, d), mrow),
        pl.BlockSpec((1, d), lambda i: (0, 0)), pl.BlockSpec((d, LANE), lambda i: (0, 0))]
    return pl.pallas_call(
        _outproj_kernel,
        grid=(ntot // tm,),
        in_specs=in_specs,
        out_specs=[pl.BlockSpec((tm, d), row), pl.BlockSpec((tm, d), row), pl.BlockSpec((tm, LANE), row)],
        out_shape=[jax.ShapeDtypeStruct((ntot, d), F32), jax.ShapeDtypeStruct((ntot, d), BF16),
                   jax.ShapeDtypeStruct((ntot, LANE), F32)],
        compiler_params=_cparams(("parallel",)),
        name="out_proj_norm2_router",
    )(*ys, w_out4, x, mod, g2, w_router)


def _moe_kernel(be_ref, xb_ref, wg_ref, wu_ref, wd_ref, o_ref):
    del be_ref
    xb = xb_ref[...]
    g = _dot(xb, wg_ref[0])
    hid = (g * _sigmoid(g)) * _dot(xb, wu_ref[0])
    o_ref[...] = _dot(_bf(hid), wd_ref[0])


def _moe_blocks(blk_exp, xb, wg, wu, wd):
    nslot, d = xb.shape
    de = wg.shape[-1]
    nblk = nslot // MOE_BLOCK
    return pl.pallas_call(
        _moe_kernel,
        grid_spec=pltpu.PrefetchScalarGridSpec(
            num_scalar_prefetch=1,
            grid=(nblk,),
            in_specs=[pl.BlockSpec((MOE_BLOCK, d), lambda i, be: (i, 0)),
                      pl.BlockSpec((1, d, de), lambda i, be: (be[i], 0, 0)),
                      pl.BlockSpec((1, d, de), lambda i, be: (be[i], 0, 0)),
                      pl.BlockSpec((1, de, d), lambda i, be: (be[i], 0, 0))],
            out_specs=pl.BlockSpec((MOE_BLOCK, d), lambda i, be: (i, 0))),
        out_shape=jax.ShapeDtypeStruct((nslot, d), F32),
        compiler_params=_cparams(("arbitrary",)),
        name="moe_expert_blocks",
    )(blk_exp, xb, wg, wu, wd)


def _route(logits):
    n_tok = logits.shape[0]
    n_exp = N_EXPERT_GROUPS * EXPERTS_PER_GROUP
    g_logits = logits[:, :N_EXPERT_GROUPS]
    g_prob = jax.nn.softmax(g_logits, axis=-1)
    g_sel = jnp.argmax(g_logits, axis=-1)
    e_logits = logits[:, N_EXPERT_GROUPS:N_EXPERT_GROUPS + n_exp].reshape(
        n_tok, N_EXPERT_GROUPS, EXPERTS_PER_GROUP)
    within = jnp.take_along_axis(e_logits, g_sel[:, None, None], axis=1)[:, 0]
    top_v, top_i = lax.top_k(within, TOP_K)
    wts = jax.nn.softmax(top_v, axis=-1) * jnp.take_along_axis(g_prob, g_sel[:, None], axis=1)
    eid = (g_sel[:, None] * EXPERTS_PER_GROUP + top_i).astype(jnp.int32)

    n_assign = n_tok * TOP_K
    flat_e = eid.reshape(-1)
    onehot = (flat_e[:, None] == jnp.arange(n_exp, dtype=jnp.int32)[None, :]).astype(jnp.int32)
    csum = jnp.cumsum(onehot, axis=0)
    rank = jnp.take_along_axis(csum, flat_e[:, None], axis=1)[:, 0] - 1
    counts = csum[-1]
    padded = (counts + MOE_BLOCK - 1) // MOE_BLOCK * MOE_BLOCK
    pend = jnp.cumsum(padded)
    pstart = pend - padded
    dest = pstart[flat_e] + rank
    n_blk = -(-n_assign // MOE_BLOCK) + n_exp
    tok = jnp.arange(n_assign, dtype=jnp.int32) // TOP_K
    slot_tok = jnp.zeros((n_blk * MOE_BLOCK,), jnp.int32).at[dest].set(tok)
    blk_start = jnp.arange(n_blk, dtype=jnp.int32) * MOE_BLOCK
    blk_exp = jnp.minimum(jnp.sum(blk_start[:, None] >= pend[None, :], axis=1), n_exp - 1).astype(jnp.int32)
    return wts, dest, slot_tok, blk_exp


def _moe(h2, logits, wg, wu, wd):
    n_tok, d = h2.shape
    wts, dest, slot_tok, blk_exp = _route(logits)
    xb = jnp.take(h2, slot_tok, axis=0)
    yb = _moe_blocks(blk_exp, xb, wg, wu, wd)
    contrib = jnp.take(yb, dest, axis=0) * wts.reshape(-1)[:, None]
    return contrib.reshape(n_tok, TOP_K, d).sum(axis=1)


def _final_kernel(x_ref, y_ref, pmod_ref, g_ref, o_ref):
    x = x_ref[...] + pmod_ref[0, 5:6, :] * y_ref[...]
    o_ref[...] = x * lax.rsqrt(jnp.mean(x * x, axis=-1, keepdims=True) + NORM_EPS) * g_ref[...]


def _final(x, y, pmod, g, *, n_lat, seq, b):
    d = x.shape[1]
    tm = 512
    row = lambda i: (i, 0)
    return pl.pallas_call(
        _final_kernel,
        grid=(n_lat // tm,),
        in_specs=[pl.BlockSpec((tm, d), row), pl.BlockSpec((tm, d), row),
                  pl.BlockSpec((1, 6, d), _mod_row_map(tm, n_lat, seq, b)),
                  pl.BlockSpec((1, d), lambda i: (0, 0))],
        out_specs=pl.BlockSpec((tm, d), row),
        out_shape=jax.ShapeDtypeStruct((n_lat, d), F32),
        compiler_params=_cparams(("parallel",)),
        name="final_residual_norm",
    )(x, y, pmod, g)


def _arrange_w_in(w_in_l):
    d = w_in_l.shape[0]
    hg = w_in_l[:, :5 * GW]
    o = 5 * GW
    ssd_z = w_in_l[:, o:o + GW]
    ssd_xbc = w_in_l[:, o + GW:o + GW + 1024]
    ssd_dt = w_in_l[:, o + GW + 1024:o + GW + 1024 + 2 * SSD_HEADS]
    o += GW + 1024 + 2 * SSD_HEADS
    lru = w_in_l[:, o:o + 2 * GW]
    o += 2 * GW
    ml_main = w_in_l[:, o:o + 4 * GW]
    ml_gates = w_in_l[:, o + 4 * GW:o + 4 * GW + 4 * ML_HEADS]
    small = jnp.concatenate([ssd_dt, ml_gates], axis=1)
    pad = jnp.zeros((d, NCOLS - C_SMALL - small.shape[1]), w_in_l.dtype)
    return jnp.concatenate([hg, ssd_z, ssd_xbc, lru, ml_main, small, pad], axis=1)


def _blockdiag_dense(w):
    n, bw, _ = w.shape
    eye = jnp.eye(n, dtype=w.dtype)
    return (eye[:, None, :, None] * w[:, :, None, :]).reshape(n * bw, n * bw)


def _pad_lanes(v):
    return jnp.concatenate([v, jnp.zeros((LANE - v.shape[0],), v.dtype)])[None, :]


def kernel(x, c, ctx, c_ctx, ada_w, ada_b, norm1_g, norm2_g, w_in, w_out, hg_lb_logits, ssd_conv_w, ssd_a_log, ssd_dt_bias, ssd_d, ssd_norm_g, lru_conv_w, lru_wa, lru_ba, lru_wx, lru_bx, lru_lambda, ml_b_i, ml_b_f, router_group_w, router_expert_w, moe_w_gate, moe_w_up, moe_w_down, final_norm_g):
    nb, seq, d = x.shape
    nctx = ctx.shape[1]
    depth = ada_w.shape[0]
    n_lat = nb * seq
    t = T_CHUNK
    nc, nl = nctx // t, seq // t
    kw = dict(nb=nb, nc=nc, nl=nl)
    lay = dict(n_lat=n_lat, seq=seq, b=nb)

    p = jax.nn.softmax(hg_lb_logits.astype(F32), axis=0)
    lb_all = (jnp.cumsum(p, axis=0) - p[:1])[:, :, None, :]

    c_all = jnp.concatenate([c, c_ctx[None, :], jnp.zeros((SUB - nb - 1, d), F32)], axis=0)
    mods = _modulation(c_all, ada_w, ada_b).reshape(depth, SUB, 6, d)

    xall = jnp.concatenate([x.reshape(n_lat, d), ctx.reshape(nb * nctx, d)], axis=0)
    y_moe = None
    for l in range(depth):
        w_l = _bf(_arrange_w_in(w_in[l]))
        u, xall = _in_proj(xall, y_moe, mods[l - 1] if l else None, norm1_g[l][None, :], mods[l], w_l, **lay)

        o = _hg_pass(u, lb_all[l], None, rev=False, **kw)
        y_hg = _hg_pass(u, lb_all[l], o, rev=True, **kw)

        sbias = _pad_lanes(jnp.concatenate([ssd_dt_bias[l, 0], ssd_dt_bias[l, 1],
                                            ml_b_i[l, 0], ml_b_i[l, 1], ml_b_f[l, 0], ml_b_f[l, 1]]))
        arow = _pad_lanes(jnp.concatenate([-jnp.exp(ssd_a_log[l, 0]), -jnp.exp(ssd_a_log[l, 1])]))
        dsk = jnp.repeat(ssd_d[l], SSD_HEADDIM)[None, :]
        o = _ssd_pass(u, sbias, arow, ssd_conv_w[l], None, None, rev=False, **kw)
        y_ssd = _ssd_pass(u, sbias, arow, ssd_conv_w[l], (dsk, ssd_norm_g[l][None, :]), o, rev=True, **kw)

        w2 = _bf(jnp.stack([jnp.concatenate([_blockdiag_dense(lru_wa[l, dd]), _blockdiag_dense(lru_wx[l, dd])],
                                            axis=1) for dd in range(2)]))
        b2 = jnp.stack([jnp.concatenate([lru_ba[l, dd], lru_bx[l, dd]])[None, :] for dd in range(2)])
        crow = (-LRU_C * jax.nn.softplus(-lru_lambda[l]))[:, None, :]
        y_lru = _lru_mixer(u, (lru_conv_w[l], w2, b2, crow), nb=nb, seq=seq, ctx=nctx)

        o = _ml_pass(u, sbias, None, rev=False, **kw)
        y_ml = _ml_pass(u, sbias, o, rev=True, **kw)

        w_router = jnp.concatenate(
            [router_group_w[l], router_expert_w[l],
             jnp.zeros((d, LANE - N_EXPERT_GROUPS * (1 + EXPERTS_PER_GROUP)), F32)], axis=1)
        xall, h2, logits = _out_proj((y_hg, y_ssd, y_lru, y_ml), _bf(w_out[l]).reshape(4, GW, d), xall,
                                     mods[l], norm2_g[l][None, :], w_router, **lay)
        y_moe = _moe(h2, logits, _bf(moe_w_gate[l]), _bf(moe_w_up[l]), _bf(moe_w_down[l]))

    out = _final(xall, y_moe, mods[depth - 1], final_norm_g[None, :], **lay)
    return out.reshape(nb, seq, d)
```

```python
import functools
import math

import jax
import jax.numpy as jnp
from jax import lax
from jax.experimental import pallas as pl
from jax.experimental.pallas import tpu as pltpu

F32 = jnp.float32
BF16 = jnp.bfloat16

GRID_W = 64
HG_HEADS, HG_DK = 4, 128
SSD_HEADS, SSD_HEADDIM, SSD_GROUPS, SSD_STATE = 8, 64, 2, 128
LRU_C = 8.0
ML_HEADS, ML_DH = 4, 128
N_EXPERT_GROUPS, EXPERTS_PER_GROUP, TOP_K = 4, 8, 2
MOE_BLOCK = 256
NORM_EPS = 1e-6
NEG = -1e30

GW = 512
C_HG_Q, C_HG_I, C_HG_G, C_HG_ZF, C_HG_ZB = 0, 512, 1024, 1536, 2048
C_SSD_Z, C_SSD_XBC = 2560, 3072
C_LRU_GATE, C_LRU_XR = 4096, 4608
C_ML_Q, C_ML_K, C_ML_V, C_ML_O = 5120, 5632, 6144, 6656
C_SMALL = 7168
SM_DT, SM_IG, SM_FG = 0, 16, 24
NCOLS = 7680

LANE = 128
SUB = 8
VMEM_LIMIT = 48 * 1024 * 1024

T_CHUNK = 128


def _cparams(sem):
    return pltpu.CompilerParams(dimension_semantics=sem, vmem_limit_bytes=VMEM_LIMIT)


def _dot(a, b):
    return jnp.dot(a, b, preferred_element_type=F32)


def _dot_nt(a, b):
    return lax.dot_general(a, b, (((1,), (1,)), ((), ())), preferred_element_type=F32)


def _dot_tn(a, b):
    return lax.dot_general(a, b, (((0,), (0,)), ((), ())), preferred_element_type=F32)


def _bf(x):
    return x.astype(BF16)


def _sigmoid(x):
    return 1.0 / (1.0 + jnp.exp(-x))


def _softplus(x):
    return jnp.maximum(x, 0.0) + jnp.log1p(jnp.exp(-jnp.abs(x)))


def _split_dot(mask_bf, x):
    hi = _bf(x)
    r1 = x - hi.astype(F32)
    mid = _bf(r1)
    lo = _bf(r1 - mid.astype(F32))
    return _dot(mask_bf, hi) + _dot(mask_bf, mid) + _dot(mask_bf, lo)


def _rollb(x, n, rev):
    if n == 0:
        return x
    t = x.shape[0]
    return pltpu.roll(x, (t - n) if rev else n, 0)


def _iota2(t):
    return (lax.broadcasted_iota(jnp.int32, (t, t), 0), lax.broadcasted_iota(jnp.int32, (t, t), 1))


def _mod_kernel(c_ref, w_ref, b_ref, o_ref):
    c = c_ref[...]
    s = c * _sigmoid(c)
    o_ref[0] = jnp.dot(s, w_ref[0], precision=lax.Precision.HIGHEST,
                       preferred_element_type=F32) + b_ref[0]


def _modulation(c_all, ada_w, ada_b):
    depth, d, n6 = ada_w.shape
    tn = 1024
    return pl.pallas_call(
        _mod_kernel,
        grid=(depth, n6 // tn),
        in_specs=[pl.BlockSpec((SUB, d), lambda l, j: (0, 0)),
                  pl.BlockSpec((1, d, tn), lambda l, j: (l, 0, j)),
                  pl.BlockSpec((1, 1, tn), lambda l, j: (l, 0, j))],
        out_specs=pl.BlockSpec((1, SUB, tn), lambda l, j: (l, 0, j)),
        out_shape=jax.ShapeDtypeStruct((depth, SUB, n6), F32),
        compiler_params=_cparams(("parallel", "parallel")),
        name="adaln_modulation",
    )(c_all, ada_w, ada_b.reshape(depth, 1, n6))


def _rms_mod(x, g, shift, scale):
    y = x * lax.rsqrt(jnp.mean(x * x, axis=-1, keepdims=True) + NORM_EPS) * g
    return y * (1.0 + scale) + shift


def _mod_row_map(tm, n_lat, seq, b):
    def f(i, *_):
        return (jnp.where(i * tm < n_lat, (i * tm) // seq, b), 0, 0)
    return f


def _resnorm_kernel(*refs, residual):
    if residual:
        x_ref, y0_ref, y1_ref, wt_ref, pmod_ref, g_ref, mod_ref, xn_ref, h_ref = refs
        y = (wt_ref[:, 0:1] * y0_ref[...].astype(F32) + wt_ref[:, 1:2] * y1_ref[...].astype(F32))
        x = x_ref[...] + pmod_ref[0, 5:6, :] * y
        xn_ref[...] = x
    else:
        x_ref, g_ref, mod_ref, h_ref = refs
        x = x_ref[...]
    h_ref[...] = _bf(_rms_mod(x, g_ref[...], mod_ref[0, 0:1, :], mod_ref[0, 1:2, :]))


def _res_norm(x, ymoe, pmod, g1, mod, *, n_lat, seq, b):
    ntot, d = x.shape
    tm = 512
    residual = ymoe is not None
    row = lambda i: (i, 0)
    mrow = _mod_row_map(tm, n_lat, seq, b)
    in_specs = [pl.BlockSpec((tm, d), row)]
    args = [x]
    if residual:
        y0, y1, wts = ymoe
        in_specs += [pl.BlockSpec((tm, d), row), pl.BlockSpec((tm, d), row),
                     pl.BlockSpec((tm, TOP_K), row), pl.BlockSpec((1, 6, d), mrow)]
        args += [y0, y1, wts, pmod]
    in_specs += [pl.BlockSpec((1, d), lambda i: (0, 0)), pl.BlockSpec((1, 6, d), mrow)]
    args += [g1, mod]
    out_specs = [pl.BlockSpec((tm, d), row)]
    out_shape = [jax.ShapeDtypeStruct((ntot, d), BF16)]
    if residual:
        out_specs = [pl.BlockSpec((tm, d), row)] + out_specs
        out_shape = [jax.ShapeDtypeStruct((ntot, d), F32)] + out_shape
    res = pl.pallas_call(
        functools.partial(_resnorm_kernel, residual=residual),
        grid=(ntot // tm,),
        in_specs=in_specs, out_specs=out_specs, out_shape=out_shape,
        compiler_params=_cparams(("parallel",)),
        name="residual_norm1",
    )(*args)
    return (res[0], res[1]) if residual else (x, res[0])


def _inproj_kernel(h_ref, w_ref, u_ref):
    u_ref[...] = _dot(h_ref[...], w_ref[0])


def _in_proj(h, w_tiles):
    ntot, d = h.shape
    nt, _, tn = w_tiles.shape
    tm = 1536
    assert ntot % tm == 0
    return pl.pallas_call(
        _inproj_kernel,
        grid=(ntot // tm, nt),
        in_specs=[pl.BlockSpec((tm, d), lambda i, j: (i, 0)),
                  pl.BlockSpec((1, d, tn), lambda i, j: (j, 0, 0))],
        out_specs=pl.BlockSpec((tm, tn), lambda i, j: (i, j)),
        out_shape=jax.ShapeDtypeStruct((ntot, nt * tn), F32),
        compiler_params=_cparams(("parallel", "arbitrary")),
        name="in_proj",
    )(h, w_tiles)


def _chunk_nat(i, rev, nc, nl):
    is_ctx = i < nc
    if rev:
        c = jnp.where(is_ctx, nc - 1 - i, nl - 1 - (i - nc))
    else:
        c = jnp.where(is_ctx, i, i - nc)
    return is_ctx, c


def _row_block(b, i, *, rev, nc, nl, nb):
    is_ctx, c = _chunk_nat(i, rev, nc, nl)
    return jnp.where(is_ctx, nb * nl + b * nc + c, b * nl + c)


def _col_spec(t, width, col, rb):
    cb = col // width
    return pl.BlockSpec((t, width), lambda b, i: (rb(b, i), cb))


def _hg_kernel(*refs, rev, finish):
    if finish:
        q_ref, v_ref, z_ref, lb_ref, g_ref, of_ref, o_ref, st_scr = refs
    else:
        q_ref, v_ref, z_ref, lb_ref, o_ref, st_scr = refs
    t = q_ref.shape[0]
    dk = HG_DK

    @pl.when(pl.program_id(1) == 0)
    def _():
        st_scr[...] = jnp.zeros_like(st_scr)

    lb = lb_ref[0]
    sig = _sigmoid(z_ref[...])
    f = lb + (1.0 - lb) * sig
    lf = jnp.log(f)
    k = (1.0 - lb) * (1.0 - sig)
    q = q_ref[...]
    v = v_ref[...]

    w = q.shape[1]
    row, col = _iota2(t)
    valid = (col >= row) if rev else (col <= row)
    lvl = jnp.where(valid, 31 - lax.clz(row ^ col), -2)
    b = _split_dot(_bf(valid.astype(F32)), lf)
    total = jnp.sum(lf, axis=0, keepdims=True)

    tpos = lax.broadcasted_iota(jnp.int32, (t, w), 0)
    b8 = b.reshape(t // SUB, SUB, w)
    r8 = lax.broadcasted_iota(jnp.int32, (t // SUB, SUB, w), 1)
    ops = [(-1, _bf(q), _bf(k))]
    h = 1
    while h < t:
        blk = 2 * h
        lg = int(math.log2(h))
        is_q = ((tpos & h) == 0) if rev else ((tpos & h) != 0)
        if h == 1:
            e = f
        else:
            off = h if rev else h - 1
            if blk <= SUB:
                ref = b8[:, off:off + 1, :]
                for sb in range(1, SUB // blk):
                    ref = jnp.where(r8 >= sb * blk, b8[:, sb * blk + off:sb * blk + off + 1, :], ref)
                ref = jnp.broadcast_to(ref, b8.shape).reshape(t, w)
            else:
                b3 = b.reshape(t // blk, blk, w)
                ref = jnp.broadcast_to(b3[:, off:off + 1, :], b3.shape).reshape(t, w)
            e = jnp.exp(-jnp.abs(b - ref))
        if h == 1:
            m = _bf(jnp.where(is_q, q * e, k))
        else:
            m = _bf(jnp.where(is_q, q, k) * e)
        ops.append((lg, m, m))
        h = blk

    qb = _bf(q * jnp.exp(b))
    kw = _bf(k * jnp.exp(total - b))
    vb = _bf(v)
    dec_tot = jnp.exp(total)

    outs = []
    for hd in range(HG_HEADS):
        sl = slice(hd * dk, (hd + 1) * dk)
        a = jnp.zeros((t, t), F32)
        for lg, mq, mk in ops:
            a = jnp.where(lvl == lg, _dot_nt(mq[:, sl], mk[:, sl]), a)
        st = st_scr[hd]
        o = _dot(_bf(a), vb[:, sl]) + _dot_nt(qb[:, sl], _bf(st))
        st_scr[hd] = st * dec_tot[:, sl] + _dot_tn(vb[:, sl], kw[:, sl])
        outs.append(o)

    if finish:
        for hd in range(HG_HEADS):
            sl = slice(hd * dk, (hd + 1) * dk)
            tot = outs[hd] + of_ref[:, sl]
            y = tot * lax.rsqrt(jnp.mean(tot * tot, axis=-1, keepdims=True) + NORM_EPS)
            g = g_ref[:, sl]
            o_ref[:, sl] = (y * (g * _sigmoid(g))).astype(o_ref.dtype)
    else:
        for hd in range(HG_HEADS):
            o_ref[:, hd * dk:(hd + 1) * dk] = outs[hd]


def _hg_pass(u, lb, o_f, *, rev, nb, nc, nl):
    t = T_CHUNK
    ntot = u.shape[0]
    finish = o_f is not None
    rb = functools.partial(_row_block, rev=rev, nc=nc, nl=nl, nb=nb)
    d = 1 if rev else 0
    in_specs = [_col_spec(t, GW, C_HG_Q, rb), _col_spec(t, GW, C_HG_I, rb),
                _col_spec(t, GW, C_HG_ZB if rev else C_HG_ZF, rb),
                pl.BlockSpec((1, 1, GW), lambda b, i: (d, 0, 0))]
    args = [u, u, u, lb]
    if finish:
        in_specs += [_col_spec(t, GW, C_HG_G, rb), _col_spec(t, GW, 0, rb)]
        args += [u, o_f]
    return pl.pallas_call(
        functools.partial(_hg_kernel, rev=rev, finish=finish),
        grid=(nb, nc + nl),
        in_specs=in_specs,
        out_specs=_col_spec(t, GW, 0, rb),
        out_shape=jax.ShapeDtypeStruct((ntot, GW), BF16 if finish else F32),
        scratch_shapes=[pltpu.VMEM((HG_HEADS, HG_DK, HG_DK), F32)],
        compiler_params=_cparams(("parallel", "arbitrary")),
        name="hgrn2_bwd_finish" if rev else "hgrn2_fwd",
    )(*args)


def _ml_kernel(*refs, rev, finish):
    if finish:
        q_ref, k_ref, v_ref, sm_ref, sb_ref, og_ref, of_ref, o_ref, ct_scr, n_scr, m_scr = refs
    else:
        q_ref, k_ref, v_ref, sm_ref, sb_ref, o_ref, ct_scr, n_scr, m_scr = refs
    t = q_ref.shape[0]
    dh = ML_DH
    d = 1 if rev else 0

    @pl.when(pl.program_id(1) == 0)
    def _():
        ct_scr[...] = jnp.zeros_like(ct_scr)
        n_scr[...] = jnp.zeros_like(n_scr)
        m_scr[...] = jnp.full(m_scr.shape, NEG, F32)

    sm = sm_ref[...] + sb_ref[...]
    lsg = jnp.minimum(sm, 0.0) - jnp.log1p(jnp.exp(-jnp.abs(sm)))
    row, col = _iota2(t)
    valid = (col >= row) if rev else (col <= row)
    bcol = _split_dot(_bf(valid.astype(F32)), lsg)
    tot_row = jnp.sum(lsg, axis=0, keepdims=True)
    b_t = bcol.T
    sm_t = sm.T

    q = q_ref[...]
    ks = k_ref[...] * (dh ** -0.5)
    qb, kb, vb = _bf(q), _bf(ks), _bf(v_ref[...])

    outs = []
    for hd in range(ML_HEADS):
        sl = slice(hd * dh, (hd + 1) * dh)
        ci = SM_IG + ML_HEADS * d + hd
        cf = SM_FG + ML_HEADS * d + hd
        bc = bcol[:, cf:cf + 1]
        br = b_t[cf:cf + 1, :]
        lir = sm_t[ci:ci + 1, :]
        lic = sm[:, ci:ci + 1]
        m_prev = m_scr[hd:hd + 1, 0:1]
        dmat = jnp.where(valid, bc - br + lir, NEG)
        g = bc + m_prev
        mt = jnp.maximum(g, jnp.max(dmat, axis=-1, keepdims=True))
        w_intra = jnp.exp(dmat - mt)
        w_inter = jnp.exp(g - mt)
        s = _dot_nt(qb[:, sl], kb[:, sl]) * w_intra
        ct = ct_scr[hd]
        nrow = n_scr[hd:hd + 1, :]
        num = _dot(_bf(s), vb[:, sl]) + w_inter * _dot_nt(qb[:, sl], _bf(ct))
        den = (jnp.sum(s, axis=-1, keepdims=True)
               + w_inter * jnp.sum(q[:, sl] * nrow, axis=-1, keepdims=True))
        outs.append(num / jnp.maximum(jnp.abs(den), jnp.exp(-mt)))
        tot = tot_row[:, cf:cf + 1]
        dl = tot - bc + lic
        m_new = jnp.maximum(tot + m_prev, jnp.max(dl, axis=0, keepdims=True))
        ws = jnp.exp(dl - m_new)
        wc = jnp.exp(tot + m_prev - m_new)
        kws = ks[:, sl] * ws
        ct_scr[hd] = wc * ct + _dot_tn(vb[:, sl], _bf(kws))
        n_scr[hd:hd + 1, :] = wc * nrow + jnp.sum(kws, axis=0, keepdims=True)
        m_scr[hd:hd + 1, :] = jnp.broadcast_to(m_new, (1, m_scr.shape[1]))

    for hd in range(ML_HEADS):
        sl = slice(hd * dh, (hd + 1) * dh)
        if finish:
            tot = outs[hd] + of_ref[:, sl]
            y = tot * lax.rsqrt(jnp.mean(tot * tot, axis=-1, keepdims=True) + NORM_EPS)
            o_ref[:, sl] = (y * _sigmoid(og_ref[:, sl])).astype(o_ref.dtype)
        else:
            o_ref[:, sl] = outs[hd]


def _ml_pass(u, sbias, o_f, *, rev, nb, nc, nl):
    t = T_CHUNK
    ntot = u.shape[0]
    finish = o_f is not None
    rb = functools.partial(_row_block, rev=rev, nc=nc, nl=nl, nb=nb)
    in_specs = [_col_spec(t, GW, C_ML_Q, rb), _col_spec(t, GW, C_ML_K, rb),
                _col_spec(t, GW, C_ML_V, rb), _col_spec(t, LANE, C_SMALL, rb),
                pl.BlockSpec((1, LANE), lambda b, i: (0, 0))]
    args = [u, u, u, u, sbias]
    if finish:
        in_specs += [_col_spec(t, GW, C_ML_O, rb), _col_spec(t, GW, 0, rb)]
        args += [u, o_f]
    return pl.pallas_call(
        functools.partial(_ml_kernel, rev=rev, finish=finish),
        grid=(nb, nc + nl),
        in_specs=in_specs,
        out_specs=_col_spec(t, GW, 0, rb),
        out_shape=jax.ShapeDtypeStruct((ntot, GW), BF16 if finish else F32),
        scratch_shapes=[pltpu.VMEM((ML_HEADS, ML_DH, ML_DH), F32),
                        pltpu.VMEM((SUB, ML_DH), F32),
                        pltpu.VMEM((SUB, LANE), F32)],
        compiler_params=_cparams(("parallel", "arbitrary")),
        name="mlstm_bwd_finish" if rev else "mlstm_fwd",
    )(*args)


def _conv4(xe_scr, x, prev8, next8, w, first, last):
    t = x.shape[0]
    xe_scr[0:SUB, :] = jnp.where(first, 0.0, prev8)
    xe_scr[SUB:SUB + t, :] = x
    xe_scr[SUB + t:SUB + t + SUB, :] = jnp.where(last, 0.0, next8)
    out = w[0:1, :] * xe_scr[SUB - 2:SUB - 2 + t, :]
    for j in range(1, 4):
        out = out + w[j:j + 1, :] * xe_scr[SUB - 2 + j:SUB - 2 + j + t, :]
    return out


def _ssd_kernel(*refs, rev, finish, nc, nl):
    if finish:
        (x_ref, xp_ref, xn_ref, sm_ref, sb_ref, arow_ref, cw_ref,
         z_ref, of_ref, dsk_ref, ng_ref, o_ref, st_scr, xe_scr) = refs
    else:
        x_ref, xp_ref, xn_ref, sm_ref, sb_ref, arow_ref, cw_ref, o_ref, st_scr, xe_scr = refs
    t = x_ref.shape[0]
    p = SSD_HEADDIM
    n = SSD_STATE
    d = 1 if rev else 0
    i = pl.program_id(1)

    @pl.when(i == 0)
    def _():
        st_scr[...] = jnp.zeros_like(st_scr)

    is_ctx, c = _chunk_nat(i, rev, nc, nl)
    first = c == 0
    last = c == jnp.where(is_ctx, nc, nl) - 1
    xc = _conv4(xe_scr, x_ref[...], xp_ref[...], xn_ref[...], cw_ref[...], first, last)
    xc = xc * _sigmoid(xc)
    xs = xc[:, :SSD_HEADS * p]
    bm = xc[:, SSD_HEADS * p:SSD_HEADS * p + SSD_GROUPS * n]
    cm = xc[:, SSD_HEADS * p + SSD_GROUPS * n:]

    dl = _softplus(sm_ref[...] + sb_ref[...])
    lfm = dl * arow_ref[...]
    row, col = _iota2(t)
    valid = (col >= row) if rev else (col <= row)
    bcol = _split_dot(_bf(valid.astype(F32)), lfm)
    tot_row = jnp.sum(lfm, axis=0, keepdims=True)
    b_t = bcol.T
    dl_t = dl.T

    xsb = _bf(xs)
    rep = SSD_HEADS // SSD_GROUPS
    outs = []
    for gi in range(SSD_GROUPS):
        bg = bm[:, gi * n:(gi + 1) * n]
        cgb = _bf(cm[:, gi * n:(gi + 1) * n])
        gmat = _dot_nt(cgb, _bf(bg))
        for hh in range(rep):
            hd = gi * rep + hh
            cc = SM_DT + SSD_HEADS * d + hd
            bc = bcol[:, cc:cc + 1]
            br = b_t[cc:cc + 1, :]
            dr = dl_t[cc:cc + 1, :]
            dcol = dl[:, cc:cc + 1]
            tot = tot_row[:, cc:cc + 1]
            att = gmat * jnp.exp(jnp.where(valid, bc - br, NEG)) * dr
            xh = xsb[:, hd * p:(hd + 1) * p]
            st = st_scr[hd]
            o = _dot(_bf(att), xh) + jnp.exp(bc) * _dot_nt(cgb, _bf(st))
            wk = dcol * jnp.exp(tot - bc)
            st_scr[hd] = jnp.exp(tot) * st + _dot_tn(xh, _bf(bg * wk))
            outs.append(o)
    o_all = jnp.concatenate(outs, axis=-1)

    if finish:
        y = (o_all + of_ref[...] + dsk_ref[...] * xs)
        z = z_ref[...]
        y = y * (z * _sigmoid(z))
        gw = SSD_HEADS * p // SSD_GROUPS
        for gi in range(SSD_GROUPS):
            sl = slice(gi * gw, (gi + 1) * gw)
            yg = y[:, sl]
            yg = yg * lax.rsqrt(jnp.mean(yg * yg, axis=-1, keepdims=True) + NORM_EPS)
            o_ref[:, sl] = (yg * ng_ref[:, sl]).astype(o_ref.dtype)
    else:
        o_ref[...] = o_all


def _ssd_pass(u, sbias, arow, conv_w, z_extra, o_f, *, rev, nb, nc, nl):
    t = T_CHUNK
    ntot = u.shape[0]
    finish = o_f is not None
    rb = functools.partial(_row_block, rev=rev, nc=nc, nl=nl, nb=nb)
    r8 = t // SUB
    nblk8 = ntot // SUB
    cb = C_SSD_XBC // 1024

    def prev_map(b, i):
        return (jnp.maximum(rb(b, i) * r8 - 1, 0), cb)

    def next_map(b, i):
        return (jnp.minimum((rb(b, i) + 1) * r8, nblk8 - 1), cb)

    const2 = lambda b, i: (0, 0)
    in_specs = [_col_spec(t, 1024, C_SSD_XBC, rb),
                pl.BlockSpec((SUB, 1024), prev_map), pl.BlockSpec((SUB, 1024), next_map),
                _col_spec(t, LANE, C_SMALL, rb),
                pl.BlockSpec((1, LANE), const2), pl.BlockSpec((1, LANE), const2),
                pl.BlockSpec((4, 1024), const2)]
    args = [u, u, u, u, sbias, arow, conv_w]
    if finish:
        dsk, ng = z_extra
        in_specs += [_col_spec(t, GW, C_SSD_Z, rb), _col_spec(t, GW, 0, rb),
                     pl.BlockSpec((1, GW), const2), pl.BlockSpec((1, GW), const2)]
        args += [u, o_f, dsk, ng]
    return pl.pallas_call(
        functools.partial(_ssd_kernel, rev=rev, finish=finish, nc=nc, nl=nl),
        grid=(nb, nc + nl),
        in_specs=in_specs,
        out_specs=_col_spec(t, GW, 0, rb),
        out_shape=jax.ShapeDtypeStruct((ntot, GW), BF16 if finish else F32),
        scratch_shapes=[pltpu.VMEM((SSD_HEADS, SSD_HEADDIM, SSD_STATE), F32),
                        pltpu.VMEM((t + 2 * SUB, 1024), F32)],
        compiler_params=_cparams(("parallel", "arbitrary")),
        name="ssd_bwd_finish" if rev else "ssd_fwd",
    )(*args)


def _lru_kernel(*refs, rev, finish, nc, nl):
    if finish:
        (x_ref, xp_ref, xn_ref, cw_ref, w2_ref, b2_ref, crow_ref,
         gate_ref, of_ref, o_ref, h_scr, xe_scr) = refs
    else:
        x_ref, xp_ref, xn_ref, cw_ref, w2_ref, b2_ref, crow_ref, o_ref, h_scr, xe_scr = refs
    t = x_ref.shape[0]
    w = x_ref.shape[1]
    ng = t // SUB
    i = pl.program_id(1)

    @pl.when(i == 0)
    def _():
        h_scr[...] = jnp.zeros_like(h_scr)

    is_ctx, c = _chunk_nat(i, rev, nc, nl)
    first = c == 0
    last = c == jnp.where(is_ctx, nc, nl) - 1
    xr = _conv4(xe_scr, x_ref[...], xp_ref[...], xn_ref[...], cw_ref[...], first, last)
    pre = _dot(_bf(xr), w2_ref[0]) + b2_ref[0]
    r = _sigmoid(pre[:, :w])
    ig = _sigmoid(pre[:, w:])
    log_a = crow_ref[0] * r
    a = jnp.exp(log_a)
    bb = jnp.sqrt(jnp.tanh(-log_a) * (1.0 + a * a)) * (ig * xr)

    a3 = a.reshape(ng, SUB, w)
    b3 = bb.reshape(ng, SUB, w)
    r8 = lax.broadcasted_iota(jnp.int32, (ng, SUB, w), 1)
    pos = (SUB - 1 - r8) if rev else r8
    step = 1
    while step < SUB:
        sh = (SUB - step) if rev else step
        a_s = pltpu.roll(a3, sh, 1)
        b_s = pltpu.roll(b3, sh, 1)
        ok = pos >= step
        b3 = jnp.where(ok, a3 * b_s + b3, b3)
        a3 = jnp.where(ok, a3 * a_s, a3)
        step *= 2
    h = h_scr[...]
    groups = [None] * ng
    for g in (range(ng - 1, -1, -1) if rev else range(ng)):
        hg = b3[g] + a3[g] * h
        groups[g] = hg
        h = hg[0:1, :] if rev else hg[SUB - 1:SUB, :]
    h_scr[...] = h
    hs = jnp.concatenate(groups, axis=0)

    if finish:
        g = gate_ref[...]
        gelu = 0.5 * g * (1.0 + jnp.tanh(math.sqrt(2.0 / math.pi) * (g + 0.044715 * (g * g * g))))
        o_ref[...] = ((hs + of_ref[...]) * gelu).astype(o_ref.dtype)
    else:
        o_ref[...] = hs


def _lru_pass(ul, consts, o_f, *, rev, nb, nc, nl):
    t = T_CHUNK
    ntot = ul.shape[0]
    finish = o_f is not None
    d = 1 if rev else 0
    rb = functools.partial(_row_block, rev=rev, nc=nc, nl=nl, nb=nb)
    r8 = t // SUB
    nblk8 = ntot // SUB
    cw, w2, b2, crow = consts
    const2 = lambda b, i: (0, 0)
    dsel = lambda b, i: (d, 0, 0)
    in_specs = [_col_spec(t, GW, GW, rb),
                pl.BlockSpec((SUB, GW), lambda b, i: (jnp.maximum(rb(b, i) * r8 - 1, 0), 1)),
                pl.BlockSpec((SUB, GW), lambda b, i: (jnp.minimum((rb(b, i) + 1) * r8, nblk8 - 1), 1)),
                pl.BlockSpec((4, GW), const2),
                pl.BlockSpec((1, GW, 2 * GW), dsel), pl.BlockSpec((1, 1, 2 * GW), dsel),
                pl.BlockSpec((1, 1, GW), dsel)]
    args = [ul, ul, ul, cw, w2, b2, crow]
    if finish:
        in_specs += [_col_spec(t, GW, 0, rb), _col_spec(t, GW, 0, rb)]
        args += [ul, o_f]
    return pl.pallas_call(
        functools.partial(_lru_kernel, rev=rev, finish=finish, nc=nc, nl=nl),
        grid=(nb, nc + nl),
        in_specs=in_specs,
        out_specs=_col_spec(t, GW, 0, rb),
        out_shape=jax.ShapeDtypeStruct((ntot, GW), BF16 if finish else F32),
        scratch_shapes=[pltpu.VMEM((1, GW), F32), pltpu.VMEM((t + 2 * SUB, GW), F32)],
        compiler_params=_cparams(("parallel", "arbitrary")),
        name="rglru_bwd_finish" if rev else "rglru_fwd",
    )(*args)


def _lru_mixer(u, consts, *, nb, seq, nc, nl):
    n_lat = nb * seq
    rows = seq // GRID_W
    assert rows == T_CHUNK, "one scan chunk must be one latent grid column"
    ul = u[:, C_LRU_GATE:C_LRU_GATE + 2 * GW]
    lat = ul[:n_lat].reshape(nb, rows, GRID_W, 2 * GW).transpose(0, 2, 1, 3).reshape(n_lat, 2 * GW)
    ul = jnp.concatenate([lat, ul[n_lat:]], axis=0)
    o = _lru_pass(ul, consts, None, rev=False, nb=nb, nc=nc, nl=nl)
    y = _lru_pass(ul, consts, o, rev=True, nb=nb, nc=nc, nl=nl)
    lat = y[:n_lat].reshape(nb, GRID_W, rows, GW).transpose(0, 2, 1, 3).reshape(n_lat, GW)
    return jnp.concatenate([lat, y[n_lat:]], axis=0)


def _outproj_kernel(yh_ref, ys_ref, yl_ref, ym_ref, w_ref, x_ref, mod_ref, g_ref, wr_ref,
                    xn_ref, h_ref, lg_ref):
    acc = _dot(yh_ref[...], w_ref[0])
    acc = acc + _dot(ys_ref[...], w_ref[1])
    acc = acc + _dot(yl_ref[...], w_ref[2])
    acc = acc + _dot(ym_ref[...], w_ref[3])
    xn = x_ref[...] + mod_ref[0, 2:3, :] * acc
    xn_ref[...] = xn
    h = _rms_mod(xn, g_ref[...], mod_ref[0, 3:4, :], mod_ref[0, 4:5, :])
    h1 = _bf(h)
    h_ref[...] = h1
    h2 = _bf(h - h1.astype(F32))
    p = _dot(h1, wr_ref[...])
    lg_ref[...] = p[:, :LANE] + p[:, LANE:] + _dot(h2, wr_ref[:, :LANE])


def _out_proj(ys, w_out4, x, mod, g2, w_router, *, n_lat, seq, b):
    ntot, d = x.shape
    tm = 512
    row = lambda i: (i, 0)
    mrow = _mod_row_map(tm, n_lat, seq, b)
    in_specs = [pl.BlockSpec((tm, GW), row)] * 4 + [
        pl.BlockSpec((4, GW, d), lambda i: (0, 0, 0)),
        pl.BlockSpec((tm, d), row), pl.BlockSpec((1, 6, d), mrow),
        pl.BlockSpec((1, d), lambda i: (0, 0)), pl.BlockSpec((d, 2 * LANE), lambda i: (0, 0))]
    return pl.pallas_call(
        _outproj_kernel,
        grid=(ntot // tm,),
        in_specs=in_specs,
        out_specs=[pl.BlockSpec((tm, d), row), pl.BlockSpec((tm, d), row), pl.BlockSpec((tm, LANE), row)],
        out_shape=[jax.ShapeDtypeStruct((ntot, d), F32), jax.ShapeDtypeStruct((ntot, d), BF16),
                   jax.ShapeDtypeStruct((ntot, LANE), F32)],
        compiler_params=_cparams(("parallel",)),
        name="out_proj_norm2_router",
    )(*ys, w_out4, x, mod, g2, w_router)


def _moe_kernel(be_ref, xb_ref, wg_ref, wu_ref, wd_ref, o_ref, wgu_scr, wd_scr):
    i = pl.program_id(0)
    de = wg_ref.shape[-1]

    @pl.when(jnp.logical_or(i == 0, be_ref[i] != be_ref[jnp.maximum(i - 1, 0)]))
    def _():
        wgu_scr[:, :de] = _bf(wg_ref[0])
        wgu_scr[:, de:] = _bf(wu_ref[0])
        wd_scr[...] = _bf(wd_ref[0])

    gu = _dot(xb_ref[...], wgu_scr[...])
    g = gu[:, :de]
    hid = (g * _sigmoid(g)) * gu[:, de:]
    o_ref[...] = _bf(_dot(_bf(hid), wd_scr[...]))


def _moe_blocks(blk_exp, xb, wg, wu, wd):
    nslot, d = xb.shape
    de = wg.shape[-1]
    nblk = nslot // MOE_BLOCK
    return pl.pallas_call(
        _moe_kernel,
        grid_spec=pltpu.PrefetchScalarGridSpec(
            num_scalar_prefetch=1,
            grid=(nblk,),
            in_specs=[pl.BlockSpec((MOE_BLOCK, d), lambda i, be: (i, 0)),
                      pl.BlockSpec((1, d, de), lambda i, be: (be[i], 0, 0)),
                      pl.BlockSpec((1, d, de), lambda i, be: (be[i], 0, 0)),
                      pl.BlockSpec((1, de, d), lambda i, be: (be[i], 0, 0))],
            out_specs=pl.BlockSpec((MOE_BLOCK, d), lambda i, be: (i, 0)),
            scratch_shapes=[pltpu.VMEM((d, 2 * de), BF16), pltpu.VMEM((de, d), BF16)]),
        out_shape=jax.ShapeDtypeStruct((nslot, d), BF16),
        compiler_params=_cparams(("arbitrary",)),
        name="moe_expert_blocks",
    )(blk_exp, xb, wg, wu, wd)


def _route(logits):
    n_tok = logits.shape[0]
    n_exp = N_EXPERT_GROUPS * EXPERTS_PER_GROUP
    g_logits = logits[:, :N_EXPERT_GROUPS]
    g_prob = jax.nn.softmax(g_logits, axis=-1)
    g_sel = jnp.argmax(g_logits, axis=-1)
    e_logits = logits[:, N_EXPERT_GROUPS:N_EXPERT_GROUPS + n_exp].reshape(
        n_tok, N_EXPERT_GROUPS, EXPERTS_PER_GROUP)
    within = jnp.take_along_axis(e_logits, g_sel[:, None, None], axis=1)[:, 0]
    top_v, top_i = lax.top_k(within, TOP_K)
    wts = jax.nn.softmax(top_v, axis=-1) * jnp.take_along_axis(g_prob, g_sel[:, None], axis=1)
    eid = (g_sel[:, None] * EXPERTS_PER_GROUP + top_i).astype(jnp.int32)

    n_assign = n_tok * TOP_K
    flat_e = eid.reshape(-1)
    onehot = (flat_e[:, None] == jnp.arange(n_exp, dtype=jnp.int32)[None, :]).astype(jnp.int32)
    csum = jnp.cumsum(onehot, axis=0)
    rank = jnp.take_along_axis(csum, flat_e[:, None], axis=1)[:, 0] - 1
    counts = csum[-1]
    padded = (counts + MOE_BLOCK - 1) // MOE_BLOCK * MOE_BLOCK
    pend = jnp.cumsum(padded)
    pstart = pend - padded
    dest = pstart[flat_e] + rank
    n_blk = -(-n_assign // MOE_BLOCK) + n_exp
    tok = jnp.arange(n_assign, dtype=jnp.int32) // TOP_K
    slot_tok = jnp.zeros((n_blk * MOE_BLOCK,), jnp.int32).at[dest].set(tok)
    blk_start = jnp.arange(n_blk, dtype=jnp.int32) * MOE_BLOCK
    blk_exp = jnp.minimum(jnp.sum(blk_start[:, None] >= pend[None, :], axis=1), n_exp - 1).astype(jnp.int32)
    return wts, dest, slot_tok, blk_exp


def _moe(h2, logits, layer, wg, wu, wd):
    n_tok = h2.shape[0]
    n_exp = wg.shape[1]
    wts, dest, slot_tok, blk_exp = _route(logits)
    xb = jnp.take(h2, slot_tok, axis=0)
    flat = lambda w: w.reshape((w.shape[0] * n_exp,) + w.shape[2:])
    yb = _moe_blocks(blk_exp + layer * n_exp, xb, flat(wg), flat(wu), flat(wd))
    dest = dest.reshape(n_tok, TOP_K)
    return jnp.take(yb, dest[:, 0], axis=0), jnp.take(yb, dest[:, 1], axis=0), wts


def _final_kernel(x_ref, y0_ref, y1_ref, wt_ref, pmod_ref, g_ref, o_ref):
    y = wt_ref[:, 0:1] * y0_ref[...].astype(F32) + wt_ref[:, 1:2] * y1_ref[...].astype(F32)
    x = x_ref[...] + pmod_ref[0, 5:6, :] * y
    o_ref[...] = x * lax.rsqrt(jnp.mean(x * x, axis=-1, keepdims=True) + NORM_EPS) * g_ref[...]


def _final(x, ymoe, pmod, g, *, n_lat, seq, b):
    d = x.shape[1]
    tm = 512
    row = lambda i: (i, 0)
    y0, y1, wts = ymoe
    return pl.pallas_call(
        _final_kernel,
        grid=(n_lat // tm,),
        in_specs=[pl.BlockSpec((tm, d), row), pl.BlockSpec((tm, d), row), pl.BlockSpec((tm, d), row),
                  pl.BlockSpec((tm, TOP_K), row),
                  pl.BlockSpec((1, 6, d), _mod_row_map(tm, n_lat, seq, b)),
                  pl.BlockSpec((1, d), lambda i: (0, 0))],
        out_specs=pl.BlockSpec((tm, d), row),
        out_shape=jax.ShapeDtypeStruct((n_lat, d), F32),
        compiler_params=_cparams(("parallel",)),
        name="final_residual_norm",
    )(x, y0, y1, wts, pmod, g)


def _arrange_w_in(w_in_l):
    d = w_in_l.shape[0]
    hg = w_in_l[:, :5 * GW]
    o = 5 * GW
    ssd_z = w_in_l[:, o:o + GW]
    ssd_xbc = w_in_l[:, o + GW:o + GW + 1024]
    ssd_dt = w_in_l[:, o + GW + 1024:o + GW + 1024 + 2 * SSD_HEADS]
    o += GW + 1024 + 2 * SSD_HEADS
    lru = w_in_l[:, o:o + 2 * GW]
    o += 2 * GW
    ml_main = w_in_l[:, o:o + 4 * GW]
    ml_gates = w_in_l[:, o + 4 * GW:o + 4 * GW + 4 * ML_HEADS]
    small = jnp.concatenate([ssd_dt, ml_gates], axis=1)
    pad = jnp.zeros((d, NCOLS - C_SMALL - small.shape[1]), w_in_l.dtype)
    return jnp.concatenate([hg, ssd_z, ssd_xbc, lru, ml_main, small, pad], axis=1)


def _blockdiag_dense(w):
    n, bw, _ = w.shape
    eye = jnp.eye(n, dtype=w.dtype)
    return (eye[:, None, :, None] * w[:, :, None, :]).reshape(n * bw, n * bw)


def _pad_lanes(v):
    return jnp.concatenate([v, jnp.zeros((LANE - v.shape[0],), v.dtype)])[None, :]


def kernel(x, c, ctx, c_ctx, ada_w, ada_b, norm1_g, norm2_g, w_in, w_out, hg_lb_logits, ssd_conv_w, ssd_a_log, ssd_dt_bias, ssd_d, ssd_norm_g, lru_conv_w, lru_wa, lru_ba, lru_wx, lru_bx, lru_lambda, ml_b_i, ml_b_f, router_group_w, router_expert_w, moe_w_gate, moe_w_up, moe_w_down, final_norm_g):
    nb, seq, d = x.shape
    nctx = ctx.shape[1]
    depth = ada_w.shape[0]
    n_lat = nb * seq
    t = T_CHUNK
    nc, nl = nctx // t, seq // t
    kw = dict(nb=nb, nc=nc, nl=nl)
    lay = dict(n_lat=n_lat, seq=seq, b=nb)

    p = jax.nn.softmax(hg_lb_logits.astype(F32), axis=0)
    lb_all = (jnp.cumsum(p, axis=0) - p[:1])[:, :, None, :]

    c_all = jnp.concatenate([c, c_ctx[None, :], jnp.zeros((SUB - nb - 1, d), F32)], axis=0)
    mods = _modulation(c_all, ada_w, ada_b).reshape(depth, SUB, 6, d)

    xall = jnp.concatenate([x.reshape(n_lat, d), ctx.reshape(nb * nctx, d)], axis=0)
    y_moe = None
    tn = 512
    for l in range(depth):
        w_l = _bf(_arrange_w_in(w_in[l])).reshape(d, NCOLS // tn, tn).transpose(1, 0, 2)
        xall, h1 = _res_norm(xall, y_moe, mods[l - 1] if l else None, norm1_g[l][None, :], mods[l], **lay)
        u = _in_proj(h1, w_l)

        o = _hg_pass(u, lb_all[l], None, rev=False, **kw)
        y_hg = _hg_pass(u, lb_all[l], o, rev=True, **kw)

        sbias = _pad_lanes(jnp.concatenate([ssd_dt_bias[l, 0], ssd_dt_bias[l, 1],
                                            ml_b_i[l, 0], ml_b_i[l, 1], ml_b_f[l, 0], ml_b_f[l, 1]]))
        arow = _pad_lanes(jnp.concatenate([-jnp.exp(ssd_a_log[l, 0]), -jnp.exp(ssd_a_log[l, 1])]))
        dsk = jnp.repeat(ssd_d[l], SSD_HEADDIM)[None, :]
        o = _ssd_pass(u, sbias, arow, ssd_conv_w[l], None, None, rev=False, **kw)
        y_ssd = _ssd_pass(u, sbias, arow, ssd_conv_w[l], (dsk, ssd_norm_g[l][None, :]), o, rev=True, **kw)

        w2 = _bf(jnp.stack([jnp.concatenate([_blockdiag_dense(lru_wa[l, dd]), _blockdiag_dense(lru_wx[l, dd])],
                                            axis=1) for dd in range(2)]))
        b2 = jnp.stack([jnp.concatenate([lru_ba[l, dd], lru_bx[l, dd]])[None, :] for dd in range(2)])
        crow = (-LRU_C * jax.nn.softplus(-lru_lambda[l]))[:, None, :]
        y_lru = _lru_mixer(u, (lru_conv_w[l], w2, b2, crow), nb=nb, seq=seq, nc=nc, nl=nl)

        o = _ml_pass(u, sbias, None, rev=False, **kw)
        y_ml = _ml_pass(u, sbias, o, rev=True, **kw)

        w_router = jnp.concatenate(
            [router_group_w[l], router_expert_w[l],
             jnp.zeros((d, LANE - N_EXPERT_GROUPS * (1 + EXPERTS_PER_GROUP)), F32)], axis=1)
        wr1 = _bf(w_router)
        wr12 = jnp.concatenate([wr1, _bf(w_router - wr1.astype(F32))], axis=1)
        xall, h2, logits = _out_proj((y_hg, y_ssd, y_lru, y_ml), _bf(w_out[l]).reshape(4, GW, d), xall,
                                     mods[l], norm2_g[l][None, :], wr12, **lay)
        y_moe = _moe(h2, logits, l, moe_w_gate, moe_w_up, moe_w_down)

    out = _final(xall, y_moe, mods[depth - 1], final_norm_g[None, :], **lay)
    return out.reshape(nb, seq, d)
```

```python
import functools
import math

import jax
import jax.numpy as jnp
from jax import lax
from jax.experimental import pallas as pl
from jax.experimental.pallas import tpu as pltpu

F32 = jnp.float32
BF16 = jnp.bfloat16

GRID_W = 64
HG_HEADS, HG_DK = 4, 128
SSD_HEADS, SSD_HEADDIM, SSD_GROUPS, SSD_STATE = 8, 64, 2, 128
LRU_C = 8.0
ML_HEADS, ML_DH = 4, 128
N_EXPERT_GROUPS, EXPERTS_PER_GROUP, TOP_K = 4, 8, 2
MOE_BLOCK = 256
NORM_EPS = 1e-6
NEG = -1e30

GW = 512
C_HG_Q, C_HG_I, C_HG_G, C_HG_ZF, C_HG_ZB = 0, 512, 1024, 1536, 2048
C_SSD_Z, C_SSD_XBC = 2560, 3072
C_LRU_GATE, C_LRU_XR = 4096, 4608
C_ML_Q, C_ML_K, C_ML_V, C_ML_O = 5120, 5632, 6144, 6656
C_SMALL = 7168
SM_DT, SM_IG, SM_FG = 0, 16, 24
NCOLS = 7680

LANE = 128
SUB = 8
VMEM_LIMIT = 48 * 1024 * 1024

T_CHUNK = 128


def _cparams(sem):
    return pltpu.CompilerParams(dimension_semantics=sem, vmem_limit_bytes=VMEM_LIMIT)


def _dot(a, b):
    return jnp.dot(a, b, preferred_element_type=F32)


def _dot_nt(a, b):
    return lax.dot_general(a, b, (((1,), (1,)), ((), ())), preferred_element_type=F32)


def _dot_tn(a, b):
    return lax.dot_general(a, b, (((0,), (0,)), ((), ())), preferred_element_type=F32)


def _bf(x):
    return x.astype(BF16)


def _sigmoid(x):
    return 1.0 / (1.0 + jnp.exp(-x))


def _softplus(x):
    return jnp.maximum(x, 0.0) + jnp.log1p(jnp.exp(-jnp.abs(x)))


def _split_dot(mask_bf, x):
    hi = _bf(x)
    r1 = x - hi.astype(F32)
    mid = _bf(r1)
    lo = _bf(r1 - mid.astype(F32))
    return _dot(mask_bf, hi) + _dot(mask_bf, mid) + _dot(mask_bf, lo)


def _rollb(x, n, rev):
    if n == 0:
        return x
    t = x.shape[0]
    return pltpu.roll(x, (t - n) if rev else n, 0)


def _iota2(t):
    return (lax.broadcasted_iota(jnp.int32, (t, t), 0), lax.broadcasted_iota(jnp.int32, (t, t), 1))


def _mod_kernel(c_ref, w_ref, b_ref, o_ref):
    c = c_ref[...]
    s = c * _sigmoid(c)
    o_ref[0] = jnp.dot(s, w_ref[0], precision=lax.Precision.HIGHEST,
                       preferred_element_type=F32) + b_ref[0]


def _modulation(c_all, ada_w, ada_b):
    depth, d, n6 = ada_w.shape
    tn = 1024
    return pl.pallas_call(
        _mod_kernel,
        grid=(depth, n6 // tn),
        in_specs=[pl.BlockSpec((SUB, d), lambda l, j: (0, 0)),
                  pl.BlockSpec((1, d, tn), lambda l, j: (l, 0, j)),
                  pl.BlockSpec((1, 1, tn), lambda l, j: (l, 0, j))],
        out_specs=pl.BlockSpec((1, SUB, tn), lambda l, j: (l, 0, j)),
        out_shape=jax.ShapeDtypeStruct((depth, SUB, n6), F32),
        compiler_params=_cparams(("parallel", "parallel")),
        name="adaln_modulation",
    )(c_all, ada_w, ada_b.reshape(depth, 1, n6))


def _rms_mod(x, g, shift, scale):
    y = x * lax.rsqrt(jnp.mean(x * x, axis=-1, keepdims=True) + NORM_EPS) * g
    return y * (1.0 + scale) + shift


def _mod_row_map(tm, n_lat, seq, b):
    def f(i, *_):
        return (jnp.where(i * tm < n_lat, (i * tm) // seq, b), 0, 0)
    return f


def _resnorm_kernel(*refs, residual):
    if residual:
        x_ref, y0_ref, y1_ref, wt_ref, pmod_ref, g_ref, mod_ref, xn_ref, h_ref = refs
        y = (wt_ref[:, 0:1] * y0_ref[...].astype(F32) + wt_ref[:, 1:2] * y1_ref[...].astype(F32))
        x = x_ref[...] + pmod_ref[0, 5:6, :] * y
        xn_ref[...] = x
    else:
        x_ref, g_ref, mod_ref, h_ref = refs
        x = x_ref[...]
    h_ref[...] = _bf(_rms_mod(x, g_ref[...], mod_ref[0, 0:1, :], mod_ref[0, 1:2, :]))


def _res_norm(x, ymoe, pmod, g1, mod, *, n_lat, seq, b):
    ntot, d = x.shape
    tm = 512
    residual = ymoe is not None
    row = lambda i: (i, 0)
    mrow = _mod_row_map(tm, n_lat, seq, b)
    in_specs = [pl.BlockSpec((tm, d), row)]
    args = [x]
    if residual:
        y0, y1, wts = ymoe
        in_specs += [pl.BlockSpec((tm, d), row), pl.BlockSpec((tm, d), row),
                     pl.BlockSpec((tm, TOP_K), row), pl.BlockSpec((1, 6, d), mrow)]
        args += [y0, y1, wts, pmod]
    in_specs += [pl.BlockSpec((1, d), lambda i: (0, 0)), pl.BlockSpec((1, 6, d), mrow)]
    args += [g1, mod]
    out_specs = [pl.BlockSpec((tm, d), row)]
    out_shape = [jax.ShapeDtypeStruct((ntot, d), BF16)]
    if residual:
        out_specs = [pl.BlockSpec((tm, d), row)] + out_specs
        out_shape = [jax.ShapeDtypeStruct((ntot, d), F32)] + out_shape
    res = pl.pallas_call(
        functools.partial(_resnorm_kernel, residual=residual),
        grid=(ntot // tm,),
        in_specs=in_specs, out_specs=out_specs, out_shape=out_shape,
        compiler_params=_cparams(("parallel",)),
        name="residual_norm1",
    )(*args)
    return (res[0], res[1]) if residual else (x, res[0])


def _inproj_kernel(h_ref, w_ref, u_ref):
    u_ref[...] = _dot(h_ref[...], w_ref[0])


def _in_proj(h, w_tiles):
    ntot, d = h.shape
    nt, _, tn = w_tiles.shape
    tm = 1536
    assert ntot % tm == 0
    return pl.pallas_call(
        _inproj_kernel,
        grid=(ntot // tm, nt),
        in_specs=[pl.BlockSpec((tm, d), lambda i, j: (i, 0)),
                  pl.BlockSpec((1, d, tn), lambda i, j: (j, 0, 0))],
        out_specs=pl.BlockSpec((tm, tn), lambda i, j: (i, j)),
        out_shape=jax.ShapeDtypeStruct((ntot, nt * tn), F32),
        compiler_params=_cparams(("parallel", "arbitrary")),
        name="in_proj",
    )(h, w_tiles)


def _chunk_nat(i, rev, nc, nl):
    is_ctx = i < nc
    if rev:
        c = jnp.where(is_ctx, nc - 1 - i, nl - 1 - (i - nc))
    else:
        c = jnp.where(is_ctx, i, i - nc)
    return is_ctx, c


def _row_block(b, i, *, rev, nc, nl, nb):
    is_ctx, c = _chunk_nat(i, rev, nc, nl)
    return jnp.where(is_ctx, nb * nl + b * nc + c, b * nl + c)


def _col_spec(t, width, col, rb):
    cb = col // width
    return pl.BlockSpec((t, width), lambda b, i: (rb(b, i), cb))


def _hg_kernel(*refs, rev, finish):
    if finish:
        q_ref, v_ref, z_ref, lb_ref, g_ref, of_ref, o_ref, st_scr = refs
    else:
        q_ref, v_ref, z_ref, lb_ref, o_ref, st_scr = refs
    t = q_ref.shape[0]
    dk = HG_DK

    @pl.when(pl.program_id(1) == 0)
    def _():
        st_scr[...] = jnp.zeros_like(st_scr)

    lb = lb_ref[0]
    sig = _sigmoid(z_ref[...])
    f = lb + (1.0 - lb) * sig
    lf = jnp.log(f)
    k = (1.0 - lb) * (1.0 - sig)
    q = q_ref[...]
    v = v_ref[...]

    w = q.shape[1]
    row, col = _iota2(t)
    valid = (col >= row) if rev else (col <= row)
    lvl = jnp.where(valid, 31 - lax.clz(row ^ col), -2)
    b = _split_dot(_bf(valid.astype(F32)), lf)
    total = jnp.sum(lf, axis=0, keepdims=True)

    tpos = lax.broadcasted_iota(jnp.int32, (t, w), 0)
    b8 = b.reshape(t // SUB, SUB, w)
    r8 = lax.broadcasted_iota(jnp.int32, (t // SUB, SUB, w), 1)
    ops = [(-1, _bf(q), _bf(k))]
    h = 1
    while h < t:
        blk = 2 * h
        lg = int(math.log2(h))
        is_q = ((tpos & h) == 0) if rev else ((tpos & h) != 0)
        if h == 1:
            e = f
        else:
            off = h if rev else h - 1
            if blk <= SUB:
                ref = b8[:, off:off + 1, :]
                for sb in range(1, SUB // blk):
                    ref = jnp.where(r8 >= sb * blk, b8[:, sb * blk + off:sb * blk + off + 1, :], ref)
                ref = jnp.broadcast_to(ref, b8.shape).reshape(t, w)
            else:
                b3 = b.reshape(t // blk, blk, w)
                ref = jnp.broadcast_to(b3[:, off:off + 1, :], b3.shape).reshape(t, w)
            e = jnp.exp(-jnp.abs(b - ref))
        if h == 1:
            m = _bf(jnp.where(is_q, q * e, k))
        else:
            m = _bf(jnp.where(is_q, q, k) * e)
        ops.append((lg, m, m))
        h = blk

    qb = _bf(q * jnp.exp(b))
    kw = _bf(k * jnp.exp(total - b))
    vb = _bf(v)
    dec_tot = jnp.exp(total)

    masks = [lvl == lg for lg, _, _ in ops]
    outs = []
    for hd in range(HG_HEADS):
        sl = slice(hd * dk, (hd + 1) * dk)
        a = jnp.zeros((t, t), F32)
        for msk, (lg, mq, mk) in zip(masks, ops):
            a = jnp.where(msk, _dot_nt(mq[:, sl], mk[:, sl]), a)
        st = st_scr[hd]
        o = _dot(_bf(a), vb[:, sl]) + _dot_nt(qb[:, sl], _bf(st))
        st_scr[hd] = st * dec_tot[:, sl] + _dot_tn(vb[:, sl], kw[:, sl])
        outs.append(o)

    if finish:
        for hd in range(HG_HEADS):
            sl = slice(hd * dk, (hd + 1) * dk)
            tot = outs[hd] + of_ref[:, sl]
            y = tot * lax.rsqrt(jnp.mean(tot * tot, axis=-1, keepdims=True) + NORM_EPS)
            g = g_ref[:, sl]
            o_ref[:, sl] = (y * (g * _sigmoid(g))).astype(o_ref.dtype)
    else:
        for hd in range(HG_HEADS):
            o_ref[:, hd * dk:(hd + 1) * dk] = outs[hd]


def _hg_pass(u, lb, o_f, *, rev, nb, nc, nl):
    t = T_CHUNK
    ntot = u.shape[0]
    finish = o_f is not None
    rb = functools.partial(_row_block, rev=rev, nc=nc, nl=nl, nb=nb)
    d = 1 if rev else 0
    in_specs = [_col_spec(t, GW, C_HG_Q, rb), _col_spec(t, GW, C_HG_I, rb),
                _col_spec(t, GW, C_HG_ZB if rev else C_HG_ZF, rb),
                pl.BlockSpec((1, 1, GW), lambda b, i: (d, 0, 0))]
    args = [u, u, u, lb]
    if finish:
        in_specs += [_col_spec(t, GW, C_HG_G, rb), _col_spec(t, GW, 0, rb)]
        args += [u, o_f]
    return pl.pallas_call(
        functools.partial(_hg_kernel, rev=rev, finish=finish),
        grid=(nb, nc + nl),
        in_specs=in_specs,
        out_specs=_col_spec(t, GW, 0, rb),
        out_shape=jax.ShapeDtypeStruct((ntot, GW), BF16 if finish else F32),
        scratch_shapes=[pltpu.VMEM((HG_HEADS, HG_DK, HG_DK), F32)],
        compiler_params=_cparams(("parallel", "arbitrary")),
        name="hgrn2_bwd_finish" if rev else "hgrn2_fwd",
    )(*args)


def _ml_kernel(*refs, rev, finish):
    if finish:
        q_ref, k_ref, v_ref, sm_ref, sb_ref, og_ref, of_ref, o_ref, ct_scr, n_scr, m_scr = refs
    else:
        q_ref, k_ref, v_ref, sm_ref, sb_ref, o_ref, ct_scr, n_scr, m_scr = refs
    t = q_ref.shape[0]
    dh = ML_DH
    d = 1 if rev else 0

    @pl.when(pl.program_id(1) == 0)
    def _():
        ct_scr[...] = jnp.zeros_like(ct_scr)
        n_scr[...] = jnp.zeros_like(n_scr)
        m_scr[...] = jnp.full(m_scr.shape, NEG, F32)

    sm = sm_ref[...] + sb_ref[...]
    lsg = jnp.minimum(sm, 0.0) - jnp.log1p(jnp.exp(-jnp.abs(sm)))
    row, col = _iota2(t)
    valid = (col >= row) if rev else (col <= row)
    bcol = _split_dot(_bf(valid.astype(F32)), lsg)
    tot_row = jnp.sum(lsg, axis=0, keepdims=True)
    b_t = bcol.T
    sm_t = sm.T

    q = q_ref[...]
    ks = k_ref[...] * (dh ** -0.5)
    qb, kb, vb = _bf(q), _bf(ks), _bf(v_ref[...])

    outs = []
    for hd in range(ML_HEADS):
        sl = slice(hd * dh, (hd + 1) * dh)
        ci = SM_IG + ML_HEADS * d + hd
        cf = SM_FG + ML_HEADS * d + hd
        bc = bcol[:, cf:cf + 1]
        br = b_t[cf:cf + 1, :]
        lir = sm_t[ci:ci + 1, :]
        lic = sm[:, ci:ci + 1]
        m_prev = m_scr[hd:hd + 1, 0:1]
        dmat = jnp.where(valid, bc - br + lir, NEG)
        g = bc + m_prev
        mt = jnp.maximum(g, jnp.max(dmat, axis=-1, keepdims=True))
        w_intra = jnp.exp(dmat - mt)
        w_inter = jnp.exp(g - mt)
        s = _dot_nt(qb[:, sl], kb[:, sl]) * w_intra
        ct = ct_scr[hd]
        nrow = n_scr[hd:hd + 1, :]
        num = _dot(_bf(s), vb[:, sl]) + w_inter * _dot_nt(qb[:, sl], _bf(ct))
        den = (jnp.sum(s, axis=-1, keepdims=True)
               + w_inter * jnp.sum(q[:, sl] * nrow, axis=-1, keepdims=True))
        outs.append(num / jnp.maximum(jnp.abs(den), jnp.exp(-mt)))
        tot = tot_row[:, cf:cf + 1]
        dl = tot - bc + lic
        m_new = jnp.maximum(tot + m_prev, jnp.max(dl, axis=0, keepdims=True))
        ws = jnp.exp(dl - m_new)
        wc = jnp.exp(tot + m_prev - m_new)
        kws = ks[:, sl] * ws
        ct_scr[hd] = wc * ct + _dot_tn(vb[:, sl], _bf(kws))
        n_scr[hd:hd + 1, :] = wc * nrow + jnp.sum(kws, axis=0, keepdims=True)
        m_scr[hd:hd + 1, :] = jnp.broadcast_to(m_new, (1, m_scr.shape[1]))

    for hd in range(ML_HEADS):
        sl = slice(hd * dh, (hd + 1) * dh)
        if finish:
            tot = outs[hd] + of_ref[:, sl]
            y = tot * lax.rsqrt(jnp.mean(tot * tot, axis=-1, keepdims=True) + NORM_EPS)
            o_ref[:, sl] = (y * _sigmoid(og_ref[:, sl])).astype(o_ref.dtype)
        else:
            o_ref[:, sl] = outs[hd]


def _ml_pass(u, sbias, o_f, *, rev, nb, nc, nl):
    t = T_CHUNK
    ntot = u.shape[0]
    finish = o_f is not None
    rb = functools.partial(_row_block, rev=rev, nc=nc, nl=nl, nb=nb)
    in_specs = [_col_spec(t, GW, C_ML_Q, rb), _col_spec(t, GW, C_ML_K, rb),
                _col_spec(t, GW, C_ML_V, rb), _col_spec(t, LANE, C_SMALL, rb),
                pl.BlockSpec((1, LANE), lambda b, i: (0, 0))]
    args = [u, u, u, u, sbias]
    if finish:
        in_specs += [_col_spec(t, GW, C_ML_O, rb), _col_spec(t, GW, 0, rb)]
        args += [u, o_f]
    return pl.pallas_call(
        functools.partial(_ml_kernel, rev=rev, finish=finish),
        grid=(nb, nc + nl),
        in_specs=in_specs,
        out_specs=_col_spec(t, GW, 0, rb),
        out_shape=jax.ShapeDtypeStruct((ntot, GW), BF16 if finish else F32),
        scratch_shapes=[pltpu.VMEM((ML_HEADS, ML_DH, ML_DH), F32),
                        pltpu.VMEM((SUB, ML_DH), F32),
                        pltpu.VMEM((SUB, LANE), F32)],
        compiler_params=_cparams(("parallel", "arbitrary")),
        name="mlstm_bwd_finish" if rev else "mlstm_fwd",
    )(*args)


def _conv4(xe_scr, x, prev8, next8, w, first, last):
    t = x.shape[0]
    xe_scr[0:SUB, :] = jnp.where(first, 0.0, prev8)
    xe_scr[SUB:SUB + t, :] = x
    xe_scr[SUB + t:SUB + t + SUB, :] = jnp.where(last, 0.0, next8)
    out = w[0:1, :] * xe_scr[SUB - 2:SUB - 2 + t, :]
    for j in range(1, 4):
        out = out + w[j:j + 1, :] * xe_scr[SUB - 2 + j:SUB - 2 + j + t, :]
    return out


def _ssd_kernel(*refs, rev, finish, nc, nl):
    if finish:
        (x_ref, xp_ref, xn_ref, sm_ref, sb_ref, arow_ref, cw_ref,
         z_ref, of_ref, dsk_ref, ng_ref, o_ref, st_scr, xe_scr) = refs
    else:
        x_ref, xp_ref, xn_ref, sm_ref, sb_ref, arow_ref, cw_ref, o_ref, st_scr, xe_scr = refs
    t = x_ref.shape[0]
    p = SSD_HEADDIM
    n = SSD_STATE
    d = 1 if rev else 0
    i = pl.program_id(1)

    @pl.when(i == 0)
    def _():
        st_scr[...] = jnp.zeros_like(st_scr)

    is_ctx, c = _chunk_nat(i, rev, nc, nl)
    first = c == 0
    last = c == jnp.where(is_ctx, nc, nl) - 1
    xc = _conv4(xe_scr, x_ref[...], xp_ref[...], xn_ref[...], cw_ref[...], first, last)
    xc = xc * _sigmoid(xc)
    xs = xc[:, :SSD_HEADS * p]
    bm = xc[:, SSD_HEADS * p:SSD_HEADS * p + SSD_GROUPS * n]
    cm = xc[:, SSD_HEADS * p + SSD_GROUPS * n:]

    dl = _softplus(sm_ref[...] + sb_ref[...])
    lfm = dl * arow_ref[...]
    row, col = _iota2(t)
    valid = (col >= row) if rev else (col <= row)
    bcol = _split_dot(_bf(valid.astype(F32)), lfm)
    tot_row = jnp.sum(lfm, axis=0, keepdims=True)
    b_t = bcol.T
    dl_t = dl.T

    xsb = _bf(xs)
    rep = SSD_HEADS // SSD_GROUPS
    outs = []
    for gi in range(SSD_GROUPS):
        bg = bm[:, gi * n:(gi + 1) * n]
        cgb = _bf(cm[:, gi * n:(gi + 1) * n])
        gmat = _dot_nt(cgb, _bf(bg))
        for hh in range(rep):
            hd = gi * rep + hh
            cc = SM_DT + SSD_HEADS * d + hd
            bc = bcol[:, cc:cc + 1]
            br = b_t[cc:cc + 1, :]
            dr = dl_t[cc:cc + 1, :]
            dcol = dl[:, cc:cc + 1]
            tot = tot_row[:, cc:cc + 1]
            att = gmat * jnp.exp(jnp.where(valid, bc - br, NEG)) * dr
            xh = xsb[:, hd * p:(hd + 1) * p]
            st = st_scr[hd]
            o = _dot(_bf(att), xh) + jnp.exp(bc) * _dot_nt(cgb, _bf(st))
            wk = dcol * jnp.exp(tot - bc)
            st_scr[hd] = jnp.exp(tot) * st + _dot_tn(xh, _bf(bg * wk))
            outs.append(o)
    o_all = jnp.concatenate(outs, axis=-1)

    if finish:
        y = (o_all + of_ref[...] + dsk_ref[...] * xs)
        z = z_ref[...]
        y = y * (z * _sigmoid(z))
        gw = SSD_HEADS * p // SSD_GROUPS
        for gi in range(SSD_GROUPS):
            sl = slice(gi * gw, (gi + 1) * gw)
            yg = y[:, sl]
            yg = yg * lax.rsqrt(jnp.mean(yg * yg, axis=-1, keepdims=True) + NORM_EPS)
            o_ref[:, sl] = (yg * ng_ref[:, sl]).astype(o_ref.dtype)
    else:
        o_ref[...] = o_all


def _ssd_pass(u, sbias, arow, conv_w, z_extra, o_f, *, rev, nb, nc, nl):
    t = T_CHUNK
    ntot = u.shape[0]
    finish = o_f is not None
    rb = functools.partial(_row_block, rev=rev, nc=nc, nl=nl, nb=nb)
    r8 = t // SUB
    nblk8 = ntot // SUB
    cb = C_SSD_XBC // 1024

    def prev_map(b, i):
        return (jnp.maximum(rb(b, i) * r8 - 1, 0), cb)

    def next_map(b, i):
        return (jnp.minimum((rb(b, i) + 1) * r8, nblk8 - 1), cb)

    const2 = lambda b, i: (0, 0)
    in_specs = [_col_spec(t, 1024, C_SSD_XBC, rb),
                pl.BlockSpec((SUB, 1024), prev_map), pl.BlockSpec((SUB, 1024), next_map),
                _col_spec(t, LANE, C_SMALL, rb),
                pl.BlockSpec((1, LANE), const2), pl.BlockSpec((1, LANE), const2),
                pl.BlockSpec((4, 1024), const2)]
    args = [u, u, u, u, sbias, arow, conv_w]
    if finish:
        dsk, ng = z_extra
        in_specs += [_col_spec(t, GW, C_SSD_Z, rb), _col_spec(t, GW, 0, rb),
                     pl.BlockSpec((1, GW), const2), pl.BlockSpec((1, GW), const2)]
        args += [u, o_f, dsk, ng]
    return pl.pallas_call(
        functools.partial(_ssd_kernel, rev=rev, finish=finish, nc=nc, nl=nl),
        grid=(nb, nc + nl),
        in_specs=in_specs,
        out_specs=_col_spec(t, GW, 0, rb),
        out_shape=jax.ShapeDtypeStruct((ntot, GW), BF16 if finish else F32),
        scratch_shapes=[pltpu.VMEM((SSD_HEADS, SSD_HEADDIM, SSD_STATE), F32),
                        pltpu.VMEM((t + 2 * SUB, 1024), F32)],
        compiler_params=_cparams(("parallel", "arbitrary")),
        name="ssd_bwd_finish" if rev else "ssd_fwd",
    )(*args)


def _lru_kernel(*refs, rev, finish, nc, nl):
    if finish:
        (x_ref, xp_ref, xn_ref, cw_ref, w2_ref, b2_ref, crow_ref,
         gate_ref, of_ref, o_ref, h_scr, xe_scr) = refs
    else:
        x_ref, xp_ref, xn_ref, cw_ref, w2_ref, b2_ref, crow_ref, o_ref, h_scr, xe_scr = refs
    t = x_ref.shape[0]
    w = x_ref.shape[1]
    ng = t // SUB
    i = pl.program_id(1)

    @pl.when(i == 0)
    def _():
        h_scr[...] = jnp.zeros_like(h_scr)

    is_ctx, c = _chunk_nat(i, rev, nc, nl)
    first = c == 0
    last = c == jnp.where(is_ctx, nc, nl) - 1
    xr = _conv4(xe_scr, x_ref[...], xp_ref[...], xn_ref[...], cw_ref[...], first, last)
    pre = _dot(_bf(xr), w2_ref[0]) + b2_ref[0]
    r = _sigmoid(pre[:, :w])
    ig = _sigmoid(pre[:, w:])
    log_a = crow_ref[0] * r
    a = jnp.exp(log_a)
    bb = jnp.sqrt(jnp.tanh(-log_a) * (1.0 + a * a)) * (ig * xr)

    a3 = a.reshape(ng, SUB, w)
    b3 = bb.reshape(ng, SUB, w)
    r8 = lax.broadcasted_iota(jnp.int32, (ng, SUB, w), 1)
    pos = (SUB - 1 - r8) if rev else r8
    step = 1
    while step < SUB:
        sh = (SUB - step) if rev else step
        a_s = pltpu.roll(a3, sh, 1)
        b_s = pltpu.roll(b3, sh, 1)
        ok = pos >= step
        b3 = jnp.where(ok, a3 * b_s + b3, b3)
        a3 = jnp.where(ok, a3 * a_s, a3)
        step *= 2
    h = h_scr[...]
    groups = [None] * ng
    for g in (range(ng - 1, -1, -1) if rev else range(ng)):
        hg = b3[g] + a3[g] * h
        groups[g] = hg
        h = hg[0:1, :] if rev else hg[SUB - 1:SUB, :]
    h_scr[...] = h
    hs = jnp.concatenate(groups, axis=0)

    if finish:
        g = gate_ref[...]
        gelu = 0.5 * g * (1.0 + jnp.tanh(math.sqrt(2.0 / math.pi) * (g + 0.044715 * (g * g * g))))
        o_ref[...] = ((hs + of_ref[...]) * gelu).astype(o_ref.dtype)
    else:
        o_ref[...] = hs


def _lru_pass(ul, consts, o_f, *, rev, nb, nc, nl):
    t = T_CHUNK
    ntot = ul.shape[0]
    finish = o_f is not None
    d = 1 if rev else 0
    rb = functools.partial(_row_block, rev=rev, nc=nc, nl=nl, nb=nb)
    r8 = t // SUB
    nblk8 = ntot // SUB
    cw, w2, b2, crow = consts
    const2 = lambda b, i: (0, 0)
    dsel = lambda b, i: (d, 0, 0)
    in_specs = [_col_spec(t, GW, GW, rb),
                pl.BlockSpec((SUB, GW), lambda b, i: (jnp.maximum(rb(b, i) * r8 - 1, 0), 1)),
                pl.BlockSpec((SUB, GW), lambda b, i: (jnp.minimum((rb(b, i) + 1) * r8, nblk8 - 1), 1)),
                pl.BlockSpec((4, GW), const2),
                pl.BlockSpec((1, GW, 2 * GW), dsel), pl.BlockSpec((1, 1, 2 * GW), dsel),
                pl.BlockSpec((1, 1, GW), dsel)]
    args = [ul, ul, ul, cw, w2, b2, crow]
    if finish:
        in_specs += [_col_spec(t, GW, 0, rb), _col_spec(t, GW, 0, rb)]
        args += [ul, o_f]
    return pl.pallas_call(
        functools.partial(_lru_kernel, rev=rev, finish=finish, nc=nc, nl=nl),
        grid=(nb, nc + nl),
        in_specs=in_specs,
        out_specs=_col_spec(t, GW, 0, rb),
        out_shape=jax.ShapeDtypeStruct((ntot, GW), BF16 if finish else F32),
        scratch_shapes=[pltpu.VMEM((1, GW), F32), pltpu.VMEM((t + 2 * SUB, GW), F32)],
        compiler_params=_cparams(("parallel", "arbitrary")),
        name="rglru_bwd_finish" if rev else "rglru_fwd",
    )(*args)


def _lru_mixer(u, consts, *, nb, seq, nc, nl):
    n_lat = nb * seq
    rows = seq // GRID_W
    assert rows == T_CHUNK, "one scan chunk must be one latent grid column"
    ul = u[:, C_LRU_GATE:C_LRU_GATE + 2 * GW]
    lat = ul[:n_lat].reshape(nb, rows, GRID_W, 2 * GW).transpose(0, 2, 1, 3).reshape(n_lat, 2 * GW)
    ul = jnp.concatenate([lat, ul[n_lat:]], axis=0)
    o = _lru_pass(ul, consts, None, rev=False, nb=nb, nc=nc, nl=nl)
    y = _lru_pass(ul, consts, o, rev=True, nb=nb, nc=nc, nl=nl)
    lat = y[:n_lat].reshape(nb, GRID_W, rows, GW).transpose(0, 2, 1, 3).reshape(n_lat, GW)
    return jnp.concatenate([lat, y[n_lat:]], axis=0)


def _outproj_kernel(yh_ref, ys_ref, yl_ref, ym_ref, w_ref, x_ref, mod_ref, g_ref, wr_ref,
                    xn_ref, h_ref, lg_ref):
    acc = _dot(yh_ref[...], w_ref[0])
    acc = acc + _dot(ys_ref[...], w_ref[1])
    acc = acc + _dot(yl_ref[...], w_ref[2])
    acc = acc + _dot(ym_ref[...], w_ref[3])
    xn = x_ref[...] + mod_ref[0, 2:3, :] * acc
    xn_ref[...] = xn
    h = _rms_mod(xn, g_ref[...], mod_ref[0, 3:4, :], mod_ref[0, 4:5, :])
    h_ref[...] = h
    h1 = _bf(h)
    h2 = _bf(h - h1.astype(F32))
    p = _dot(h1, wr_ref[...])
    lg_ref[...] = p[:, :LANE] + p[:, LANE:] + _dot(h2, wr_ref[:, :LANE])


def _out_proj(ys, w_out4, x, mod, g2, w_router, *, n_lat, seq, b):
    ntot, d = x.shape
    tm = 512
    row = lambda i: (i, 0)
    mrow = _mod_row_map(tm, n_lat, seq, b)
    in_specs = [pl.BlockSpec((tm, GW), row)] * 4 + [
        pl.BlockSpec((4, GW, d), lambda i: (0, 0, 0)),
        pl.BlockSpec((tm, d), row), pl.BlockSpec((1, 6, d), mrow),
        pl.BlockSpec((1, d), lambda i: (0, 0)), pl.BlockSpec((d, 2 * LANE), lambda i: (0, 0))]
    return pl.pallas_call(
        _outproj_kernel,
        grid=(ntot // tm,),
        in_specs=in_specs,
        out_specs=[pl.BlockSpec((tm, d), row), pl.BlockSpec((tm, d), row), pl.BlockSpec((tm, LANE), row)],
        out_shape=[jax.ShapeDtypeStruct((ntot, d), F32), jax.ShapeDtypeStruct((ntot, d), F32),
                   jax.ShapeDtypeStruct((ntot, LANE), F32)],
        compiler_params=_cparams(("parallel",)),
        name="out_proj_norm2_router",
    )(*ys, w_out4, x, mod, g2, w_router)


def _row_copy(h_hbm, xbuf, sem, tok, slot, r):
    return pltpu.make_async_copy(h_hbm.at[pl.ds(tok, 1), :], xbuf.at[slot, pl.ds(r, 1), :], sem.at[slot])


def _moe_kernel(be_ref, tok_ref, h_hbm, wg_ref, wu_ref, wd_ref, o_ref, xbuf, sem, wgu_scr, wd_scr):
    i = pl.program_id(0)
    nblk = pl.num_programs(0)
    de = wg_ref.shape[-1]
    slot = i % 2

    def start_gather(blk, s):
        for r in range(MOE_BLOCK):
            _row_copy(h_hbm, xbuf, sem, tok_ref[blk * MOE_BLOCK + r], s, r).start()

    @pl.when(i == 0)
    def _():
        start_gather(0, 0)

    @pl.when(jnp.logical_or(i == 0, be_ref[i] != be_ref[jnp.maximum(i - 1, 0)]))
    def _():
        wgu_scr[:, :de] = _bf(wg_ref[0])
        wgu_scr[:, de:] = _bf(wu_ref[0])
        wd_scr[...] = _bf(wd_ref[0])

    for r in range(MOE_BLOCK):
        _row_copy(h_hbm, xbuf, sem, 0, slot, r).wait()
    start_gather(jnp.where(i + 1 < nblk, i + 1, 0), 1 - slot)

    gu = _dot(_bf(xbuf[slot]), wgu_scr[...])
    g = gu[:, :de]
    hid = (g * _sigmoid(g)) * gu[:, de:]
    o_ref[...] = _bf(_dot(_bf(hid), wd_scr[...]))

    @pl.when(i == nblk - 1)
    def _():
        for r in range(MOE_BLOCK):
            _row_copy(h_hbm, xbuf, sem, 0, 1 - slot, r).wait()


def _moe_blocks(blk_exp, slot_tok, h, wg, wu, wd):
    nslot = slot_tok.shape[0]
    d = h.shape[1]
    de = wg.shape[-1]
    nblk = nslot // MOE_BLOCK
    return pl.pallas_call(
        _moe_kernel,
        grid_spec=pltpu.PrefetchScalarGridSpec(
            num_scalar_prefetch=2,
            grid=(nblk,),
            in_specs=[pl.BlockSpec(memory_space=pl.ANY),
                      pl.BlockSpec((1, d, de), lambda i, be, tk: (be[i], 0, 0)),
                      pl.BlockSpec((1, d, de), lambda i, be, tk: (be[i], 0, 0)),
                      pl.BlockSpec((1, de, d), lambda i, be, tk: (be[i], 0, 0))],
            out_specs=pl.BlockSpec((MOE_BLOCK, d), lambda i, be, tk: (i, 0)),
            scratch_shapes=[pltpu.VMEM((2, MOE_BLOCK, d), F32), pltpu.SemaphoreType.DMA((2,)),
                            pltpu.VMEM((d, 2 * de), BF16), pltpu.VMEM((de, d), BF16)]),
        out_shape=jax.ShapeDtypeStruct((nslot, d), BF16),
        compiler_params=_cparams(("arbitrary",)),
        name="moe_expert_blocks",
    )(blk_exp, slot_tok, h, wg, wu, wd)


def _route(logits):
    n_tok = logits.shape[0]
    n_exp = N_EXPERT_GROUPS * EXPERTS_PER_GROUP
    g_logits = logits[:, :N_EXPERT_GROUPS]
    g_prob = jax.nn.softmax(g_logits, axis=-1)
    g_sel = jnp.argmax(g_logits, axis=-1)
    g_hot = g_sel[:, None] == jnp.arange(N_EXPERT_GROUPS)[None, :]
    e_logits = logits[:, N_EXPERT_GROUPS:N_EXPERT_GROUPS + n_exp].reshape(
        n_tok, N_EXPERT_GROUPS, EXPERTS_PER_GROUP)
    within = jnp.sum(jnp.where(g_hot[:, :, None], e_logits, 0.0), axis=1)
    top_v, top_i = lax.top_k(within, TOP_K)
    wts = jax.nn.softmax(top_v, axis=-1) * jnp.sum(jnp.where(g_hot, g_prob, 0.0), axis=1, keepdims=True)
    eid = (g_sel[:, None] * EXPERTS_PER_GROUP + top_i).astype(jnp.int32)

    n_assign = n_tok * TOP_K
    flat_e = eid.reshape(-1)
    onehot = (flat_e[:, None] == jnp.arange(n_exp, dtype=jnp.int32)[None, :]).astype(jnp.int32)
    csum = jnp.cumsum(onehot, axis=0)
    counts = csum[-1]
    padded = (counts + MOE_BLOCK - 1) // MOE_BLOCK * MOE_BLOCK
    pend = jnp.cumsum(padded)
    pstart = pend - padded
    dest = jnp.sum(onehot * (csum - 1 + pstart[None, :]), axis=1)
    n_blk = -(-n_assign // MOE_BLOCK) + n_exp
    tok = jnp.arange(n_assign, dtype=jnp.int32) // TOP_K
    slot_tok = jnp.zeros((n_blk * MOE_BLOCK,), jnp.int32).at[dest].set(tok)
    blk_start = jnp.arange(n_blk, dtype=jnp.int32) * MOE_BLOCK
    blk_exp = jnp.minimum(jnp.sum(blk_start[:, None] >= pend[None, :], axis=1), n_exp - 1).astype(jnp.int32)
    return wts, dest, slot_tok, blk_exp


def _moe(h2, logits, layer, wg, wu, wd):
    n_tok = h2.shape[0]
    n_exp = wg.shape[1]
    wts, dest, slot_tok, blk_exp = _route(logits)
    flat = lambda w: w.reshape((w.shape[0] * n_exp,) + w.shape[2:])
    yb = _moe_blocks(blk_exp + layer * n_exp, slot_tok, h2, flat(wg), flat(wu), flat(wd))
    dest = dest.reshape(n_tok, TOP_K)
    take = functools.partial(jnp.take, axis=0, mode="clip")
    return take(yb, dest[:, 0]), take(yb, dest[:, 1]), wts


def _final_kernel(x_ref, y0_ref, y1_ref, wt_ref, pmod_ref, g_ref, o_ref):
    y = wt_ref[:, 0:1] * y0_ref[...].astype(F32) + wt_ref[:, 1:2] * y1_ref[...].astype(F32)
    x = x_ref[...] + pmod_ref[0, 5:6, :] * y
    o_ref[...] = x * lax.rsqrt(jnp.mean(x * x, axis=-1, keepdims=True) + NORM_EPS) * g_ref[...]


def _final(x, ymoe, pmod, g, *, n_lat, seq, b):
    d = x.shape[1]
    tm = 512
    row = lambda i: (i, 0)
    y0, y1, wts = ymoe
    return pl.pallas_call(
        _final_kernel,
        grid=(n_lat // tm,),
        in_specs=[pl.BlockSpec((tm, d), row), pl.BlockSpec((tm, d), row), pl.BlockSpec((tm, d), row),
                  pl.BlockSpec((tm, TOP_K), row),
                  pl.BlockSpec((1, 6, d), _mod_row_map(tm, n_lat, seq, b)),
                  pl.BlockSpec((1, d), lambda i: (0, 0))],
        out_specs=pl.BlockSpec((tm, d), row),
        out_shape=jax.ShapeDtypeStruct((n_lat, d), F32),
        compiler_params=_cparams(("parallel",)),
        name="final_residual_norm",
    )(x, y0, y1, wts, pmod, g)


def _arrange_w_in(w_in_l):
    d = w_in_l.shape[0]
    hg = w_in_l[:, :5 * GW]
    o = 5 * GW
    ssd_z = w_in_l[:, o:o + GW]
    ssd_xbc = w_in_l[:, o + GW:o + GW + 1024]
    ssd_dt = w_in_l[:, o + GW + 1024:o + GW + 1024 + 2 * SSD_HEADS]
    o += GW + 1024 + 2 * SSD_HEADS
    lru = w_in_l[:, o:o + 2 * GW]
    o += 2 * GW
    ml_main = w_in_l[:, o:o + 4 * GW]
    ml_gates = w_in_l[:, o + 4 * GW:o + 4 * GW + 4 * ML_HEADS]
    small = jnp.concatenate([ssd_dt, ml_gates], axis=1)
    pad = jnp.zeros((d, NCOLS - C_SMALL - small.shape[1]), w_in_l.dtype)
    return jnp.concatenate([hg, ssd_z, ssd_xbc, lru, ml_main, small, pad], axis=1)


def _blockdiag_dense(w):
    n, bw, _ = w.shape
    eye = jnp.eye(n, dtype=w.dtype)
    return (eye[:, None, :, None] * w[:, :, None, :]).reshape(n * bw, n * bw)


def _pad_lanes(v):
    return jnp.concatenate([v, jnp.zeros((LANE - v.shape[0],), v.dtype)])[None, :]


def kernel(x, c, ctx, c_ctx, ada_w, ada_b, norm1_g, norm2_g, w_in, w_out, hg_lb_logits, ssd_conv_w, ssd_a_log, ssd_dt_bias, ssd_d, ssd_norm_g, lru_conv_w, lru_wa, lru_ba, lru_wx, lru_bx, lru_lambda, ml_b_i, ml_b_f, router_group_w, router_expert_w, moe_w_gate, moe_w_up, moe_w_down, final_norm_g):
    nb, seq, d = x.shape
    nctx = ctx.shape[1]
    depth = ada_w.shape[0]
    n_lat = nb * seq
    t = T_CHUNK
    nc, nl = nctx // t, seq // t
    kw = dict(nb=nb, nc=nc, nl=nl)
    lay = dict(n_lat=n_lat, seq=seq, b=nb)

    p = jax.nn.softmax(hg_lb_logits.astype(F32), axis=0)
    lb_all = (jnp.cumsum(p, axis=0) - p[:1])[:, :, None, :]

    c_all = jnp.concatenate([c, c_ctx[None, :], jnp.zeros((SUB - nb - 1, d), F32)], axis=0)
    mods = _modulation(c_all, ada_w, ada_b).reshape(depth, SUB, 6, d)

    xall = jnp.concatenate([x.reshape(n_lat, d), ctx.reshape(nb * nctx, d)], axis=0)
    y_moe = None
    tn = 512
    for l in range(depth):
        w_l = _bf(_arrange_w_in(w_in[l])).reshape(d, NCOLS // tn, tn).transpose(1, 0, 2)
        xall, h1 = _res_norm(xall, y_moe, mods[l - 1] if l else None, norm1_g[l][None, :], mods[l], **lay)
        u = _in_proj(h1, w_l)

        o = _hg_pass(u, lb_all[l], None, rev=False, **kw)
        y_hg = _hg_pass(u, lb_all[l], o, rev=True, **kw)

        sbias = _pad_lanes(jnp.concatenate([ssd_dt_bias[l, 0], ssd_dt_bias[l, 1],
                                            ml_b_i[l, 0], ml_b_i[l, 1], ml_b_f[l, 0], ml_b_f[l, 1]]))
        arow = _pad_lanes(jnp.concatenate([-jnp.exp(ssd_a_log[l, 0]), -jnp.exp(ssd_a_log[l, 1])]))
        dsk = jnp.repeat(ssd_d[l], SSD_HEADDIM)[None, :]
        o = _ssd_pass(u, sbias, arow, ssd_conv_w[l], None, None, rev=False, **kw)
        y_ssd = _ssd_pass(u, sbias, arow, ssd_conv_w[l], (dsk, ssd_norm_g[l][None, :]), o, rev=True, **kw)

        w2 = _bf(jnp.stack([jnp.concatenate([_blockdiag_dense(lru_wa[l, dd]), _blockdiag_dense(lru_wx[l, dd])],
                                            axis=1) for dd in range(2)]))
        b2 = jnp.stack([jnp.concatenate([lru_ba[l, dd], lru_bx[l, dd]])[None, :] for dd in range(2)])
        crow = (-LRU_C * jax.nn.softplus(-lru_lambda[l]))[:, None, :]
        y_lru = _lru_mixer(u, (lru_conv_w[l], w2, b2, crow), nb=nb, seq=seq, nc=nc, nl=nl)

        o = _ml_pass(u, sbias, None, rev=False, **kw)
        y_ml = _ml_pass(u, sbias, o, rev=True, **kw)

        w_router = jnp.concatenate(
            [router_group_w[l], router_expert_w[l],
             jnp.zeros((d, LANE - N_EXPERT_GROUPS * (1 + EXPERTS_PER_GROUP)), F32)], axis=1)
        wr1 = _bf(w_router)
        wr12 = jnp.concatenate([wr1, _bf(w_router - wr1.astype(F32))], axis=1)
        xall, h2, logits = _out_proj((y_hg, y_ssd, y_lru, y_ml), _bf(w_out[l]).reshape(4, GW, d), xall,
                                     mods[l], norm2_g[l][None, :], wr12, **lay)
        y_moe = _moe(h2, logits, l, moe_w_gate, moe_w_up, moe_w_down)

    out = _final(xall, y_moe, mods[depth - 1], final_norm_g[None, :], **lay)
    return out.reshape(nb, seq, d)
```

```python
import functools
import math

import jax
import jax.numpy as jnp
from jax import lax
from jax.experimental import pallas as pl
from jax.experimental.pallas import tpu as pltpu

F32 = jnp.float32
BF16 = jnp.bfloat16

GRID_W = 64
HG_HEADS, HG_DK = 4, 128
SSD_HEADS, SSD_HEADDIM, SSD_GROUPS, SSD_STATE = 8, 64, 2, 128
LRU_C = 8.0
ML_HEADS, ML_DH = 4, 128
N_EXPERT_GROUPS, EXPERTS_PER_GROUP, TOP_K = 4, 8, 2
MOE_BLOCK = 256
NORM_EPS = 1e-6
NEG = -1e30

GW = 512
C_HG_Q, C_HG_I, C_HG_G, C_HG_ZF, C_HG_ZB = 0, 512, 1024, 1536, 2048
C_SSD_Z, C_SSD_XBC = 2560, 3072
C_LRU_GATE, C_LRU_XR = 4096, 4608
C_ML_Q, C_ML_K, C_ML_V, C_ML_O = 5120, 5632, 6144, 6656
C_SMALL = 7168
SM_DT, SM_IG, SM_FG = 0, 16, 24
NCOLS = 7680

LANE = 128
SUB = 8
VMEM_LIMIT = 48 * 1024 * 1024

T_CHUNK = 128


def _cparams(sem):
    return pltpu.CompilerParams(dimension_semantics=sem, vmem_limit_bytes=VMEM_LIMIT)


def _dot(a, b):
    return jnp.dot(a, b, preferred_element_type=F32)


def _dot_nt(a, b):
    return lax.dot_general(a, b, (((1,), (1,)), ((), ())), preferred_element_type=F32)


def _dot_tn(a, b):
    return lax.dot_general(a, b, (((0,), (0,)), ((), ())), preferred_element_type=F32)


def _bf(x):
    return x.astype(BF16)


def _sigmoid(x):
    return 0.5 * jnp.tanh(0.5 * x) + 0.5


def _softplus(x):
    return jnp.maximum(x, 0.0) + jnp.log1p(jnp.exp(-jnp.abs(x)))


def _split_dot(mask_bf, x):
    hi = _bf(x)
    r1 = x - hi.astype(F32)
    mid = _bf(r1)
    lo = _bf(r1 - mid.astype(F32))
    return _dot(mask_bf, hi) + _dot(mask_bf, mid) + _dot(mask_bf, lo)


def _rollb(x, n, rev):
    if n == 0:
        return x
    t = x.shape[0]
    return pltpu.roll(x, (t - n) if rev else n, 0)


def _iota2(t):
    return (lax.broadcasted_iota(jnp.int32, (t, t), 0), lax.broadcasted_iota(jnp.int32, (t, t), 1))


def _mod_kernel(c_ref, w_ref, b_ref, o_ref):
    c = c_ref[...]
    s = c * _sigmoid(c)
    o_ref[0] = jnp.dot(s, w_ref[0], precision=lax.Precision.HIGHEST,
                       preferred_element_type=F32) + b_ref[0]


def _modulation(c_all, ada_w, ada_b):
    depth, d, n6 = ada_w.shape
    tn = 1024
    return pl.pallas_call(
        _mod_kernel,
        grid=(depth, n6 // tn),
        in_specs=[pl.BlockSpec((SUB, d), lambda l, j: (0, 0)),
                  pl.BlockSpec((1, d, tn), lambda l, j: (l, 0, j)),
                  pl.BlockSpec((1, 1, tn), lambda l, j: (l, 0, j))],
        out_specs=pl.BlockSpec((1, SUB, tn), lambda l, j: (l, 0, j)),
        out_shape=jax.ShapeDtypeStruct((depth, SUB, n6), F32),
        compiler_params=_cparams(("parallel", "parallel")),
        name="adaln_modulation",
    )(c_all, ada_w, ada_b.reshape(depth, 1, n6))


def _rms_mod(x, g, shift, scale):
    y = x * lax.rsqrt(jnp.mean(x * x, axis=-1, keepdims=True) + NORM_EPS) * g
    return y * (1.0 + scale) + shift


def _mod_row_map(tm, n_lat, seq, b):
    def f(i, *_):
        return (jnp.where(i * tm < n_lat, (i * tm) // seq, b), 0, 0)
    return f


def _resnorm_kernel(*refs, residual):
    if residual:
        x_ref, y0_ref, y1_ref, wt_ref, pmod_ref, g_ref, mod_ref, xn_ref, h_ref = refs
        y = (wt_ref[:, 0:1] * y0_ref[...].astype(F32) + wt_ref[:, 1:2] * y1_ref[...].astype(F32))
        x = x_ref[...] + pmod_ref[0, 5:6, :] * y
        xn_ref[...] = x
    else:
        x_ref, g_ref, mod_ref, h_ref = refs
        x = x_ref[...]
    h_ref[...] = _bf(_rms_mod(x, g_ref[...], mod_ref[0, 0:1, :], mod_ref[0, 1:2, :]))


def _res_norm(x, ymoe, pmod, g1, mod, *, n_lat, seq, b):
    ntot, d = x.shape
    tm = 512
    residual = ymoe is not None
    row = lambda i: (i, 0)
    mrow = _mod_row_map(tm, n_lat, seq, b)
    in_specs = [pl.BlockSpec((tm, d), row)]
    args = [x]
    if residual:
        y0, y1, wts = ymoe
        in_specs += [pl.BlockSpec((tm, d), row), pl.BlockSpec((tm, d), row),
                     pl.BlockSpec((tm, TOP_K), row), pl.BlockSpec((1, 6, d), mrow)]
        args += [y0, y1, wts, pmod]
    in_specs += [pl.BlockSpec((1, d), lambda i: (0, 0)), pl.BlockSpec((1, 6, d), mrow)]
    args += [g1, mod]
    out_specs = [pl.BlockSpec((tm, d), row)]
    out_shape = [jax.ShapeDtypeStruct((ntot, d), BF16)]
    if residual:
        out_specs = [pl.BlockSpec((tm, d), row)] + out_specs
        out_shape = [jax.ShapeDtypeStruct((ntot, d), F32)] + out_shape
    res = pl.pallas_call(
        functools.partial(_resnorm_kernel, residual=residual),
        grid=(ntot // tm,),
        in_specs=in_specs, out_specs=out_specs, out_shape=out_shape,
        compiler_params=_cparams(("parallel",)),
        name="residual_norm1",
    )(*args)
    return (res[0], res[1]) if residual else (x, res[0])


def _inproj_kernel(h_ref, w_ref, u_ref):
    u_ref[...] = _dot(h_ref[...], w_ref[0])


def _in_proj(h, w_tiles):
    ntot, d = h.shape
    nt, _, tn = w_tiles.shape
    tm = 1536
    assert ntot % tm == 0
    return pl.pallas_call(
        _inproj_kernel,
        grid=(ntot // tm, nt),
        in_specs=[pl.BlockSpec((tm, d), lambda i, j: (i, 0)),
                  pl.BlockSpec((1, d, tn), lambda i, j: (j, 0, 0))],
        out_specs=pl.BlockSpec((tm, tn), lambda i, j: (i, j)),
        out_shape=jax.ShapeDtypeStruct((ntot, nt * tn), F32),
        compiler_params=_cparams(("parallel", "arbitrary")),
        name="in_proj",
    )(h, w_tiles)


def _chunk_nat(i, rev, nc, nl):
    is_ctx = i < nc
    if rev:
        c = jnp.where(is_ctx, nc - 1 - i, nl - 1 - (i - nc))
    else:
        c = jnp.where(is_ctx, i, i - nc)
    return is_ctx, c


def _row_block(b, i, *, rev, nc, nl, nb):
    is_ctx, c = _chunk_nat(i, rev, nc, nl)
    return jnp.where(is_ctx, nb * nl + b * nc + c, b * nl + c)


def _col_spec(t, width, col, rb):
    cb = col // width
    return pl.BlockSpec((t, width), lambda b, i: (rb(b, i), cb))


def _hg_kernel(*refs, rev, finish):
    if finish:
        q_ref, v_ref, z_ref, lb_ref, g_ref, of_ref, o_ref, st_scr = refs
    else:
        q_ref, v_ref, z_ref, lb_ref, o_ref, st_scr = refs
    t = q_ref.shape[0]
    dk = HG_DK

    @pl.when(pl.program_id(1) == 0)
    def _():
        st_scr[...] = jnp.zeros_like(st_scr)

    lb = lb_ref[0]
    sig = _sigmoid(z_ref[...])
    f = lb + (1.0 - lb) * sig
    lf = jnp.log(f)
    k = (1.0 - lb) * (1.0 - sig)
    q = q_ref[...]
    v = v_ref[...]

    w = q.shape[1]
    row, col = _iota2(t)
    valid = (col >= row) if rev else (col <= row)
    lvl = jnp.where(valid, 31 - lax.clz(row ^ col), -2)
    b = _split_dot(_bf(valid.astype(F32)), lf)
    total = jnp.sum(lf, axis=0, keepdims=True)

    tpos = lax.broadcasted_iota(jnp.int32, (t, w), 0)
    b8 = b.reshape(t // SUB, SUB, w)
    r8 = lax.broadcasted_iota(jnp.int32, (t // SUB, SUB, w), 1)
    ops = [(-1, _bf(q), _bf(k))]
    h = 1
    while h < t:
        blk = 2 * h
        lg = int(math.log2(h))
        is_q = ((tpos & h) == 0) if rev else ((tpos & h) != 0)
        if h == 1:
            e = f
        else:
            off = h if rev else h - 1
            if blk <= SUB:
                ref = b8[:, off:off + 1, :]
                for sb in range(1, SUB // blk):
                    ref = jnp.where(r8 >= sb * blk, b8[:, sb * blk + off:sb * blk + off + 1, :], ref)
                ref = jnp.broadcast_to(ref, b8.shape).reshape(t, w)
            else:
                b3 = b.reshape(t // blk, blk, w)
                ref = jnp.broadcast_to(b3[:, off:off + 1, :], b3.shape).reshape(t, w)
            e = jnp.exp(-jnp.abs(b - ref))
        if h == 1:
            m = _bf(jnp.where(is_q, q * e, k))
        else:
            m = _bf(jnp.where(is_q, q, k) * e)
        ops.append((lg, m, m))
        h = blk

    qb = _bf(q * jnp.exp(b))
    kw = _bf(k * jnp.exp(total - b))
    vb = _bf(v)
    dec_tot = jnp.exp(total)

    masks = [lvl == lg for lg, _, _ in ops]
    outs = []
    for hd in range(HG_HEADS):
        sl = slice(hd * dk, (hd + 1) * dk)
        a = jnp.zeros((t, t), F32)
        for msk, (lg, mq, mk) in zip(masks, ops):
            a = jnp.where(msk, _dot_nt(mq[:, sl], mk[:, sl]), a)
        st = st_scr[hd]
        o = _dot(_bf(a), vb[:, sl]) + _dot_nt(qb[:, sl], _bf(st))
        st_scr[hd] = st * dec_tot[:, sl] + _dot_tn(vb[:, sl], kw[:, sl])
        outs.append(o)

    if finish:
        for hd in range(HG_HEADS):
            sl = slice(hd * dk, (hd + 1) * dk)
            tot = outs[hd] + of_ref[:, sl]
            y = tot * lax.rsqrt(jnp.mean(tot * tot, axis=-1, keepdims=True) + NORM_EPS)
            g = g_ref[:, sl]
            o_ref[:, sl] = (y * (g * _sigmoid(g))).astype(o_ref.dtype)
    else:
        for hd in range(HG_HEADS):
            o_ref[:, hd * dk:(hd + 1) * dk] = outs[hd]


def _hg_pass(u, lb, o_f, *, rev, nb, nc, nl):
    t = T_CHUNK
    ntot = u.shape[0]
    finish = o_f is not None
    rb = functools.partial(_row_block, rev=rev, nc=nc, nl=nl, nb=nb)
    d = 1 if rev else 0
    in_specs = [_col_spec(t, GW, C_HG_Q, rb), _col_spec(t, GW, C_HG_I, rb),
                _col_spec(t, GW, C_HG_ZB if rev else C_HG_ZF, rb),
                pl.BlockSpec((1, 1, GW), lambda b, i: (d, 0, 0))]
    args = [u, u, u, lb]
    if finish:
        in_specs += [_col_spec(t, GW, C_HG_G, rb), _col_spec(t, GW, 0, rb)]
        args += [u, o_f]
    return pl.pallas_call(
        functools.partial(_hg_kernel, rev=rev, finish=finish),
        grid=(nb, nc + nl),
        in_specs=in_specs,
        out_specs=_col_spec(t, GW, 0, rb),
        out_shape=jax.ShapeDtypeStruct((ntot, GW), BF16 if finish else F32),
        scratch_shapes=[pltpu.VMEM((HG_HEADS, HG_DK, HG_DK), F32)],
        compiler_params=_cparams(("parallel", "arbitrary")),
        name="hgrn2_bwd_finish" if rev else "hgrn2_fwd",
    )(*args)


def _ml_kernel(*refs, rev, finish):
    if finish:
        q_ref, k_ref, v_ref, sm_ref, sb_ref, og_ref, of_ref, o_ref, ct_scr, n_scr, m_scr = refs
    else:
        q_ref, k_ref, v_ref, sm_ref, sb_ref, o_ref, ct_scr, n_scr, m_scr = refs
    t = q_ref.shape[0]
    dh = ML_DH
    d = 1 if rev else 0

    @pl.when(pl.program_id(1) == 0)
    def _():
        ct_scr[...] = jnp.zeros_like(ct_scr)
        n_scr[...] = jnp.zeros_like(n_scr)
        m_scr[...] = jnp.full(m_scr.shape, NEG, F32)

    sm = sm_ref[...] + sb_ref[...]
    lsg = jnp.minimum(sm, 0.0) - jnp.log1p(jnp.exp(-jnp.abs(sm)))
    row, col = _iota2(t)
    valid = (col >= row) if rev else (col <= row)
    bcol = _split_dot(_bf(valid.astype(F32)), lsg)
    tot_row = jnp.sum(lsg, axis=0, keepdims=True)
    b_t = bcol.T
    sm_t = sm.T

    q = q_ref[...]
    ks = k_ref[...] * (dh ** -0.5)
    qb, kb, vb = _bf(q), _bf(ks), _bf(v_ref[...])

    outs = []
    for hd in range(ML_HEADS):
        sl = slice(hd * dh, (hd + 1) * dh)
        ci = SM_IG + ML_HEADS * d + hd
        cf = SM_FG + ML_HEADS * d + hd
        bc = bcol[:, cf:cf + 1]
        br = b_t[cf:cf + 1, :]
        lir = sm_t[ci:ci + 1, :]
        lic = sm[:, ci:ci + 1]
        m_prev = m_scr[hd:hd + 1, 0:1]
        dmat = jnp.where(valid, bc - br + lir, NEG)
        g = bc + m_prev
        mt = jnp.maximum(g, jnp.max(dmat, axis=-1, keepdims=True))
        w_intra = jnp.exp(dmat - mt)
        w_inter = jnp.exp(g - mt)
        s = _dot_nt(qb[:, sl], kb[:, sl]) * w_intra
        ct = ct_scr[hd]
        nrow = n_scr[hd:hd + 1, :]
        num = _dot(_bf(s), vb[:, sl]) + w_inter * _dot_nt(qb[:, sl], _bf(ct))
        den = (jnp.sum(s, axis=-1, keepdims=True)
               + w_inter * jnp.sum(q[:, sl] * nrow, axis=-1, keepdims=True))
        outs.append(num / jnp.maximum(jnp.abs(den), jnp.exp(-mt)))
        tot = tot_row[:, cf:cf + 1]
        dl = tot - bc + lic
        m_new = jnp.maximum(tot + m_prev, jnp.max(dl, axis=0, keepdims=True))
        ws = jnp.exp(dl - m_new)
        wc = jnp.exp(tot + m_prev - m_new)
        kws = ks[:, sl] * ws
        ct_scr[hd] = wc * ct + _dot_tn(vb[:, sl], _bf(kws))
        n_scr[hd:hd + 1, :] = wc * nrow + jnp.sum(kws, axis=0, keepdims=True)
        m_scr[hd:hd + 1, :] = jnp.broadcast_to(m_new, (1, m_scr.shape[1]))

    for hd in range(ML_HEADS):
        sl = slice(hd * dh, (hd + 1) * dh)
        if finish:
            tot = outs[hd] + of_ref[:, sl]
            y = tot * lax.rsqrt(jnp.mean(tot * tot, axis=-1, keepdims=True) + NORM_EPS)
            o_ref[:, sl] = (y * _sigmoid(og_ref[:, sl])).astype(o_ref.dtype)
        else:
            o_ref[:, sl] = outs[hd]


def _ml_pass(u, sbias, o_f, *, rev, nb, nc, nl):
    t = T_CHUNK
    ntot = u.shape[0]
    finish = o_f is not None
    rb = functools.partial(_row_block, rev=rev, nc=nc, nl=nl, nb=nb)
    in_specs = [_col_spec(t, GW, C_ML_Q, rb), _col_spec(t, GW, C_ML_K, rb),
                _col_spec(t, GW, C_ML_V, rb), _col_spec(t, LANE, C_SMALL, rb),
                pl.BlockSpec((1, LANE), lambda b, i: (0, 0))]
    args = [u, u, u, u, sbias]
    if finish:
        in_specs += [_col_spec(t, GW, C_ML_O, rb), _col_spec(t, GW, 0, rb)]
        args += [u, o_f]
    return pl.pallas_call(
        functools.partial(_ml_kernel, rev=rev, finish=finish),
        grid=(nb, nc + nl),
        in_specs=in_specs,
        out_specs=_col_spec(t, GW, 0, rb),
        out_shape=jax.ShapeDtypeStruct((ntot, GW), BF16 if finish else F32),
        scratch_shapes=[pltpu.VMEM((ML_HEADS, ML_DH, ML_DH), F32),
                        pltpu.VMEM((SUB, ML_DH), F32),
                        pltpu.VMEM((SUB, LANE), F32)],
        compiler_params=_cparams(("parallel", "arbitrary")),
        name="mlstm_bwd_finish" if rev else "mlstm_fwd",
    )(*args)


def _conv4(xe_scr, x, prev8, next8, w, first, last):
    t = x.shape[0]
    xe_scr[0:SUB, :] = jnp.where(first, 0.0, prev8)
    xe_scr[SUB:SUB + t, :] = x
    xe_scr[SUB + t:SUB + t + SUB, :] = jnp.where(last, 0.0, next8)
    out = w[0:1, :] * xe_scr[SUB - 2:SUB - 2 + t, :]
    for j in range(1, 4):
        out = out + w[j:j + 1, :] * xe_scr[SUB - 2 + j:SUB - 2 + j + t, :]
    return out


def _ssd_kernel(*refs, rev, finish, nc, nl):
    if finish:
        (x_ref, xp_ref, xn_ref, sm_ref, sb_ref, arow_ref, cw_ref,
         z_ref, of_ref, dsk_ref, ng_ref, o_ref, st_scr, xe_scr) = refs
    else:
        x_ref, xp_ref, xn_ref, sm_ref, sb_ref, arow_ref, cw_ref, o_ref, st_scr, xe_scr = refs
    t = x_ref.shape[0]
    p = SSD_HEADDIM
    n = SSD_STATE
    d = 1 if rev else 0
    i = pl.program_id(1)

    @pl.when(i == 0)
    def _():
        st_scr[...] = jnp.zeros_like(st_scr)

    is_ctx, c = _chunk_nat(i, rev, nc, nl)
    first = c == 0
    last = c == jnp.where(is_ctx, nc, nl) - 1
    xc = _conv4(xe_scr, x_ref[...], xp_ref[...], xn_ref[...], cw_ref[...], first, last)
    xc = xc * _sigmoid(xc)
    xs = xc[:, :SSD_HEADS * p]
    bm = xc[:, SSD_HEADS * p:SSD_HEADS * p + SSD_GROUPS * n]
    cm = xc[:, SSD_HEADS * p + SSD_GROUPS * n:]

    dl = _softplus(sm_ref[...] + sb_ref[...])
    lfm = dl * arow_ref[...]
    row, col = _iota2(t)
    valid = (col >= row) if rev else (col <= row)
    bcol = _split_dot(_bf(valid.astype(F32)), lfm)
    tot_row = jnp.sum(lfm, axis=0, keepdims=True)
    b_t = bcol.T
    dl_t = dl.T

    xsb = _bf(xs)
    rep = SSD_HEADS // SSD_GROUPS
    outs = []
    for gi in range(SSD_GROUPS):
        bg = bm[:, gi * n:(gi + 1) * n]
        cgb = _bf(cm[:, gi * n:(gi + 1) * n])
        gmat = _dot_nt(cgb, _bf(bg))
        for hh in range(rep):
            hd = gi * rep + hh
            cc = SM_DT + SSD_HEADS * d + hd
            bc = bcol[:, cc:cc + 1]
            br = b_t[cc:cc + 1, :]
            dr = dl_t[cc:cc + 1, :]
            dcol = dl[:, cc:cc + 1]
            tot = tot_row[:, cc:cc + 1]
            att = gmat * jnp.exp(jnp.where(valid, bc - br, NEG)) * dr
            xh = xsb[:, hd * p:(hd + 1) * p]
            st = st_scr[hd]
            o = _dot(_bf(att), xh) + jnp.exp(bc) * _dot_nt(cgb, _bf(st))
            wk = dcol * jnp.exp(tot - bc)
            st_scr[hd] = jnp.exp(tot) * st + _dot_tn(xh, _bf(bg * wk))
            outs.append(o)
    o_all = jnp.concatenate(outs, axis=-1)

    if finish:
        y = (o_all + of_ref[...] + dsk_ref[...] * xs)
        z = z_ref[...]
        y = y * (z * _sigmoid(z))
        gw = SSD_HEADS * p // SSD_GROUPS
        for gi in range(SSD_GROUPS):
            sl = slice(gi * gw, (gi + 1) * gw)
            yg = y[:, sl]
            yg = yg * lax.rsqrt(jnp.mean(yg * yg, axis=-1, keepdims=True) + NORM_EPS)
            o_ref[:, sl] = (yg * ng_ref[:, sl]).astype(o_ref.dtype)
    else:
        o_ref[...] = o_all


def _ssd_pass(u, sbias, arow, conv_w, z_extra, o_f, *, rev, nb, nc, nl):
    t = T_CHUNK
    ntot = u.shape[0]
    finish = o_f is not None
    rb = functools.partial(_row_block, rev=rev, nc=nc, nl=nl, nb=nb)
    r8 = t // SUB
    nblk8 = ntot // SUB
    cb = C_SSD_XBC // 1024

    def prev_map(b, i):
        return (jnp.maximum(rb(b, i) * r8 - 1, 0), cb)

    def next_map(b, i):
        return (jnp.minimum((rb(b, i) + 1) * r8, nblk8 - 1), cb)

    const2 = lambda b, i: (0, 0)
    in_specs = [_col_spec(t, 1024, C_SSD_XBC, rb),
                pl.BlockSpec((SUB, 1024), prev_map), pl.BlockSpec((SUB, 1024), next_map),
                _col_spec(t, LANE, C_SMALL, rb),
                pl.BlockSpec((1, LANE), const2), pl.BlockSpec((1, LANE), const2),
                pl.BlockSpec((4, 1024), const2)]
    args = [u, u, u, u, sbias, arow, conv_w]
    if finish:
        dsk, ng = z_extra
        in_specs += [_col_spec(t, GW, C_SSD_Z, rb), _col_spec(t, GW, 0, rb),
                     pl.BlockSpec((1, GW), const2), pl.BlockSpec((1, GW), const2)]
        args += [u, o_f, dsk, ng]
    return pl.pallas_call(
        functools.partial(_ssd_kernel, rev=rev, finish=finish, nc=nc, nl=nl),
        grid=(nb, nc + nl),
        in_specs=in_specs,
        out_specs=_col_spec(t, GW, 0, rb),
        out_shape=jax.ShapeDtypeStruct((ntot, GW), BF16 if finish else F32),
        scratch_shapes=[pltpu.VMEM((SSD_HEADS, SSD_HEADDIM, SSD_STATE), F32),
                        pltpu.VMEM((t + 2 * SUB, 1024), F32)],
        compiler_params=_cparams(("parallel", "arbitrary")),
        name="ssd_bwd_finish" if rev else "ssd_fwd",
    )(*args)


def _lru_kernel(*refs, rev, finish, nc, nl):
    if finish:
        (x_ref, xp_ref, xn_ref, cw_ref, w2_ref, b2_ref, crow_ref,
         gate_ref, of_ref, o_ref, h_scr, xe_scr) = refs
    else:
        x_ref, xp_ref, xn_ref, cw_ref, w2_ref, b2_ref, crow_ref, o_ref, h_scr, xe_scr = refs
    t = x_ref.shape[0]
    w = x_ref.shape[1]
    ng = t // SUB
    i = pl.program_id(1)

    @pl.when(i == 0)
    def _():
        h_scr[...] = jnp.zeros_like(h_scr)

    is_ctx, c = _chunk_nat(i, rev, nc, nl)
    first = c == 0
    last = c == jnp.where(is_ctx, nc, nl) - 1
    xr = _conv4(xe_scr, x_ref[...], xp_ref[...], xn_ref[...], cw_ref[...], first, last)
    pre = _dot(_bf(xr), w2_ref[0]) + b2_ref[0]
    r = _sigmoid(pre[:, :w])
    ig = _sigmoid(pre[:, w:])
    log_a = crow_ref[0] * r
    a = jnp.exp(log_a)
    bb = jnp.sqrt(jnp.tanh(-log_a) * (1.0 + a * a)) * (ig * xr)

    a3 = a.reshape(ng, SUB, w)
    b3 = bb.reshape(ng, SUB, w)
    r8 = lax.broadcasted_iota(jnp.int32, (ng, SUB, w), 1)
    pos = (SUB - 1 - r8) if rev else r8
    step = 1
    while step < SUB:
        sh = (SUB - step) if rev else step
        a_s = pltpu.roll(a3, sh, 1)
        b_s = pltpu.roll(b3, sh, 1)
        ok = pos >= step
        b3 = jnp.where(ok, a3 * b_s + b3, b3)
        a3 = jnp.where(ok, a3 * a_s, a3)
        step *= 2
    h = h_scr[...]
    groups = [None] * ng
    for g in (range(ng - 1, -1, -1) if rev else range(ng)):
        hg = b3[g] + a3[g] * h
        groups[g] = hg
        h = hg[0:1, :] if rev else hg[SUB - 1:SUB, :]
    h_scr[...] = h
    hs = jnp.concatenate(groups, axis=0)

    if finish:
        g = gate_ref[...]
        gelu = 0.5 * g * (1.0 + jnp.tanh(math.sqrt(2.0 / math.pi) * (g + 0.044715 * (g * g * g))))
        o_ref[...] = ((hs + of_ref[...]) * gelu).astype(o_ref.dtype)
    else:
        o_ref[...] = hs


def _lru_pass(ul, consts, o_f, *, rev, nb, nc, nl):
    t = T_CHUNK
    ntot = ul.shape[0]
    finish = o_f is not None
    d = 1 if rev else 0
    rb = functools.partial(_row_block, rev=rev, nc=nc, nl=nl, nb=nb)
    r8 = t // SUB
    nblk8 = ntot // SUB
    cw, w2, b2, crow = consts
    const2 = lambda b, i: (0, 0)
    dsel = lambda b, i: (d, 0, 0)
    in_specs = [_col_spec(t, GW, GW, rb),
                pl.BlockSpec((SUB, GW), lambda b, i: (jnp.maximum(rb(b, i) * r8 - 1, 0), 1)),
                pl.BlockSpec((SUB, GW), lambda b, i: (jnp.minimum((rb(b, i) + 1) * r8, nblk8 - 1), 1)),
                pl.BlockSpec((4, GW), const2),
                pl.BlockSpec((1, GW, 2 * GW), dsel), pl.BlockSpec((1, 1, 2 * GW), dsel),
                pl.BlockSpec((1, 1, GW), dsel)]
    args = [ul, ul, ul, cw, w2, b2, crow]
    if finish:
        in_specs += [_col_spec(t, GW, 0, rb), _col_spec(t, GW, 0, rb)]
        args += [ul, o_f]
    return pl.pallas_call(
        functools.partial(_lru_kernel, rev=rev, finish=finish, nc=nc, nl=nl),
        grid=(nb, nc + nl),
        in_specs=in_specs,
        out_specs=_col_spec(t, GW, 0, rb),
        out_shape=jax.ShapeDtypeStruct((ntot, GW), BF16 if finish else F32),
        scratch_shapes=[pltpu.VMEM((1, GW), F32), pltpu.VMEM((t + 2 * SUB, GW), F32)],
        compiler_params=_cparams(("parallel", "arbitrary")),
        name="rglru_bwd_finish" if rev else "rglru_fwd",
    )(*args)


def _lru_mixer(u, consts, *, nb, seq, nc, nl):
    n_lat = nb * seq
    rows = seq // GRID_W
    assert rows == T_CHUNK, "one scan chunk must be one latent grid column"
    ul = u[:, C_LRU_GATE:C_LRU_GATE + 2 * GW]
    lat = ul[:n_lat].reshape(nb, rows, GRID_W, 2 * GW).transpose(0, 2, 1, 3).reshape(n_lat, 2 * GW)
    ul = jnp.concatenate([lat, ul[n_lat:]], axis=0)
    o = _lru_pass(ul, consts, None, rev=False, nb=nb, nc=nc, nl=nl)
    y = _lru_pass(ul, consts, o, rev=True, nb=nb, nc=nc, nl=nl)
    lat = y[:n_lat].reshape(nb, GRID_W, rows, GW).transpose(0, 2, 1, 3).reshape(n_lat, GW)
    return jnp.concatenate([lat, y[n_lat:]], axis=0)


def _outproj_kernel(yh_ref, ys_ref, yl_ref, ym_ref, w_ref, x_ref, mod_ref, g_ref, wr_ref,
                    xn_ref, h_ref, lg_ref):
    acc = _dot(yh_ref[...], w_ref[0])
    acc = acc + _dot(ys_ref[...], w_ref[1])
    acc = acc + _dot(yl_ref[...], w_ref[2])
    acc = acc + _dot(ym_ref[...], w_ref[3])
    xn = x_ref[...] + mod_ref[0, 2:3, :] * acc
    xn_ref[...] = xn
    h = _rms_mod(xn, g_ref[...], mod_ref[0, 3:4, :], mod_ref[0, 4:5, :])
    tm, d = h.shape
    nch = d // LANE
    for j in range(nch):
        h_ref[pl.ds(j, tm, stride=nch), :] = h[:, j * LANE:(j + 1) * LANE]
    h1 = _bf(h)
    h2 = _bf(h - h1.astype(F32))
    p = _dot(h1, wr_ref[...])
    lg_ref[...] = p[:, :LANE] + p[:, LANE:] + _dot(h2, wr_ref[:, :LANE])


def _out_proj(ys, w_out4, x, mod, g2, w_router, *, n_lat, seq, b):
    ntot, d = x.shape
    tm = 512
    row = lambda i: (i, 0)
    mrow = _mod_row_map(tm, n_lat, seq, b)
    in_specs = [pl.BlockSpec((tm, GW), row)] * 4 + [
        pl.BlockSpec((4, GW, d), lambda i: (0, 0, 0)),
        pl.BlockSpec((tm, d), row), pl.BlockSpec((1, 6, d), mrow),
        pl.BlockSpec((1, d), lambda i: (0, 0)), pl.BlockSpec((d, 2 * LANE), lambda i: (0, 0))]
    return pl.pallas_call(
        _outproj_kernel,
        grid=(ntot // tm,),
        in_specs=in_specs,
        out_specs=[pl.BlockSpec((tm, d), row), pl.BlockSpec((tm * (d // LANE), LANE), row),
                   pl.BlockSpec((tm, LANE), row)],
        out_shape=[jax.ShapeDtypeStruct((ntot, d), F32), jax.ShapeDtypeStruct((ntot * (d // LANE), LANE), F32),
                   jax.ShapeDtypeStruct((ntot, LANE), F32)],
        compiler_params=_cparams(("parallel",)),
        name="out_proj_norm2_router",
    )(*ys, w_out4, x, mod, g2, w_router)


def _row_copy(h_hbm, xbuf, sem, tok, slot, r):
    nch = xbuf.shape[1] // MOE_BLOCK
    return pltpu.make_async_copy(h_hbm.at[pl.ds(tok * nch, nch), :], xbuf.at[slot, pl.ds(r * nch, nch), :],
                                 sem.at[slot])


def _moe_kernel(be_ref, tok_ref, h_hbm, wg_ref, wu_ref, wd_ref, o_ref, xbuf, sem, wgu_scr, wd_scr):
    i = pl.program_id(0)
    nblk = pl.num_programs(0)
    de = wg_ref.shape[-1]
    slot = i % 2

    def start_gather(blk, s):
        for r in range(MOE_BLOCK):
            _row_copy(h_hbm, xbuf, sem, tok_ref[blk * MOE_BLOCK + r], s, r).start()

    @pl.when(i == 0)
    def _():
        start_gather(0, 0)

    @pl.when(jnp.logical_or(i == 0, be_ref[i] != be_ref[jnp.maximum(i - 1, 0)]))
    def _():
        wgu_scr[:, :de] = _bf(wg_ref[0])
        wgu_scr[:, de:] = _bf(wu_ref[0])
        wd_scr[...] = _bf(wd_ref[0])

    for r in range(MOE_BLOCK):
        _row_copy(h_hbm, xbuf, sem, 0, slot, r).wait()
    start_gather(jnp.where(i + 1 < nblk, i + 1, 0), 1 - slot)

    nch = xbuf.shape[1] // MOE_BLOCK
    xb = jnp.concatenate([_bf(xbuf[slot, pl.ds(j, MOE_BLOCK, stride=nch), :]) for j in range(nch)], axis=-1)
    gu = _dot(xb, wgu_scr[...])
    g = gu[:, :de]
    hid = (g * _sigmoid(g)) * gu[:, de:]
    o_ref[...] = _bf(_dot(_bf(hid), wd_scr[...]))

    @pl.when(i == nblk - 1)
    def _():
        for r in range(MOE_BLOCK):
            _row_copy(h_hbm, xbuf, sem, 0, 1 - slot, r).wait()


def _moe_blocks(blk_exp, slot_tok, h, wg, wu, wd):
    nslot = slot_tok.shape[0]
    d = wg.shape[1]
    nch = d // LANE
    de = wg.shape[-1]
    nblk = nslot // MOE_BLOCK
    return pl.pallas_call(
        _moe_kernel,
        grid_spec=pltpu.PrefetchScalarGridSpec(
            num_scalar_prefetch=2,
            grid=(nblk,),
            in_specs=[pl.BlockSpec(memory_space=pl.ANY),
                      pl.BlockSpec((1, d, de), lambda i, be, tk: (be[i], 0, 0)),
                      pl.BlockSpec((1, d, de), lambda i, be, tk: (be[i], 0, 0)),
                      pl.BlockSpec((1, de, d), lambda i, be, tk: (be[i], 0, 0))],
            out_specs=pl.BlockSpec((MOE_BLOCK, d), lambda i, be, tk: (i, 0)),
            scratch_shapes=[pltpu.VMEM((2, MOE_BLOCK * nch, LANE), F32), pltpu.SemaphoreType.DMA((2,)),
                            pltpu.VMEM((d, 2 * de), BF16), pltpu.VMEM((de, d), BF16)]),
        out_shape=jax.ShapeDtypeStruct((nslot, d), BF16),
        compiler_params=_cparams(("arbitrary",)),
        name="moe_expert_blocks",
    )(blk_exp, slot_tok, h, wg, wu, wd)


def _route(logits):
    n_tok = logits.shape[0]
    n_exp = N_EXPERT_GROUPS * EXPERTS_PER_GROUP
    g_logits = logits[:, :N_EXPERT_GROUPS]
    g_prob = jax.nn.softmax(g_logits, axis=-1)
    g_sel = jnp.argmax(g_logits, axis=-1)
    g_hot = g_sel[:, None] == jnp.arange(N_EXPERT_GROUPS)[None, :]
    e_logits = logits[:, N_EXPERT_GROUPS:N_EXPERT_GROUPS + n_exp].reshape(
        n_tok, N_EXPERT_GROUPS, EXPERTS_PER_GROUP)
    within = jnp.sum(jnp.where(g_hot[:, :, None], e_logits, 0.0), axis=1)
    top_v, top_i = lax.top_k(within, TOP_K)
    wts = jax.nn.softmax(top_v, axis=-1) * jnp.sum(jnp.where(g_hot, g_prob, 0.0), axis=1, keepdims=True)
    eid = (g_sel[:, None] * EXPERTS_PER_GROUP + top_i).astype(jnp.int32)

    n_assign = n_tok * TOP_K
    flat_e = eid.reshape(-1)
    onehot = (flat_e[:, None] == jnp.arange(n_exp, dtype=jnp.int32)[None, :]).astype(jnp.int32)
    csum = jnp.cumsum(onehot, axis=0)
    counts = csum[-1]
    padded = (counts + MOE_BLOCK - 1) // MOE_BLOCK * MOE_BLOCK
    pend = jnp.cumsum(padded)
    pstart = pend - padded
    dest = jnp.sum(onehot * (csum - 1 + pstart[None, :]), axis=1)
    n_blk = -(-n_assign // MOE_BLOCK) + n_exp
    tok = jnp.arange(n_assign, dtype=jnp.int32) // TOP_K
    slot_tok = jnp.zeros((n_blk * MOE_BLOCK,), jnp.int32).at[dest].set(tok)
    blk_start = jnp.arange(n_blk, dtype=jnp.int32) * MOE_BLOCK
    blk_exp = jnp.minimum(jnp.sum(blk_start[:, None] >= pend[None, :], axis=1), n_exp - 1).astype(jnp.int32)
    return wts, dest, slot_tok, blk_exp


def _moe(h2, logits, layer, wg, wu, wd):
    n_tok = logits.shape[0]
    n_exp = wg.shape[1]
    wts, dest, slot_tok, blk_exp = _route(logits)
    flat = lambda w: w.reshape((w.shape[0] * n_exp,) + w.shape[2:])
    yb = _moe_blocks(blk_exp + layer * n_exp, slot_tok, h2, flat(wg), flat(wu), flat(wd))
    dest = dest.reshape(n_tok, TOP_K)
    take = functools.partial(jnp.take, axis=0, mode="clip")
    return take(yb, dest[:, 0]), take(yb, dest[:, 1]), wts


def _final_kernel(x_ref, y0_ref, y1_ref, wt_ref, pmod_ref, g_ref, o_ref):
    y = wt_ref[:, 0:1] * y0_ref[...].astype(F32) + wt_ref[:, 1:2] * y1_ref[...].astype(F32)
    x = x_ref[...] + pmod_ref[0, 5:6, :] * y
    o_ref[...] = x * lax.rsqrt(jnp.mean(x * x, axis=-1, keepdims=True) + NORM_EPS) * g_ref[...]


def _final(x, ymoe, pmod, g, *, n_lat, seq, b):
    d = x.shape[1]
    tm = 512
    row = lambda i: (i, 0)
    y0, y1, wts = ymoe
    return pl.pallas_call(
        _final_kernel,
        grid=(n_lat // tm,),
        in_specs=[pl.BlockSpec((tm, d), row), pl.BlockSpec((tm, d), row), pl.BlockSpec((tm, d), row),
                  pl.BlockSpec((tm, TOP_K), row),
                  pl.BlockSpec((1, 6, d), _mod_row_map(tm, n_lat, seq, b)),
                  pl.BlockSpec((1, d), lambda i: (0, 0))],
        out_specs=pl.BlockSpec((tm, d), row),
        out_shape=jax.ShapeDtypeStruct((n_lat, d), F32),
        compiler_params=_cparams(("parallel",)),
        name="final_residual_norm",
    )(x, y0, y1, wts, pmod, g)


def _arrange_w_in(w_in_l):
    d = w_in_l.shape[0]
    hg = w_in_l[:, :5 * GW]
    o = 5 * GW
    ssd_z = w_in_l[:, o:o + GW]
    ssd_xbc = w_in_l[:, o + GW:o + GW + 1024]
    ssd_dt = w_in_l[:, o + GW + 1024:o + GW + 1024 + 2 * SSD_HEADS]
    o += GW + 1024 + 2 * SSD_HEADS
    lru = w_in_l[:, o:o + 2 * GW]
    o += 2 * GW
    ml_main = w_in_l[:, o:o + 4 * GW]
    ml_gates = w_in_l[:, o + 4 * GW:o + 4 * GW + 4 * ML_HEADS]
    small = jnp.concatenate([ssd_dt, ml_gates], axis=1)
    pad = jnp.zeros((d, NCOLS - C_SMALL - small.shape[1]), w_in_l.dtype)
    return jnp.concatenate([hg, ssd_z, ssd_xbc, lru, ml_main, small, pad], axis=1)


def _blockdiag_dense(w):
    n, bw, _ = w.shape
    eye = jnp.eye(n, dtype=w.dtype)
    return (eye[:, None, :, None] * w[:, :, None, :]).reshape(n * bw, n * bw)


def _pad_lanes(v):
    return jnp.concatenate([v, jnp.zeros((LANE - v.shape[0],), v.dtype)])[None, :]


def kernel(x, c, ctx, c_ctx, ada_w, ada_b, norm1_g, norm2_g, w_in, w_out, hg_lb_logits, ssd_conv_w, ssd_a_log, ssd_dt_bias, ssd_d, ssd_norm_g, lru_conv_w, lru_wa, lru_ba, lru_wx, lru_bx, lru_lambda, ml_b_i, ml_b_f, router_group_w, router_expert_w, moe_w_gate, moe_w_up, moe_w_down, final_norm_g):
    nb, seq, d = x.shape
    nctx = ctx.shape[1]
    depth = ada_w.shape[0]
    n_lat = nb * seq
    t = T_CHUNK
    nc, nl = nctx // t, seq // t
    kw = dict(nb=nb, nc=nc, nl=nl)
    lay = dict(n_lat=n_lat, seq=seq, b=nb)

    p = jax.nn.softmax(hg_lb_logits.astype(F32), axis=0)
    lb_all = (jnp.cumsum(p, axis=0) - p[:1])[:, :, None, :]

    c_all = jnp.concatenate([c, c_ctx[None, :], jnp.zeros((SUB - nb - 1, d), F32)], axis=0)
    mods = _modulation(c_all, ada_w, ada_b).reshape(depth, SUB, 6, d)

    xall = jnp.concatenate([x.reshape(n_lat, d), ctx.reshape(nb * nctx, d)], axis=0)
    y_moe = None
    tn = 512
    for l in range(depth):
        w_l = _bf(_arrange_w_in(w_in[l])).reshape(d, NCOLS // tn, tn).transpose(1, 0, 2)
        xall, h1 = _res_norm(xall, y_moe, mods[l - 1] if l else None, norm1_g[l][None, :], mods[l], **lay)
        u = _in_proj(h1, w_l)

        o = _hg_pass(u, lb_all[l], None, rev=False, **kw)
        y_hg = _hg_pass(u, lb_all[l], o, rev=True, **kw)

        sbias = _pad_lanes(jnp.concatenate([ssd_dt_bias[l, 0], ssd_dt_bias[l, 1],
                                            ml_b_i[l, 0], ml_b_i[l, 1], ml_b_f[l, 0], ml_b_f[l, 1]]))
        arow = _pad_lanes(jnp.concatenate([-jnp.exp(ssd_a_log[l, 0]), -jnp.exp(ssd_a_log[l, 1])]))
        dsk = jnp.repeat(ssd_d[l], SSD_HEADDIM)[None, :]
        o = _ssd_pass(u, sbias, arow, ssd_conv_w[l], None, None, rev=False, **kw)
        y_ssd = _ssd_pass(u, sbias, arow, ssd_conv_w[l], (dsk, ssd_norm_g[l][None, :]), o, rev=True, **kw)

        w2 = _bf(jnp.stack([jnp.concatenate([_blockdiag_dense(lru_wa[l, dd]), _blockdiag_dense(lru_wx[l, dd])],
                                            axis=1) for dd in range(2)]))
        b2 = jnp.stack([jnp.concatenate([lru_ba[l, dd], lru_bx[l, dd]])[None, :] for dd in range(2)])
        crow = (-LRU_C * jax.nn.softplus(-lru_lambda[l]))[:, None, :]
        y_lru = _lru_mixer(u, (lru_conv_w[l], w2, b2, crow), nb=nb, seq=seq, nc=nc, nl=nl)

        o = _ml_pass(u, sbias, None, rev=False, **kw)
        y_ml = _ml_pass(u, sbias, o, rev=True, **kw)

        w_router = jnp.concatenate(
            [router_group_w[l], router_expert_w[l],
             jnp.zeros((d, LANE - N_EXPERT_GROUPS * (1 + EXPERTS_PER_GROUP)), F32)], axis=1)
        wr1 = _bf(w_router)
        wr12 = jnp.concatenate([wr1, _bf(w_router - wr1.astype(F32))], axis=1)
        xall, h2, logits = _out_proj((y_hg, y_ssd, y_lru, y_ml), _bf(w_out[l]).reshape(4, GW, d), xall,
                                     mods[l], norm2_g[l][None, :], wr12, **lay)
        y_moe = _moe(h2, logits, l, moe_w_gate, moe_w_up, moe_w_down)

    out = _final(xall, y_moe, mods[depth - 1], final_norm_g[None, :], **lay)
    return out.reshape(nb, seq, d)
```

```python
import functools
import math

import jax
import jax.numpy as jnp
from jax import lax
from jax.experimental import pallas as pl
from jax.experimental.pallas import tpu as pltpu

F32 = jnp.float32
BF16 = jnp.bfloat16

GRID_W = 64
HG_HEADS, HG_DK = 4, 128
SSD_HEADS, SSD_HEADDIM, SSD_GROUPS, SSD_STATE = 8, 64, 2, 128
LRU_C = 8.0
ML_HEADS, ML_DH = 4, 128
N_EXPERT_GROUPS, EXPERTS_PER_GROUP, TOP_K = 4, 8, 2
MOE_BLOCK = 256
NORM_EPS = 1e-6
NEG = -1e30

GW = 512
C_HG_Q, C_HG_I, C_HG_G, C_HG_ZF, C_HG_ZB = 0, 512, 1024, 1536, 2048
C_SSD_Z, C_SSD_XBC = 2560, 3072
C_LRU_GATE, C_LRU_XR = 4096, 4608
C_ML_Q, C_ML_K, C_ML_V, C_ML_O = 5120, 5632, 6144, 6656
C_SMALL = 7168
SM_DT, SM_IG, SM_FG = 0, 16, 24
NCOLS = 7680

LANE = 128
SUB = 8
VMEM_LIMIT = 48 * 1024 * 1024

T_CHUNK = 128
T_BLOCK = 2 * T_CHUNK


def _cparams(sem):
    return pltpu.CompilerParams(dimension_semantics=sem, vmem_limit_bytes=VMEM_LIMIT)


def _dot(a, b):
    return jnp.dot(a, b, preferred_element_type=F32)


def _dot_nt(a, b):
    return lax.dot_general(a, b, (((1,), (1,)), ((), ())), preferred_element_type=F32)


def _dot_tn(a, b):
    return lax.dot_general(a, b, (((0,), (0,)), ((), ())), preferred_element_type=F32)


def _bf(x):
    return x.astype(BF16)


def _sigmoid(x):
    return 0.5 * jnp.tanh(0.5 * x) + 0.5


def _softplus(x):
    return jnp.maximum(x, 0.0) + jnp.log1p(jnp.exp(-jnp.abs(x)))


def _split_dot(mask_bf, x):
    hi = _bf(x)
    r1 = x - hi.astype(F32)
    mid = _bf(r1)
    lo = _bf(r1 - mid.astype(F32))
    return _dot(mask_bf, hi) + _dot(mask_bf, mid) + _dot(mask_bf, lo)


def _rollb(x, n, rev):
    if n == 0:
        return x
    t = x.shape[0]
    return pltpu.roll(x, (t - n) if rev else n, 0)


def _iota2(t):
    return (lax.broadcasted_iota(jnp.int32, (t, t), 0), lax.broadcasted_iota(jnp.int32, (t, t), 1))


def _mod_kernel(c_ref, w_ref, b_ref, o_ref):
    c = c_ref[...]
    s = c * _sigmoid(c)
    o_ref[0] = jnp.dot(s, w_ref[0], precision=lax.Precision.HIGHEST,
                       preferred_element_type=F32) + b_ref[0]


def _modulation(c_all, ada_w, ada_b):
    depth, d, n6 = ada_w.shape
    tn = 1024
    return pl.pallas_call(
        _mod_kernel,
        grid=(depth, n6 // tn),
        in_specs=[pl.BlockSpec((SUB, d), lambda l, j: (0, 0)),
                  pl.BlockSpec((1, d, tn), lambda l, j: (l, 0, j)),
                  pl.BlockSpec((1, 1, tn), lambda l, j: (l, 0, j))],
        out_specs=pl.BlockSpec((1, SUB, tn), lambda l, j: (l, 0, j)),
        out_shape=jax.ShapeDtypeStruct((depth, SUB, n6), F32),
        compiler_params=_cparams(("parallel", "parallel")),
        name="adaln_modulation",
    )(c_all, ada_w, ada_b.reshape(depth, 1, n6))


def _rms_mod(x, g, shift, scale):
    y = x * lax.rsqrt(jnp.mean(x * x, axis=-1, keepdims=True) + NORM_EPS) * g
    return y * (1.0 + scale) + shift


def _mod_row_map(tm, n_lat, seq, b):
    def f(i, *_):
        return (jnp.where(i * tm < n_lat, (i * tm) // seq, b), 0, 0)
    return f


def _resnorm_kernel(*refs, residual):
    if residual:
        x_ref, y0_ref, y1_ref, wt_ref, pmod_ref, g_ref, mod_ref, xn_ref, h_ref = refs
        y = (wt_ref[:, 0:1] * y0_ref[...].astype(F32) + wt_ref[:, 1:2] * y1_ref[...].astype(F32))
        x = x_ref[...] + pmod_ref[0, 5:6, :] * y
        xn_ref[...] = x
    else:
        x_ref, g_ref, mod_ref, h_ref = refs
        x = x_ref[...]
    h_ref[...] = _bf(_rms_mod(x, g_ref[...], mod_ref[0, 0:1, :], mod_ref[0, 1:2, :]))


def _res_norm(x, ymoe, pmod, g1, mod, *, n_lat, seq, b):
    ntot, d = x.shape
    tm = 512
    residual = ymoe is not None
    row = lambda i: (i, 0)
    mrow = _mod_row_map(tm, n_lat, seq, b)
    in_specs = [pl.BlockSpec((tm, d), row)]
    args = [x]
    if residual:
        y0, y1, wts = ymoe
        in_specs += [pl.BlockSpec((tm, d), row), pl.BlockSpec((tm, d), row),
                     pl.BlockSpec((tm, TOP_K), row), pl.BlockSpec((1, 6, d), mrow)]
        args += [y0, y1, wts, pmod]
    in_specs += [pl.BlockSpec((1, d), lambda i: (0, 0)), pl.BlockSpec((1, 6, d), mrow)]
    args += [g1, mod]
    out_specs = [pl.BlockSpec((tm, d), row)]
    out_shape = [jax.ShapeDtypeStruct((ntot, d), BF16)]
    if residual:
        out_specs = [pl.BlockSpec((tm, d), row)] + out_specs
        out_shape = [jax.ShapeDtypeStruct((ntot, d), F32)] + out_shape
    res = pl.pallas_call(
        functools.partial(_resnorm_kernel, residual=residual),
        grid=(ntot // tm,),
        in_specs=in_specs, out_specs=out_specs, out_shape=out_shape,
        compiler_params=_cparams(("parallel",)),
        name="residual_norm1",
    )(*args)
    return (res[0], res[1]) if residual else (x, res[0])


def _inproj_kernel(h_ref, w_ref, u_ref):
    u_ref[...] = _dot(h_ref[...], w_ref[0])


def _in_proj(h, w_tiles):
    ntot, d = h.shape
    nt, _, tn = w_tiles.shape
    tm = 1536
    assert ntot % tm == 0
    return pl.pallas_call(
        _inproj_kernel,
        grid=(ntot // tm, nt),
        in_specs=[pl.BlockSpec((tm, d), lambda i, j: (i, 0)),
                  pl.BlockSpec((1, d, tn), lambda i, j: (j, 0, 0))],
        out_specs=pl.BlockSpec((tm, tn), lambda i, j: (i, j)),
        out_shape=jax.ShapeDtypeStruct((ntot, nt * tn), F32),
        compiler_params=_cparams(("parallel", "arbitrary")),
        name="in_proj",
    )(h, w_tiles)


def _chunk_nat(i, rev, nc, nl):
    is_ctx = i < nc
    if rev:
        c = jnp.where(is_ctx, nc - 1 - i, nl - 1 - (i - nc))
    else:
        c = jnp.where(is_ctx, i, i - nc)
    return is_ctx, c


def _row_block(b, i, *, rev, nc, nl, nb):
    is_ctx, c = _chunk_nat(i, rev, nc, nl)
    return jnp.where(is_ctx, nb * nl + b * nc + c, b * nl + c)


def _col_spec(t, width, col, rb):
    cb = col // width
    return pl.BlockSpec((t, width), lambda b, i: (rb(b, i), cb))


def _hg_kernel(*refs, rev, finish):
    if finish:
        q_ref, v_ref, z_ref, lb_ref, g_ref, of_ref, o_ref, st_scr = refs
    else:
        q_ref, v_ref, z_ref, lb_ref, o_ref, st_scr = refs
    t = T_CHUNK
    dk = HG_DK
    w = q_ref.shape[1]

    @pl.when(pl.program_id(1) == 0)
    def _():
        st_scr[...] = jnp.zeros_like(st_scr)

    lb = lb_ref[0]
    row, col = _iota2(t)
    valid = (col >= row) if rev else (col <= row)
    lvl = jnp.where(valid, 31 - lax.clz(row ^ col), -2)
    valid_bf = _bf(valid.astype(F32))
    tpos = lax.broadcasted_iota(jnp.int32, (t, w), 0)
    r8 = lax.broadcasted_iota(jnp.int32, (t // SUB, SUB, w), 1)
    for rs in _sub_chunks(q_ref.shape[0], rev):
        _hg_chunk(rs, rev, finish, refs, lb, valid_bf, lvl, tpos, r8)


def _sub_chunks(block_rows, rev):
    n = block_rows // T_CHUNK
    order = range(n - 1, -1, -1) if rev else range(n)
    return [slice(i * T_CHUNK, (i + 1) * T_CHUNK) for i in order]


def _hg_chunk(rs, rev, finish, refs, lb, valid_bf, lvl, tpos, r8):
    if finish:
        q_ref, v_ref, z_ref, lb_ref, g_ref, of_ref, o_ref, st_scr = refs
    else:
        q_ref, v_ref, z_ref, lb_ref, o_ref, st_scr = refs
    t = T_CHUNK
    dk = HG_DK
    w = q_ref.shape[1]
    sig = _sigmoid(z_ref[rs, :])
    f = lb + (1.0 - lb) * sig
    lf = jnp.log(f)
    k = (1.0 - lb) * (1.0 - sig)
    q = q_ref[rs, :]
    v = v_ref[rs, :]

    b = _split_dot(valid_bf, lf)
    total = jnp.sum(lf, axis=0, keepdims=True)

    b8 = b.reshape(t // SUB, SUB, w)
    ops = [(-1, _bf(q), _bf(k))]
    h = 1
    while h < t:
        blk = 2 * h
        lg = int(math.log2(h))
        is_q = ((tpos & h) == 0) if rev else ((tpos & h) != 0)
        if h == 1:
            e = f
        else:
            off = h if rev else h - 1
            if blk <= SUB:
                ref = b8[:, off:off + 1, :]
                for sb in range(1, SUB // blk):
                    ref = jnp.where(r8 >= sb * blk, b8[:, sb * blk + off:sb * blk + off + 1, :], ref)
                ref = jnp.broadcast_to(ref, b8.shape).reshape(t, w)
            else:
                b3 = b.reshape(t // blk, blk, w)
                ref = jnp.broadcast_to(b3[:, off:off + 1, :], b3.shape).reshape(t, w)
            e = jnp.exp(-jnp.abs(b - ref))
        if h == 1:
            m = _bf(jnp.where(is_q, q * e, k))
        else:
            m = _bf(jnp.where(is_q, q, k) * e)
        ops.append((lg, m, m))
        h = blk

    qb = _bf(q * jnp.exp(b))
    kw = _bf(k * jnp.exp(total - b))
    vb = _bf(v)
    dec_tot = jnp.exp(total)

    masks = [lvl == lg for lg, _, _ in ops]
    outs = []
    for hd in range(HG_HEADS):
        sl = slice(hd * dk, (hd + 1) * dk)
        a = jnp.zeros((t, t), F32)
        for msk, (lg, mq, mk) in zip(masks, ops):
            a = jnp.where(msk, _dot_nt(mq[:, sl], mk[:, sl]), a)
        st = st_scr[hd]
        o = _dot(_bf(a), vb[:, sl]) + _dot_nt(qb[:, sl], _bf(st))
        st_scr[hd] = st * dec_tot[:, sl] + _dot_tn(vb[:, sl], kw[:, sl])
        outs.append(o)

    if finish:
        for hd in range(HG_HEADS):
            sl = slice(hd * dk, (hd + 1) * dk)
            tot = outs[hd] + of_ref[rs, sl]
            y = tot * lax.rsqrt(jnp.mean(tot * tot, axis=-1, keepdims=True) + NORM_EPS)
            g = g_ref[rs, sl]
            o_ref[rs, sl] = (y * (g * _sigmoid(g))).astype(o_ref.dtype)
    else:
        for hd in range(HG_HEADS):
            o_ref[rs, hd * dk:(hd + 1) * dk] = outs[hd]


def _hg_pass(u, lb, o_f, *, rev, nb, nc, nl):
    t = T_BLOCK
    ntot = u.shape[0]
    finish = o_f is not None
    rb = functools.partial(_row_block, rev=rev, nc=nc, nl=nl, nb=nb)
    d = 1 if rev else 0
    in_specs = [_col_spec(t, GW, C_HG_Q, rb), _col_spec(t, GW, C_HG_I, rb),
                _col_spec(t, GW, C_HG_ZB if rev else C_HG_ZF, rb),
                pl.BlockSpec((1, 1, GW), lambda b, i: (d, 0, 0))]
    args = [u, u, u, lb]
    if finish:
        in_specs += [_col_spec(t, GW, C_HG_G, rb), _col_spec(t, GW, 0, rb)]
        args += [u, o_f]
    return pl.pallas_call(
        functools.partial(_hg_kernel, rev=rev, finish=finish),
        grid=(nb, nc + nl),
        in_specs=in_specs,
        out_specs=_col_spec(t, GW, 0, rb),
        out_shape=jax.ShapeDtypeStruct((ntot, GW), BF16 if finish else F32),
        scratch_shapes=[pltpu.VMEM((HG_HEADS, HG_DK, HG_DK), F32)],
        compiler_params=_cparams(("parallel", "arbitrary")),
        name="hgrn2_bwd_finish" if rev else "hgrn2_fwd",
    )(*args)


def _ml_kernel(*refs, rev, finish):
    ct_scr, m_scr = refs[-2:]
    t = T_CHUNK

    @pl.when(pl.program_id(1) == 0)
    def _():
        ct_scr[...] = jnp.zeros_like(ct_scr)
        m_scr[...] = jnp.full(m_scr.shape, NEG, F32)

    row, col = _iota2(t)
    valid = (col >= row) if rev else (col <= row)
    valid_bf = _bf(valid.astype(F32))
    for rs in _sub_chunks(refs[0].shape[0], rev):
        _ml_chunk(rs, rev, finish, refs, valid, valid_bf)


def _ml_chunk(rs, rev, finish, refs, valid, valid_bf):
    if finish:
        q_ref, k_ref, v_ref, sm_ref, sb_ref, og_ref, of_ref, o_ref, ct_scr, m_scr = refs
    else:
        q_ref, k_ref, v_ref, sm_ref, sb_ref, o_ref, ct_scr, m_scr = refs
    t = T_CHUNK
    dh = ML_DH
    d = 1 if rev else 0

    sm = sm_ref[rs, :] + sb_ref[...]
    lsg = jnp.minimum(sm, 0.0) - jnp.log1p(jnp.exp(-jnp.abs(sm)))
    bcol = _split_dot(valid_bf, lsg)
    tot_row = jnp.sum(lsg, axis=0, keepdims=True)
    b_t = bcol.T
    sm_t = sm.T

    ks = k_ref[rs, :] * (dh ** -0.5)
    qb, kb, vb = _bf(q_ref[rs, :]), _bf(ks), _bf(v_ref[rs, :])
    ones = jnp.ones((t, dh), BF16)

    for hd in range(ML_HEADS):
        sl = slice(hd * dh, (hd + 1) * dh)
        ci = SM_IG + ML_HEADS * d + hd
        cf = SM_FG + ML_HEADS * d + hd
        bc = bcol[:, cf:cf + 1]
        br = b_t[cf:cf + 1, :]
        lir = sm_t[ci:ci + 1, :]
        lic = sm[:, ci:ci + 1]
        m_prev = m_scr[hd:hd + 1, 0:1]
        dmat = jnp.where(valid, bc - br + lir, NEG)
        g = bc + m_prev
        mt = jnp.maximum(g, jnp.max(dmat, axis=-1, keepdims=True))
        w_intra = jnp.exp(dmat - mt)
        w_inter = jnp.exp(g - mt)
        s = _dot_nt(qb[:, sl], kb[:, sl]) * w_intra
        vaug = jnp.concatenate([vb[:, sl], ones], axis=-1)
        ct = ct_scr[hd]
        nd = _dot(_bf(s), vaug) + w_inter * _dot_nt(qb[:, sl], _bf(ct))
        hout = nd[:, :dh] / jnp.maximum(jnp.abs(nd[:, dh:]), jnp.exp(-mt))
        tot = tot_row[:, cf:cf + 1]
        dl = tot - bc + lic
        m_new = jnp.maximum(tot + m_prev, jnp.max(dl, axis=0, keepdims=True))
        ws = jnp.exp(dl - m_new)
        wc = jnp.exp(tot + m_prev - m_new)
        ct_scr[hd] = wc * ct + _dot_tn(vaug, _bf(ks[:, sl] * ws))
        m_scr[hd:hd + 1, :] = jnp.broadcast_to(m_new, (1, m_scr.shape[1]))
        if finish:
            tot_o = hout + of_ref[rs, sl]
            y = tot_o * lax.rsqrt(jnp.mean(tot_o * tot_o, axis=-1, keepdims=True) + NORM_EPS)
            o_ref[rs, sl] = (y * _sigmoid(og_ref[rs, sl])).astype(o_ref.dtype)
        else:
            o_ref[rs, sl] = hout


def _ml_pass(u, sbias, o_f, *, rev, nb, nc, nl):
    t = T_BLOCK
    ntot = u.shape[0]
    finish = o_f is not None
    rb = functools.partial(_row_block, rev=rev, nc=nc, nl=nl, nb=nb)
    in_specs = [_col_spec(t, GW, C_ML_Q, rb), _col_spec(t, GW, C_ML_K, rb),
                _col_spec(t, GW, C_ML_V, rb), _col_spec(t, LANE, C_SMALL, rb),
                pl.BlockSpec((1, LANE), lambda b, i: (0, 0))]
    args = [u, u, u, u, sbias]
    if finish:
        in_specs += [_col_spec(t, GW, C_ML_O, rb), _col_spec(t, GW, 0, rb)]
        args += [u, o_f]
    return pl.pallas_call(
        functools.partial(_ml_kernel, rev=rev, finish=finish),
        grid=(nb, nc + nl),
        in_specs=in_specs,
        out_specs=_col_spec(t, GW, 0, rb),
        out_shape=jax.ShapeDtypeStruct((ntot, GW), BF16 if finish else F32),
        scratch_shapes=[pltpu.VMEM((ML_HEADS, 2 * ML_DH, ML_DH), F32),
                        pltpu.VMEM((SUB, LANE), F32)],
        compiler_params=_cparams(("parallel", "arbitrary")),
        name="mlstm_bwd_finish" if rev else "mlstm_fwd",
    )(*args)


def _conv4(xe_scr, x, prev8, next8, w, first, last):
    t = x.shape[0]
    xe_scr[0:SUB, :] = jnp.where(first, 0.0, prev8)
    xe_scr[SUB:SUB + t, :] = x
    xe_scr[SUB + t:SUB + t + SUB, :] = jnp.where(last, 0.0, next8)
    out = w[0:1, :] * xe_scr[SUB - 2:SUB - 2 + t, :]
    for j in range(1, 4):
        out = out + w[j:j + 1, :] * xe_scr[SUB - 2 + j:SUB - 2 + j + t, :]
    return out


def _ssd_kernel(*refs, rev, finish, nc, nl):
    x_ref, xp_ref, xn_ref = refs[:3]
    cw_ref = refs[6]
    st_scr, xe_scr, xc_scr = refs[-3:]
    t = T_CHUNK
    i = pl.program_id(1)

    @pl.when(i == 0)
    def _():
        st_scr[...] = jnp.zeros_like(st_scr)

    is_ctx, c = _chunk_nat(i, rev, nc, nl)
    first = c == 0
    last = c == jnp.where(is_ctx, nc, nl) - 1
    xc = _conv4(xe_scr, x_ref[...], xp_ref[...], xn_ref[...], cw_ref[...], first, last)
    xc_scr[...] = xc * _sigmoid(xc)

    row, col = _iota2(t)
    valid = (col >= row) if rev else (col <= row)
    valid_bf = _bf(valid.astype(F32))
    for rs in _sub_chunks(x_ref.shape[0], rev):
        _ssd_chunk(rs, rev, finish, refs, valid, valid_bf)


def _ssd_chunk(rs, rev, finish, refs, valid, valid_bf):
    if finish:
        (x_ref, xp_ref, xn_ref, sm_ref, sb_ref, arow_ref, cw_ref,
         z_ref, of_ref, dsk_ref, ng_ref, o_ref, st_scr, xe_scr, xc_scr) = refs
    else:
        x_ref, xp_ref, xn_ref, sm_ref, sb_ref, arow_ref, cw_ref, o_ref, st_scr, xe_scr, xc_scr = refs
    p = SSD_HEADDIM
    n = SSD_STATE
    d = 1 if rev else 0
    xs = xc_scr[rs, :SSD_HEADS * p]
    bm = xc_scr[rs, SSD_HEADS * p:SSD_HEADS * p + SSD_GROUPS * n]
    cm = xc_scr[rs, SSD_HEADS * p + SSD_GROUPS * n:]

    dl = _softplus(sm_ref[rs, :] + sb_ref[...])
    lfm = dl * arow_ref[...]
    bcol = _split_dot(valid_bf, lfm)
    tot_row = jnp.sum(lfm, axis=0, keepdims=True)
    b_t = bcol.T
    dl_t = dl.T

    xsb = _bf(xs)
    rep = SSD_HEADS // SSD_GROUPS
    outs = []
    for gi in range(SSD_GROUPS):
        bg = bm[:, gi * n:(gi + 1) * n]
        cgb = _bf(cm[:, gi * n:(gi + 1) * n])
        gmat = _dot_nt(cgb, _bf(bg))
        for hh in range(rep):
            hd = gi * rep + hh
            cc = SM_DT + SSD_HEADS * d + hd
            bc = bcol[:, cc:cc + 1]
            br = b_t[cc:cc + 1, :]
            dr = dl_t[cc:cc + 1, :]
            dcol = dl[:, cc:cc + 1]
            tot = tot_row[:, cc:cc + 1]
            att = gmat * jnp.exp(jnp.where(valid, bc - br, NEG)) * dr
            xh = xsb[:, hd * p:(hd + 1) * p]
            st = st_scr[hd]
            o = _dot(_bf(att), xh) + jnp.exp(bc) * _dot_nt(cgb, _bf(st))
            wk = dcol * jnp.exp(tot - bc)
            st_scr[hd] = jnp.exp(tot) * st + _dot_tn(xh, _bf(bg * wk))
            outs.append(o)
    o_all = jnp.concatenate(outs, axis=-1)

    if finish:
        y = (o_all + of_ref[rs, :] + dsk_ref[...] * xs)
        z = z_ref[rs, :]
        y = y * (z * _sigmoid(z))
        gw = SSD_HEADS * p // SSD_GROUPS
        for gi in range(SSD_GROUPS):
            sl = slice(gi * gw, (gi + 1) * gw)
            yg = y[:, sl]
            yg = yg * lax.rsqrt(jnp.mean(yg * yg, axis=-1, keepdims=True) + NORM_EPS)
            o_ref[rs, sl] = (yg * ng_ref[:, sl]).astype(o_ref.dtype)
    else:
        o_ref[rs, :] = o_all


def _ssd_pass(u, sbias, arow, conv_w, z_extra, o_f, *, rev, nb, nc, nl):
    t = T_BLOCK
    ntot = u.shape[0]
    finish = o_f is not None
    rb = functools.partial(_row_block, rev=rev, nc=nc, nl=nl, nb=nb)
    r8 = t // SUB
    nblk8 = ntot // SUB
    cb = C_SSD_XBC // 1024

    def prev_map(b, i):
        return (jnp.maximum(rb(b, i) * r8 - 1, 0), cb)

    def next_map(b, i):
        return (jnp.minimum((rb(b, i) + 1) * r8, nblk8 - 1), cb)

    const2 = lambda b, i: (0, 0)
    in_specs = [_col_spec(t, 1024, C_SSD_XBC, rb),
                pl.BlockSpec((SUB, 1024), prev_map), pl.BlockSpec((SUB, 1024), next_map),
                _col_spec(t, LANE, C_SMALL, rb),
                pl.BlockSpec((1, LANE), const2), pl.BlockSpec((1, LANE), const2),
                pl.BlockSpec((4, 1024), const2)]
    args = [u, u, u, u, sbias, arow, conv_w]
    if finish:
        dsk, ng = z_extra
        in_specs += [_col_spec(t, GW, C_SSD_Z, rb), _col_spec(t, GW, 0, rb),
                     pl.BlockSpec((1, GW), const2), pl.BlockSpec((1, GW), const2)]
        args += [u, o_f, dsk, ng]
    return pl.pallas_call(
        functools.partial(_ssd_kernel, rev=rev, finish=finish, nc=nc, nl=nl),
        grid=(nb, nc + nl),
        in_specs=in_specs,
        out_specs=_col_spec(t, GW, 0, rb),
        out_shape=jax.ShapeDtypeStruct((ntot, GW), BF16 if finish else F32),
        scratch_shapes=[pltpu.VMEM((SSD_HEADS, SSD_HEADDIM, SSD_STATE), F32),
                        pltpu.VMEM((t + 2 * SUB, 1024), F32), pltpu.VMEM((t, 1024), F32)],
        compiler_params=_cparams(("parallel", "arbitrary")),
        name="ssd_bwd_finish" if rev else "ssd_fwd",
    )(*args)


def _lru_kernel(*refs, rev, finish, nc, nl):
    if finish:
        (x_ref, xp_ref, xn_ref, cw_ref, w2_ref, b2_ref, crow_ref,
         gate_ref, of_ref, o_ref, h_scr, xe_scr) = refs
    else:
        x_ref, xp_ref, xn_ref, cw_ref, w2_ref, b2_ref, crow_ref, o_ref, h_scr, xe_scr = refs
    t = x_ref.shape[0]
    w = x_ref.shape[1]
    ng = t // SUB
    i = pl.program_id(1)

    @pl.when(i == 0)
    def _():
        h_scr[...] = jnp.zeros_like(h_scr)

    is_ctx, c = _chunk_nat(i, rev, nc, nl)
    first = c == 0
    last = c == jnp.where(is_ctx, nc, nl) - 1
    xr = _conv4(xe_scr, x_ref[...], xp_ref[...], xn_ref[...], cw_ref[...], first, last)
    pre = _dot(_bf(xr), w2_ref[0]) + b2_ref[0]
    r = _sigmoid(pre[:, :w])
    ig = _sigmoid(pre[:, w:])
    log_a = crow_ref[0] * r
    a = jnp.exp(log_a)
    bb = jnp.sqrt(jnp.tanh(-log_a) * (1.0 + a * a)) * (ig * xr)

    a3 = a.reshape(ng, SUB, w)
    b3 = bb.reshape(ng, SUB, w)
    r8 = lax.broadcasted_iota(jnp.int32, (ng, SUB, w), 1)
    pos = (SUB - 1 - r8) if rev else r8
    step = 1
    while step < SUB:
        sh = (SUB - step) if rev else step
        a_s = pltpu.roll(a3, sh, 1)
        b_s = pltpu.roll(b3, sh, 1)
        ok = pos >= step
        b3 = jnp.where(ok, a3 * b_s + b3, b3)
        a3 = jnp.where(ok, a3 * a_s, a3)
        step *= 2
    h = h_scr[...]
    groups = [None] * ng
    for g in (range(ng - 1, -1, -1) if rev else range(ng)):
        hg = b3[g] + a3[g] * h
        groups[g] = hg
        h = hg[0:1, :] if rev else hg[SUB - 1:SUB, :]
    h_scr[...] = h
    hs = jnp.concatenate(groups, axis=0)

    if finish:
        g = gate_ref[...]
        gelu = 0.5 * g * (1.0 + jnp.tanh(math.sqrt(2.0 / math.pi) * (g + 0.044715 * (g * g * g))))
        o_ref[...] = ((hs + of_ref[...]) * gelu).astype(o_ref.dtype)
    else:
        o_ref[...] = hs


def _lru_pass(ul, consts, o_f, *, rev, nb, nc, nl):
    t = T_BLOCK
    ntot = ul.shape[0]
    finish = o_f is not None
    d = 1 if rev else 0
    rb = functools.partial(_row_block, rev=rev, nc=nc, nl=nl, nb=nb)
    r8 = t // SUB
    nblk8 = ntot // SUB
    cw, w2, b2, crow = consts
    const2 = lambda b, i: (0, 0)
    dsel = lambda b, i: (d, 0, 0)
    in_specs = [_col_spec(t, GW, GW, rb),
                pl.BlockSpec((SUB, GW), lambda b, i: (jnp.maximum(rb(b, i) * r8 - 1, 0), 1)),
                pl.BlockSpec((SUB, GW), lambda b, i: (jnp.minimum((rb(b, i) + 1) * r8, nblk8 - 1), 1)),
                pl.BlockSpec((4, GW), const2),
                pl.BlockSpec((1, GW, 2 * GW), dsel), pl.BlockSpec((1, 1, 2 * GW), dsel),
                pl.BlockSpec((1, 1, GW), dsel)]
    args = [ul, ul, ul, cw, w2, b2, crow]
    if finish:
        in_specs += [_col_spec(t, GW, 0, rb), _col_spec(t, GW, 0, rb)]
        args += [ul, o_f]
    return pl.pallas_call(
        functools.partial(_lru_kernel, rev=rev, finish=finish, nc=nc, nl=nl),
        grid=(nb, nc + nl),
        in_specs=in_specs,
        out_specs=_col_spec(t, GW, 0, rb),
        out_shape=jax.ShapeDtypeStruct((ntot, GW), BF16 if finish else F32),
        scratch_shapes=[pltpu.VMEM((1, GW), F32), pltpu.VMEM((t + 2 * SUB, GW), F32)],
        compiler_params=_cparams(("parallel", "arbitrary")),
        name="rglru_bwd_finish" if rev else "rglru_fwd",
    )(*args)


def _lru_mixer(u, consts, *, nb, seq, nc, nl):
    n_lat = nb * seq
    rows = seq // GRID_W
    ul = u[:, C_LRU_GATE:C_LRU_GATE + 2 * GW]
    lat = ul[:n_lat].reshape(nb, rows, GRID_W, 2 * GW).transpose(0, 2, 1, 3).reshape(n_lat, 2 * GW)
    ul = jnp.concatenate([lat, ul[n_lat:]], axis=0)
    o = _lru_pass(ul, consts, None, rev=False, nb=nb, nc=nc, nl=nl)
    y = _lru_pass(ul, consts, o, rev=True, nb=nb, nc=nc, nl=nl)
    lat = y[:n_lat].reshape(nb, GRID_W, rows, GW).transpose(0, 2, 1, 3).reshape(n_lat, GW)
    return jnp.concatenate([lat, y[n_lat:]], axis=0)


def _outproj_kernel(yh_ref, ys_ref, yl_ref, ym_ref, w_ref, x_ref, mod_ref, g_ref, wr_ref,
                    xn_ref, h_ref, lg_ref):
    acc = _dot(yh_ref[...], w_ref[0])
    acc = acc + _dot(ys_ref[...], w_ref[1])
    acc = acc + _dot(yl_ref[...], w_ref[2])
    acc = acc + _dot(ym_ref[...], w_ref[3])
    xn = x_ref[...] + mod_ref[0, 2:3, :] * acc
    xn_ref[...] = xn
    h = _rms_mod(xn, g_ref[...], mod_ref[0, 3:4, :], mod_ref[0, 4:5, :])
    tm, d = h.shape
    nch = d // LANE
    for j in range(nch):
        h_ref[pl.ds(j, tm, stride=nch), :] = h[:, j * LANE:(j + 1) * LANE]
    h1 = _bf(h)
    h2 = _bf(h - h1.astype(F32))
    p = _dot(h1, wr_ref[...])
    lg_ref[...] = p[:, :LANE] + p[:, LANE:] + _dot(h2, wr_ref[:, :LANE])


def _out_proj(ys, w_out4, x, mod, g2, w_router, *, n_lat, seq, b):
    ntot, d = x.shape
    tm = 512
    row = lambda i: (i, 0)
    mrow = _mod_row_map(tm, n_lat, seq, b)
    in_specs = [pl.BlockSpec((tm, GW), row)] * 4 + [
        pl.BlockSpec((4, GW, d), lambda i: (0, 0, 0)),
        pl.BlockSpec((tm, d), row), pl.BlockSpec((1, 6, d), mrow),
        pl.BlockSpec((1, d), lambda i: (0, 0)), pl.BlockSpec((d, 2 * LANE), lambda i: (0, 0))]
    return pl.pallas_call(
        _outproj_kernel,
        grid=(ntot // tm,),
        in_specs=in_specs,
        out_specs=[pl.BlockSpec((tm, d), row), pl.BlockSpec((tm * (d // LANE), LANE), row),
                   pl.BlockSpec((tm, LANE), row)],
        out_shape=[jax.ShapeDtypeStruct((ntot, d), F32), jax.ShapeDtypeStruct((ntot * (d // LANE), LANE), F32),
                   jax.ShapeDtypeStruct((ntot, LANE), F32)],
        compiler_params=_cparams(("parallel",)),
        name="out_proj_norm2_router",
    )(*ys, w_out4, x, mod, g2, w_router)


def _row_copy(h_hbm, xbuf, sem, tok, slot, r):
    nch = xbuf.shape[1] // MOE_BLOCK
    return pltpu.make_async_copy(h_hbm.at[pl.ds(tok * nch, nch), :], xbuf.at[slot, pl.ds(r * nch, nch), :],
                                 sem.at[slot])


def _moe_kernel(be_ref, tok_ref, h_hbm, wg_ref, wu_ref, wd_ref, o_ref, xbuf, sem, wgu_scr, wd_scr):
    i = pl.program_id(0)
    nblk = pl.num_programs(0)
    de = wg_ref.shape[-1]
    slot = i % 2

    def start_gather(blk, s):
        for r in range(MOE_BLOCK):
            _row_copy(h_hbm, xbuf, sem, tok_ref[blk * MOE_BLOCK + r], s, r).start(priority=r % 2)

    @pl.when(i == 0)
    def _():
        start_gather(0, 0)

    @pl.when(jnp.logical_or(i == 0, be_ref[i] != be_ref[jnp.maximum(i - 1, 0)]))
    def _():
        wgu_scr[:, :de] = _bf(wg_ref[0])
        wgu_scr[:, de:] = _bf(wu_ref[0])
        wd_scr[...] = _bf(wd_ref[0])

    for r in range(MOE_BLOCK):
        _row_copy(h_hbm, xbuf, sem, 0, slot, r).wait()
    start_gather(jnp.where(i + 1 < nblk, i + 1, 0), 1 - slot)

    nch = xbuf.shape[1] // MOE_BLOCK
    xb = jnp.concatenate([_bf(xbuf[slot, pl.ds(j, MOE_BLOCK, stride=nch), :]) for j in range(nch)], axis=-1)
    gu = _dot(xb, wgu_scr[...])
    g = gu[:, :de]
    hid = (g * _sigmoid(g)) * gu[:, de:]
    o_ref[...] = _bf(_dot(_bf(hid), wd_scr[...]))

    @pl.when(i == nblk - 1)
    def _():
        for r in range(MOE_BLOCK):
            _row_copy(h_hbm, xbuf, sem, 0, 1 - slot, r).wait()


def _moe_blocks(blk_exp, slot_tok, h, wg, wu, wd):
    nslot = slot_tok.shape[0]
    d = wg.shape[1]
    nch = d // LANE
    de = wg.shape[-1]
    nblk = nslot // MOE_BLOCK
    return pl.pallas_call(
        _moe_kernel,
        grid_spec=pltpu.PrefetchScalarGridSpec(
            num_scalar_prefetch=2,
            grid=(nblk,),
            in_specs=[pl.BlockSpec(memory_space=pl.ANY),
                      pl.BlockSpec((1, d, de), lambda i, be, tk: (be[i], 0, 0)),
                      pl.BlockSpec((1, d, de), lambda i, be, tk: (be[i], 0, 0)),
                      pl.BlockSpec((1, de, d), lambda i, be, tk: (be[i], 0, 0))],
            out_specs=pl.BlockSpec((MOE_BLOCK, d), lambda i, be, tk: (i, 0)),
            scratch_shapes=[pltpu.VMEM((2, MOE_BLOCK * nch, LANE), F32), pltpu.SemaphoreType.DMA((2,)),
                            pltpu.VMEM((d, 2 * de), BF16), pltpu.VMEM((de, d), BF16)]),
        out_shape=jax.ShapeDtypeStruct((nslot, d), BF16),
        compiler_params=_cparams(("arbitrary",)),
        name="moe_expert_blocks",
    )(blk_exp, slot_tok, h, wg, wu, wd)


def _route(logits):
    n_tok = logits.shape[0]
    n_exp = N_EXPERT_GROUPS * EXPERTS_PER_GROUP
    g_logits = logits[:, :N_EXPERT_GROUPS]
    g_prob = jax.nn.softmax(g_logits, axis=-1)
    g_sel = jnp.argmax(g_logits, axis=-1)
    g_hot = g_sel[:, None] == jnp.arange(N_EXPERT_GROUPS)[None, :]
    e_logits = logits[:, N_EXPERT_GROUPS:N_EXPERT_GROUPS + n_exp].reshape(
        n_tok, N_EXPERT_GROUPS, EXPERTS_PER_GROUP)
    within = jnp.sum(jnp.where(g_hot[:, :, None], e_logits, 0.0), axis=1)
    top_v, top_i = lax.top_k(within, TOP_K)
    wts = jax.nn.softmax(top_v, axis=-1) * jnp.sum(jnp.where(g_hot, g_prob, 0.0), axis=1, keepdims=True)
    eid = (g_sel[:, None] * EXPERTS_PER_GROUP + top_i).astype(jnp.int32)

    n_assign = n_tok * TOP_K
    flat_e = eid.reshape(-1)
    onehot = (flat_e[:, None] == jnp.arange(n_exp, dtype=jnp.int32)[None, :]).astype(jnp.int32)
    csum = jnp.cumsum(onehot, axis=0)
    counts = csum[-1]
    padded = (counts + MOE_BLOCK - 1) // MOE_BLOCK * MOE_BLOCK
    pend = jnp.cumsum(padded)
    pstart = pend - padded
    dest = jnp.sum(onehot * (csum - 1 + pstart[None, :]), axis=1)
    n_blk = -(-n_assign // MOE_BLOCK) + n_exp
    tok = jnp.arange(n_assign, dtype=jnp.int32) // TOP_K
    slot_tok = jnp.zeros((n_blk * MOE_BLOCK,), jnp.int32).at[dest].set(tok)
    blk_start = jnp.arange(n_blk, dtype=jnp.int32) * MOE_BLOCK
    blk_exp = jnp.minimum(jnp.sum(blk_start[:, None] >= pend[None, :], axis=1), n_exp - 1).astype(jnp.int32)
    return wts, dest, slot_tok, blk_exp


def _moe(h2, logits, layer, wg, wu, wd):
    n_tok = logits.shape[0]
    n_exp = wg.shape[1]
    wts, dest, slot_tok, blk_exp = _route(logits)
    flat = lambda w: w.reshape((w.shape[0] * n_exp,) + w.shape[2:])
    yb = _moe_blocks(blk_exp + layer * n_exp, slot_tok, h2, flat(wg), flat(wu), flat(wd))
    dest = dest.reshape(n_tok, TOP_K)
    take = functools.partial(jnp.take, axis=0, mode="clip")
    return take(yb, dest[:, 0]), take(yb, dest[:, 1]), wts


def _final_kernel(x_ref, y0_ref, y1_ref, wt_ref, pmod_ref, g_ref, o_ref):
    y = wt_ref[:, 0:1] * y0_ref[...].astype(F32) + wt_ref[:, 1:2] * y1_ref[...].astype(F32)
    x = x_ref[...] + pmod_ref[0, 5:6, :] * y
    o_ref[...] = x * lax.rsqrt(jnp.mean(x * x, axis=-1, keepdims=True) + NORM_EPS) * g_ref[...]


def _final(x, ymoe, pmod, g, *, n_lat, seq, b):
    d = x.shape[1]
    tm = 512
    row = lambda i: (i, 0)
    y0, y1, wts = ymoe
    return pl.pallas_call(
        _final_kernel,
        grid=(n_lat // tm,),
        in_specs=[pl.BlockSpec((tm, d), row), pl.BlockSpec((tm, d), row), pl.BlockSpec((tm, d), row),
                  pl.BlockSpec((tm, TOP_K), row),
                  pl.BlockSpec((1, 6, d), _mod_row_map(tm, n_lat, seq, b)),
                  pl.BlockSpec((1, d), lambda i: (0, 0))],
        out_specs=pl.BlockSpec((tm, d), row),
        out_shape=jax.ShapeDtypeStruct((n_lat, d), F32),
        compiler_params=_cparams(("parallel",)),
        name="final_residual_norm",
    )(x, y0, y1, wts, pmod, g)


def _arrange_w_in(w_in_l):
    d = w_in_l.shape[0]
    hg = w_in_l[:, :5 * GW]
    o = 5 * GW
    ssd_z = w_in_l[:, o:o + GW]
    ssd_xbc = w_in_l[:, o + GW:o + GW + 1024]
    ssd_dt = w_in_l[:, o + GW + 1024:o + GW + 1024 + 2 * SSD_HEADS]
    o += GW + 1024 + 2 * SSD_HEADS
    lru = w_in_l[:, o:o + 2 * GW]
    o += 2 * GW
    ml_main = w_in_l[:, o:o + 4 * GW]
    ml_gates = w_in_l[:, o + 4 * GW:o + 4 * GW + 4 * ML_HEADS]
    small = jnp.concatenate([ssd_dt, ml_gates], axis=1)
    pad = jnp.zeros((d, NCOLS - C_SMALL - small.shape[1]), w_in_l.dtype)
    return jnp.concatenate([hg, ssd_z, ssd_xbc, lru, ml_main, small, pad], axis=1)


def _blockdiag_dense(w):
    n, bw, _ = w.shape
    eye = jnp.eye(n, dtype=w.dtype)
    return (eye[:, None, :, None] * w[:, :, None, :]).reshape(n * bw, n * bw)


def _pad_lanes(v):
    return jnp.concatenate([v, jnp.zeros((LANE - v.shape[0],), v.dtype)])[None, :]


def kernel(x, c, ctx, c_ctx, ada_w, ada_b, norm1_g, norm2_g, w_in, w_out, hg_lb_logits, ssd_conv_w, ssd_a_log, ssd_dt_bias, ssd_d, ssd_norm_g, lru_conv_w, lru_wa, lru_ba, lru_wx, lru_bx, lru_lambda, ml_b_i, ml_b_f, router_group_w, router_expert_w, moe_w_gate, moe_w_up, moe_w_down, final_norm_g):
    nb, seq, d = x.shape
    nctx = ctx.shape[1]
    depth = ada_w.shape[0]
    n_lat = nb * seq
    assert nctx % T_BLOCK == 0 and seq % T_BLOCK == 0
    nc, nl = nctx // T_BLOCK, seq // T_BLOCK
    kw = dict(nb=nb, nc=nc, nl=nl)
    lay = dict(n_lat=n_lat, seq=seq, b=nb)

    p = jax.nn.softmax(hg_lb_logits.astype(F32), axis=0)
    lb_all = (jnp.cumsum(p, axis=0) - p[:1])[:, :, None, :]

    c_all = jnp.concatenate([c, c_ctx[None, :], jnp.zeros((SUB - nb - 1, d), F32)], axis=0)
    mods = _modulation(c_all, ada_w, ada_b).reshape(depth, SUB, 6, d)

    xall = jnp.concatenate([x.reshape(n_lat, d), ctx.reshape(nb * nctx, d)], axis=0)
    y_moe = None
    tn = 512
    for l in range(depth):
        w_l = _bf(_arrange_w_in(w_in[l])).reshape(d, NCOLS // tn, tn).transpose(1, 0, 2)
        xall, h1 = _res_norm(xall, y_moe, mods[l - 1] if l else None, norm1_g[l][None, :], mods[l], **lay)
        u = _in_proj(h1, w_l)

        o = _hg_pass(u, lb_all[l], None, rev=False, **kw)
        y_hg = _hg_pass(u, lb_all[l], o, rev=True, **kw)

        sbias = _pad_lanes(jnp.concatenate([ssd_dt_bias[l, 0], ssd_dt_bias[l, 1],
                                            ml_b_i[l, 0], ml_b_i[l, 1], ml_b_f[l, 0], ml_b_f[l, 1]]))
        arow = _pad_lanes(jnp.concatenate([-jnp.exp(ssd_a_log[l, 0]), -jnp.exp(ssd_a_log[l, 1])]))
        dsk = jnp.repeat(ssd_d[l], SSD_HEADDIM)[None, :]
        o = _ssd_pass(u, sbias, arow, ssd_conv_w[l], None, None, rev=False, **kw)
        y_ssd = _ssd_pass(u, sbias, arow, ssd_conv_w[l], (dsk, ssd_norm_g[l][None, :]), o, rev=True, **kw)

        w2 = _bf(jnp.stack([jnp.concatenate([_blockdiag_dense(lru_wa[l, dd]), _blockdiag_dense(lru_wx[l, dd])],
                                            axis=1) for dd in range(2)]))
        b2 = jnp.stack([jnp.concatenate([lru_ba[l, dd], lru_bx[l, dd]])[None, :] for dd in range(2)])
        crow = (-LRU_C * jax.nn.softplus(-lru_lambda[l]))[:, None, :]
        y_lru = _lru_mixer(u, (lru_conv_w[l], w2, b2, crow), nb=nb, seq=seq, nc=nc, nl=nl)

        o = _ml_pass(u, sbias, None, rev=False, **kw)
        y_ml = _ml_pass(u, sbias, o, rev=True, **kw)

        w_router = jnp.concatenate(
            [router_group_w[l], router_expert_w[l],
             jnp.zeros((d, LANE - N_EXPERT_GROUPS * (1 + EXPERTS_PER_GROUP)), F32)], axis=1)
        wr1 = _bf(w_router)
        wr12 = jnp.concatenate([wr1, _bf(w_router - wr1.astype(F32))], axis=1)
        xall, h2, logits = _out_proj((y_hg, y_ssd, y_lru, y_ml), _bf(w_out[l]).reshape(4, GW, d), xall,
                                     mods[l], norm2_g[l][None, :], wr12, **lay)
        y_moe = _moe(h2, logits, l, moe_w_gate, moe_w_up, moe_w_down)

    out = _final(xall, y_moe, mods[depth - 1], final_norm_g[None, :], **lay)
    return out.reshape(nb, seq, d)
```

```python
import functools
import math

import jax
import jax.numpy as jnp
from jax import lax
from jax.experimental import pallas as pl
from jax.experimental.pallas import tpu as pltpu

F32 = jnp.float32
BF16 = jnp.bfloat16

GRID_W = 64
HG_HEADS, HG_DK = 4, 128
SSD_HEADS, SSD_HEADDIM, SSD_GROUPS, SSD_STATE = 8, 64, 2, 128
LRU_C = 8.0
ML_HEADS, ML_DH = 4, 128
N_EXPERT_GROUPS, EXPERTS_PER_GROUP, TOP_K = 4, 8, 2
MOE_BLOCK = 256
NORM_EPS = 1e-6
NEG = -1e30

GW = 512
C_HG_Q, C_HG_I, C_HG_G, C_HG_ZF, C_HG_ZB = 0, 512, 1024, 1536, 2048
C_SSD_Z, C_SSD_XBC = 2560, 3072
C_LRU_GATE, C_LRU_XR = 4096, 4608
C_ML_Q, C_ML_K, C_ML_V, C_ML_O = 5120, 5632, 6144, 6656
C_SMALL = 7168
SM_DT, SM_IG, SM_FG = 0, 16, 24
NCOLS = 7680

LANE = 128
SUB = 8
VMEM_LIMIT = 48 * 1024 * 1024

T_CHUNK = 128
T_BLOCK = 2 * T_CHUNK


def _cparams(sem):
    return pltpu.CompilerParams(dimension_semantics=sem, vmem_limit_bytes=VMEM_LIMIT)


def _dot(a, b):
    return jnp.dot(a, b, preferred_element_type=F32)


def _dot_nt(a, b):
    return lax.dot_general(a, b, (((1,), (1,)), ((), ())), preferred_element_type=F32)


def _dot_tn(a, b):
    return lax.dot_general(a, b, (((0,), (0,)), ((), ())), preferred_element_type=F32)


def _bf(x):
    return x.astype(BF16)


def _sigmoid(x):
    return 0.5 * jnp.tanh(0.5 * x) + 0.5


def _softplus(x):
    return jnp.maximum(x, 0.0) + jnp.log1p(jnp.exp(-jnp.abs(x)))


def _split_dot(mask_bf, x):
    hi = _bf(x)
    r1 = x - hi.astype(F32)
    mid = _bf(r1)
    lo = _bf(r1 - mid.astype(F32))
    return _dot(mask_bf, hi) + _dot(mask_bf, mid) + _dot(mask_bf, lo)


def _rollb(x, n, rev):
    if n == 0:
        return x
    t = x.shape[0]
    return pltpu.roll(x, (t - n) if rev else n, 0)


def _iota2(t):
    return (lax.broadcasted_iota(jnp.int32, (t, t), 0), lax.broadcasted_iota(jnp.int32, (t, t), 1))


def _mod_kernel(c_ref, w_ref, b_ref, o_ref):
    c = c_ref[...]
    s = c * _sigmoid(c)
    o_ref[0] = jnp.dot(s, w_ref[0], precision=lax.Precision.HIGHEST,
                       preferred_element_type=F32) + b_ref[0]


def _modulation(c_all, ada_w, ada_b):
    depth, d, n6 = ada_w.shape
    tn = 1024
    return pl.pallas_call(
        _mod_kernel,
        grid=(depth, n6 // tn),
        in_specs=[pl.BlockSpec((SUB, d), lambda l, j: (0, 0)),
                  pl.BlockSpec((1, d, tn), lambda l, j: (l, 0, j)),
                  pl.BlockSpec((1, 1, tn), lambda l, j: (l, 0, j))],
        out_specs=pl.BlockSpec((1, SUB, tn), lambda l, j: (l, 0, j)),
        out_shape=jax.ShapeDtypeStruct((depth, SUB, n6), F32),
        compiler_params=_cparams(("parallel", "parallel")),
        name="adaln_modulation",
    )(c_all, ada_w, ada_b.reshape(depth, 1, n6))


def _rms_mod(x, g, shift, scale):
    y = x * lax.rsqrt(jnp.mean(x * x, axis=-1, keepdims=True) + NORM_EPS) * g
    return y * (1.0 + scale) + shift


def _mod_row_map(tm, n_lat, seq, b):
    def f(i, *_):
        return (jnp.where(i * tm < n_lat, (i * tm) // seq, b), 0, 0)
    return f


def _resnorm_kernel(*refs, residual):
    if residual:
        x_ref, y0_ref, y1_ref, wt_ref, pmod_ref, g_ref, mod_ref, xn_ref, h_ref = refs
        y = (wt_ref[:, 0:1] * y0_ref[...].astype(F32) + wt_ref[:, 1:2] * y1_ref[...].astype(F32))
        x = x_ref[...] + pmod_ref[0, 5:6, :] * y
        xn_ref[...] = x
    else:
        x_ref, g_ref, mod_ref, h_ref = refs
        x = x_ref[...]
    h_ref[...] = _bf(_rms_mod(x, g_ref[...], mod_ref[0, 0:1, :], mod_ref[0, 1:2, :]))


def _res_norm(x, ymoe, pmod, g1, mod, *, n_lat, seq, b):
    ntot, d = x.shape
    tm = 512
    residual = ymoe is not None
    row = lambda i: (i, 0)
    mrow = _mod_row_map(tm, n_lat, seq, b)
    in_specs = [pl.BlockSpec((tm, d), row)]
    args = [x]
    if residual:
        y0, y1, wts = ymoe
        in_specs += [pl.BlockSpec((tm, d), row), pl.BlockSpec((tm, d), row),
                     pl.BlockSpec((tm, TOP_K), row), pl.BlockSpec((1, 6, d), mrow)]
        args += [y0, y1, wts, pmod]
    in_specs += [pl.BlockSpec((1, d), lambda i: (0, 0)), pl.BlockSpec((1, 6, d), mrow)]
    args += [g1, mod]
    out_specs = [pl.BlockSpec((tm, d), row)]
    out_shape = [jax.ShapeDtypeStruct((ntot, d), BF16)]
    if residual:
        out_specs = [pl.BlockSpec((tm, d), row)] + out_specs
        out_shape = [jax.ShapeDtypeStruct((ntot, d), F32)] + out_shape
    res = pl.pallas_call(
        functools.partial(_resnorm_kernel, residual=residual),
        grid=(ntot // tm,),
        in_specs=in_specs, out_specs=out_specs, out_shape=out_shape,
        compiler_params=_cparams(("parallel",)),
        name="residual_norm1",
    )(*args)
    return (res[0], res[1]) if residual else (x, res[0])


def _inproj_kernel(h_ref, w_ref, u_ref):
    u_ref[...] = _dot(h_ref[...], w_ref[0])


def _in_proj(h, w_tiles):
    ntot, d = h.shape
    nt, _, tn = w_tiles.shape
    tm = 1536
    assert ntot % tm == 0
    return pl.pallas_call(
        _inproj_kernel,
        grid=(ntot // tm, nt),
        in_specs=[pl.BlockSpec((tm, d), lambda i, j: (i, 0)),
                  pl.BlockSpec((1, d, tn), lambda i, j: (j, 0, 0))],
        out_specs=pl.BlockSpec((tm, tn), lambda i, j: (i, j)),
        out_shape=jax.ShapeDtypeStruct((ntot, nt * tn), F32),
        compiler_params=_cparams(("parallel", "arbitrary")),
        name="in_proj",
    )(h, w_tiles)


def _chunk_nat(i, rev, nc, nl):
    is_ctx = i < nc
    if rev:
        c = jnp.where(is_ctx, nc - 1 - i, nl - 1 - (i - nc))
    else:
        c = jnp.where(is_ctx, i, i - nc)
    return is_ctx, c


def _row_block(b, i, *, rev, nc, nl, nb):
    is_ctx, c = _chunk_nat(i, rev, nc, nl)
    return jnp.where(is_ctx, nb * nl + b * nc + c, b * nl + c)


def _col_spec(t, width, col, rb):
    cb = col // width
    return pl.BlockSpec((t, width), lambda b, i: (rb(b, i), cb))


def _hg_kernel(*refs, rev, finish):
    if finish:
        q_ref, v_ref, z_ref, lb_ref, g_ref, of_ref, o_ref, st_scr = refs
    else:
        q_ref, v_ref, z_ref, lb_ref, o_ref, st_scr = refs
    t = T_CHUNK
    dk = HG_DK
    w = q_ref.shape[1]

    @pl.when(pl.program_id(1) == 0)
    def _():
        st_scr[...] = jnp.zeros_like(st_scr)

    lb = lb_ref[0]
    row, col = _iota2(t)
    valid = (col >= row) if rev else (col <= row)
    lvl = jnp.where(valid, 31 - lax.clz(row ^ col), -2)
    valid_bf = _bf(valid.astype(F32))
    tpos = lax.broadcasted_iota(jnp.int32, (t, w), 0)
    r8 = lax.broadcasted_iota(jnp.int32, (t // SUB, SUB, w), 1)
    for rs in _sub_chunks(q_ref.shape[0], rev):
        _hg_chunk(rs, rev, finish, refs, lb, valid_bf, lvl, tpos, r8)


def _sub_chunks(block_rows, rev):
    n = block_rows // T_CHUNK
    order = range(n - 1, -1, -1) if rev else range(n)
    return [slice(i * T_CHUNK, (i + 1) * T_CHUNK) for i in order]


def _hg_chunk(rs, rev, finish, refs, lb, valid_bf, lvl, tpos, r8):
    if finish:
        q_ref, v_ref, z_ref, lb_ref, g_ref, of_ref, o_ref, st_scr = refs
    else:
        q_ref, v_ref, z_ref, lb_ref, o_ref, st_scr = refs
    t = T_CHUNK
    dk = HG_DK
    w = q_ref.shape[1]
    sig = _sigmoid(z_ref[rs, :])
    f = lb + (1.0 - lb) * sig
    lf = jnp.log(f)
    k = (1.0 - lb) * (1.0 - sig)
    q = q_ref[rs, :]
    v = v_ref[rs, :]

    b = _split_dot(valid_bf, lf)
    total = jnp.sum(lf, axis=0, keepdims=True)

    b8 = b.reshape(t // SUB, SUB, w)
    ops = [(-1, _bf(q), _bf(k))]
    h = 1
    while h < t:
        blk = 2 * h
        lg = int(math.log2(h))
        is_q = ((tpos & h) == 0) if rev else ((tpos & h) != 0)
        if h == 1:
            e = f
        else:
            off = h if rev else h - 1
            if blk <= SUB:
                ref = b8[:, off:off + 1, :]
                for sb in range(1, SUB // blk):
                    ref = jnp.where(r8 >= sb * blk, b8[:, sb * blk + off:sb * blk + off + 1, :], ref)
                ref = jnp.broadcast_to(ref, b8.shape).reshape(t, w)
            else:
                b3 = b.reshape(t // blk, blk, w)
                ref = jnp.broadcast_to(b3[:, off:off + 1, :], b3.shape).reshape(t, w)
            e = jnp.exp(-jnp.abs(b - ref))
        if h == 1:
            m = _bf(jnp.where(is_q, q * e, k))
        else:
            m = _bf(jnp.where(is_q, q, k) * e)
        ops.append((lg, m, m))
        h = blk

    qb = _bf(q * jnp.exp(b))
    kw = _bf(k * jnp.exp(total - b))
    vb = _bf(v)
    dec_tot = jnp.exp(total)

    masks = [lvl == lg for lg, _, _ in ops]
    outs = []
    for hd in range(HG_HEADS):
        sl = slice(hd * dk, (hd + 1) * dk)
        a = jnp.zeros((t, t), F32)
        for msk, (lg, mq, mk) in zip(masks, ops):
            a = jnp.where(msk, _dot_nt(mq[:, sl], mk[:, sl]), a)
        st = st_scr[hd]
        o = _dot(_bf(a), vb[:, sl]) + _dot_nt(qb[:, sl], _bf(st))
        st_scr[hd] = st * dec_tot[:, sl] + _dot_tn(vb[:, sl], kw[:, sl])
        outs.append(o)

    if finish:
        for hd in range(HG_HEADS):
            sl = slice(hd * dk, (hd + 1) * dk)
            tot = outs[hd] + of_ref[rs, sl]
            y = tot * lax.rsqrt(jnp.mean(tot * tot, axis=-1, keepdims=True) + NORM_EPS)
            g = g_ref[rs, sl]
            o_ref[rs, sl] = (y * (g * _sigmoid(g))).astype(o_ref.dtype)
    else:
        for hd in range(HG_HEADS):
            o_ref[rs, hd * dk:(hd + 1) * dk] = outs[hd]


def _hg_pass(u, lb, o_f, *, rev, nb, nc, nl):
    t = T_BLOCK
    ntot = u.shape[0]
    finish = o_f is not None
    rb = functools.partial(_row_block, rev=rev, nc=nc, nl=nl, nb=nb)
    d = 1 if rev else 0
    in_specs = [_col_spec(t, GW, C_HG_Q, rb), _col_spec(t, GW, C_HG_I, rb),
                _col_spec(t, GW, C_HG_ZB if rev else C_HG_ZF, rb),
                pl.BlockSpec((1, 1, GW), lambda b, i: (d, 0, 0))]
    args = [u, u, u, lb]
    if finish:
        in_specs += [_col_spec(t, GW, C_HG_G, rb), _col_spec(t, GW, 0, rb)]
        args += [u, o_f]
    return pl.pallas_call(
        functools.partial(_hg_kernel, rev=rev, finish=finish),
        grid=(nb, nc + nl),
        in_specs=in_specs,
        out_specs=_col_spec(t, GW, 0, rb),
        out_shape=jax.ShapeDtypeStruct((ntot, GW), BF16 if finish else F32),
        scratch_shapes=[pltpu.VMEM((HG_HEADS, HG_DK, HG_DK), F32)],
        compiler_params=_cparams(("parallel", "arbitrary")),
        name="hgrn2_bwd_finish" if rev else "hgrn2_fwd",
    )(*args)


def _ml_kernel(*refs, rev, finish):
    ct_scr, m_scr = refs[-2:]
    t = T_CHUNK

    @pl.when(pl.program_id(1) == 0)
    def _():
        ct_scr[...] = jnp.zeros_like(ct_scr)
        m_scr[...] = jnp.full(m_scr.shape, NEG, F32)

    row, col = _iota2(t)
    valid = (col >= row) if rev else (col <= row)
    valid_bf = _bf(valid.astype(F32))
    for rs in _sub_chunks(refs[0].shape[0], rev):
        _ml_chunk(rs, rev, finish, refs, valid, valid_bf)


def _ml_chunk(rs, rev, finish, refs, valid, valid_bf):
    if finish:
        q_ref, k_ref, v_ref, sm_ref, sb_ref, og_ref, of_ref, o_ref, ct_scr, m_scr = refs
    else:
        q_ref, k_ref, v_ref, sm_ref, sb_ref, o_ref, ct_scr, m_scr = refs
    t = T_CHUNK
    dh = ML_DH
    d = 1 if rev else 0

    sm = sm_ref[rs, :] + sb_ref[...]
    lsg = jnp.minimum(sm, 0.0) - jnp.log1p(jnp.exp(-jnp.abs(sm)))
    bcol = _split_dot(valid_bf, lsg)
    tot_row = jnp.sum(lsg, axis=0, keepdims=True)
    b_t = bcol.T
    sm_t = sm.T

    ks = k_ref[rs, :] * (dh ** -0.5)
    qb, kb, vb = _bf(q_ref[rs, :]), _bf(ks), _bf(v_ref[rs, :])
    ones = jnp.ones((t, dh), BF16)

    for hd in range(ML_HEADS):
        sl = slice(hd * dh, (hd + 1) * dh)
        ci = SM_IG + ML_HEADS * d + hd
        cf = SM_FG + ML_HEADS * d + hd
        bc = bcol[:, cf:cf + 1]
        br = b_t[cf:cf + 1, :]
        lir = sm_t[ci:ci + 1, :]
        lic = sm[:, ci:ci + 1]
        m_prev = m_scr[hd:hd + 1, 0:1]
        dmat = jnp.where(valid, bc - br + lir, NEG)
        g = bc + m_prev
        mt = jnp.maximum(g, jnp.max(dmat, axis=-1, keepdims=True))
        w_intra = jnp.exp(dmat - mt)
        w_inter = jnp.exp(g - mt)
        s = _dot_nt(qb[:, sl], kb[:, sl]) * w_intra
        vaug = jnp.concatenate([vb[:, sl], ones], axis=-1)
        ct = ct_scr[hd]
        nd = _dot(_bf(s), vaug) + w_inter * _dot_nt(qb[:, sl], _bf(ct))
        hout = nd[:, :dh] / jnp.maximum(jnp.abs(nd[:, dh:]), jnp.exp(-mt))
        tot = tot_row[:, cf:cf + 1]
        dl = tot - bc + lic
        m_new = jnp.maximum(tot + m_prev, jnp.max(dl, axis=0, keepdims=True))
        ws = jnp.exp(dl - m_new)
        wc = jnp.exp(tot + m_prev - m_new)
        ct_scr[hd] = wc * ct + _dot_tn(vaug, _bf(ks[:, sl] * ws))
        m_scr[hd:hd + 1, :] = jnp.broadcast_to(m_new, (1, m_scr.shape[1]))
        if finish:
            tot_o = hout + of_ref[rs, sl]
            y = tot_o * lax.rsqrt(jnp.mean(tot_o * tot_o, axis=-1, keepdims=True) + NORM_EPS)
            o_ref[rs, sl] = (y * _sigmoid(og_ref[rs, sl])).astype(o_ref.dtype)
        else:
            o_ref[rs, sl] = hout


def _ml_pass(u, sbias, o_f, *, rev, nb, nc, nl):
    t = T_BLOCK
    ntot = u.shape[0]
    finish = o_f is not None
    rb = functools.partial(_row_block, rev=rev, nc=nc, nl=nl, nb=nb)
    in_specs = [_col_spec(t, GW, C_ML_Q, rb), _col_spec(t, GW, C_ML_K, rb),
                _col_spec(t, GW, C_ML_V, rb), _col_spec(t, LANE, C_SMALL, rb),
                pl.BlockSpec((1, LANE), lambda b, i: (0, 0))]
    args = [u, u, u, u, sbias]
    if finish:
        in_specs += [_col_spec(t, GW, C_ML_O, rb), _col_spec(t, GW, 0, rb)]
        args += [u, o_f]
    return pl.pallas_call(
        functools.partial(_ml_kernel, rev=rev, finish=finish),
        grid=(nb, nc + nl),
        in_specs=in_specs,
        out_specs=_col_spec(t, GW, 0, rb),
        out_shape=jax.ShapeDtypeStruct((ntot, GW), BF16 if finish else F32),
        scratch_shapes=[pltpu.VMEM((ML_HEADS, 2 * ML_DH, ML_DH), F32),
                        pltpu.VMEM((SUB, LANE), F32)],
        compiler_params=_cparams(("parallel", "arbitrary")),
        name="mlstm_bwd_finish" if rev else "mlstm_fwd",
    )(*args)


def _conv4(xe_scr, x, prev8, next8, w, first, last):
    t = x.shape[0]
    xe_scr[0:SUB, :] = jnp.where(first, 0.0, prev8)
    xe_scr[SUB:SUB + t, :] = x
    xe_scr[SUB + t:SUB + t + SUB, :] = jnp.where(last, 0.0, next8)
    out = w[0:1, :] * xe_scr[SUB - 2:SUB - 2 + t, :]
    for j in range(1, 4):
        out = out + w[j:j + 1, :] * xe_scr[SUB - 2 + j:SUB - 2 + j + t, :]
    return out


def _ssd_kernel(*refs, rev, finish, nc, nl):
    x_ref, xp_ref, xn_ref = refs[:3]
    cw_ref = refs[6]
    st_scr, xe_scr, xc_scr = refs[-3:]
    t = T_CHUNK
    i = pl.program_id(1)

    @pl.when(i == 0)
    def _():
        st_scr[...] = jnp.zeros_like(st_scr)

    is_ctx, c = _chunk_nat(i, rev, nc, nl)
    first = c == 0
    last = c == jnp.where(is_ctx, nc, nl) - 1
    xc = _conv4(xe_scr, x_ref[...], xp_ref[...], xn_ref[...], cw_ref[...], first, last)
    xc_scr[...] = xc * _sigmoid(xc)

    row, col = _iota2(t)
    valid = (col >= row) if rev else (col <= row)
    valid_bf = _bf(valid.astype(F32))
    for rs in _sub_chunks(x_ref.shape[0], rev):
        _ssd_chunk(rs, rev, finish, refs, valid, valid_bf)


def _ssd_chunk(rs, rev, finish, refs, valid, valid_bf):
    if finish:
        (x_ref, xp_ref, xn_ref, sm_ref, sb_ref, arow_ref, cw_ref,
         z_ref, of_ref, dsk_ref, ng_ref, o_ref, st_scr, xe_scr, xc_scr) = refs
    else:
        x_ref, xp_ref, xn_ref, sm_ref, sb_ref, arow_ref, cw_ref, o_ref, st_scr, xe_scr, xc_scr = refs
    p = SSD_HEADDIM
    n = SSD_STATE
    d = 1 if rev else 0
    xs = xc_scr[rs, :SSD_HEADS * p]
    bm = xc_scr[rs, SSD_HEADS * p:SSD_HEADS * p + SSD_GROUPS * n]
    cm = xc_scr[rs, SSD_HEADS * p + SSD_GROUPS * n:]

    dl = _softplus(sm_ref[rs, :] + sb_ref[...])
    lfm = dl * arow_ref[...]
    bcol = _split_dot(valid_bf, lfm)
    tot_row = jnp.sum(lfm, axis=0, keepdims=True)
    b_t = bcol.T
    dl_t = dl.T

    xsb = _bf(xs)
    rep = SSD_HEADS // SSD_GROUPS
    outs = []
    for gi in range(SSD_GROUPS):
        bg = bm[:, gi * n:(gi + 1) * n]
        cgb = _bf(cm[:, gi * n:(gi + 1) * n])
        gmat = _dot_nt(cgb, _bf(bg))
        for hh in range(rep):
            hd = gi * rep + hh
            cc = SM_DT + SSD_HEADS * d + hd
            bc = bcol[:, cc:cc + 1]
            br = b_t[cc:cc + 1, :]
            dr = dl_t[cc:cc + 1, :]
            dcol = dl[:, cc:cc + 1]
            tot = tot_row[:, cc:cc + 1]
            att = gmat * jnp.exp(jnp.where(valid, bc - br, NEG)) * dr
            xh = xsb[:, hd * p:(hd + 1) * p]
            st = st_scr[hd]
            o = _dot(_bf(att), xh) + jnp.exp(bc) * _dot_nt(cgb, _bf(st))
            wk = dcol * jnp.exp(tot - bc)
            st_scr[hd] = jnp.exp(tot) * st + _dot_tn(xh, _bf(bg * wk))
            outs.append(o)
    o_all = jnp.concatenate(outs, axis=-1)

    if finish:
        y = (o_all + of_ref[rs, :] + dsk_ref[...] * xs)
        z = z_ref[rs, :]
        y = y * (z * _sigmoid(z))
        gw = SSD_HEADS * p // SSD_GROUPS
        for gi in range(SSD_GROUPS):
            sl = slice(gi * gw, (gi + 1) * gw)
            yg = y[:, sl]
            yg = yg * lax.rsqrt(jnp.mean(yg * yg, axis=-1, keepdims=True) + NORM_EPS)
            o_ref[rs, sl] = (yg * ng_ref[:, sl]).astype(o_ref.dtype)
    else:
        o_ref[rs, :] = o_all


def _ssd_pass(u, sbias, arow, conv_w, z_extra, o_f, *, rev, nb, nc, nl):
    t = T_BLOCK
    ntot = u.shape[0]
    finish = o_f is not None
    rb = functools.partial(_row_block, rev=rev, nc=nc, nl=nl, nb=nb)
    r8 = t // SUB
    nblk8 = ntot // SUB
    cb = C_SSD_XBC // 1024

    def prev_map(b, i):
        return (jnp.maximum(rb(b, i) * r8 - 1, 0), cb)

    def next_map(b, i):
        return (jnp.minimum((rb(b, i) + 1) * r8, nblk8 - 1), cb)

    const2 = lambda b, i: (0, 0)
    in_specs = [_col_spec(t, 1024, C_SSD_XBC, rb),
                pl.BlockSpec((SUB, 1024), prev_map), pl.BlockSpec((SUB, 1024), next_map),
                _col_spec(t, LANE, C_SMALL, rb),
                pl.BlockSpec((1, LANE), const2), pl.BlockSpec((1, LANE), const2),
                pl.BlockSpec((4, 1024), const2)]
    args = [u, u, u, u, sbias, arow, conv_w]
    if finish:
        dsk, ng = z_extra
        in_specs += [_col_spec(t, GW, C_SSD_Z, rb), _col_spec(t, GW, 0, rb),
                     pl.BlockSpec((1, GW), const2), pl.BlockSpec((1, GW), const2)]
        args += [u, o_f, dsk, ng]
    return pl.pallas_call(
        functools.partial(_ssd_kernel, rev=rev, finish=finish, nc=nc, nl=nl),
        grid=(nb, nc + nl),
        in_specs=in_specs,
        out_specs=_col_spec(t, GW, 0, rb),
        out_shape=jax.ShapeDtypeStruct((ntot, GW), BF16 if finish else F32),
        scratch_shapes=[pltpu.VMEM((SSD_HEADS, SSD_HEADDIM, SSD_STATE), F32),
                        pltpu.VMEM((t + 2 * SUB, 1024), F32), pltpu.VMEM((t, 1024), F32)],
        compiler_params=_cparams(("parallel", "arbitrary")),
        name="ssd_bwd_finish" if rev else "ssd_fwd",
    )(*args)


def _lru_kernel(*refs, rev, finish, nc, nl):
    if finish:
        (x_ref, xp_ref, xn_ref, cw_ref, w2_ref, b2_ref, crow_ref,
         gate_ref, of_ref, o_ref, h_scr, xe_scr) = refs
    else:
        x_ref, xp_ref, xn_ref, cw_ref, w2_ref, b2_ref, crow_ref, o_ref, h_scr, xe_scr = refs
    t = x_ref.shape[0]
    w = x_ref.shape[1]
    ng = t // SUB
    i = pl.program_id(1)

    @pl.when(i == 0)
    def _():
        h_scr[...] = jnp.zeros_like(h_scr)

    is_ctx, c = _chunk_nat(i, rev, nc, nl)
    first = c == 0
    last = c == jnp.where(is_ctx, nc, nl) - 1
    xr = _conv4(xe_scr, x_ref[...], xp_ref[...], xn_ref[...], cw_ref[...], first, last)
    pre = _dot(_bf(xr), w2_ref[0]) + b2_ref[0]
    r = _sigmoid(pre[:, :w])
    ig = _sigmoid(pre[:, w:])
    log_a = crow_ref[0] * r
    a = jnp.exp(log_a)
    bb = jnp.sqrt(jnp.tanh(-log_a) * (1.0 + a * a)) * (ig * xr)

    a3 = a.reshape(ng, SUB, w)
    b3 = bb.reshape(ng, SUB, w)
    r8 = lax.broadcasted_iota(jnp.int32, (ng, SUB, w), 1)
    pos = (SUB - 1 - r8) if rev else r8
    step = 1
    while step < SUB:
        sh = (SUB - step) if rev else step
        a_s = pltpu.roll(a3, sh, 1)
        b_s = pltpu.roll(b3, sh, 1)
        ok = pos >= step
        b3 = jnp.where(ok, a3 * b_s + b3, b3)
        a3 = jnp.where(ok, a3 * a_s, a3)
        step *= 2
    h = h_scr[...]
    groups = [None] * ng
    for g in (range(ng - 1, -1, -1) if rev else range(ng)):
        hg = b3[g] + a3[g] * h
        groups[g] = hg
        h = hg[0:1, :] if rev else hg[SUB - 1:SUB, :]
    h_scr[...] = h
    hs = jnp.concatenate(groups, axis=0)

    if finish:
        g = gate_ref[...]
        gelu = 0.5 * g * (1.0 + jnp.tanh(math.sqrt(2.0 / math.pi) * (g + 0.044715 * (g * g * g))))
        o_ref[...] = ((hs + of_ref[...]) * gelu).astype(o_ref.dtype)
    else:
        o_ref[...] = hs


def _lru_pass(ul, consts, o_f, *, rev, nb, nc, nl):
    t = T_BLOCK
    ntot = ul.shape[0]
    finish = o_f is not None
    d = 1 if rev else 0
    rb = functools.partial(_row_block, rev=rev, nc=nc, nl=nl, nb=nb)
    r8 = t // SUB
    nblk8 = ntot // SUB
    cw, w2, b2, crow = consts
    const2 = lambda b, i: (0, 0)
    dsel = lambda b, i: (d, 0, 0)
    in_specs = [_col_spec(t, GW, GW, rb),
                pl.BlockSpec((SUB, GW), lambda b, i: (jnp.maximum(rb(b, i) * r8 - 1, 0), 1)),
                pl.BlockSpec((SUB, GW), lambda b, i: (jnp.minimum((rb(b, i) + 1) * r8, nblk8 - 1), 1)),
                pl.BlockSpec((4, GW), const2),
                pl.BlockSpec((1, GW, 2 * GW), dsel), pl.BlockSpec((1, 1, 2 * GW), dsel),
                pl.BlockSpec((1, 1, GW), dsel)]
    args = [ul, ul, ul, cw, w2, b2, crow]
    if finish:
        in_specs += [_col_spec(t, GW, 0, rb), _col_spec(t, GW, 0, rb)]
        args += [ul, o_f]
    return pl.pallas_call(
        functools.partial(_lru_kernel, rev=rev, finish=finish, nc=nc, nl=nl),
        grid=(nb, nc + nl),
        in_specs=in_specs,
        out_specs=_col_spec(t, GW, 0, rb),
        out_shape=jax.ShapeDtypeStruct((ntot, GW), BF16 if finish else F32),
        scratch_shapes=[pltpu.VMEM((1, GW), F32), pltpu.VMEM((t + 2 * SUB, GW), F32)],
        compiler_params=_cparams(("parallel", "arbitrary")),
        name="rglru_bwd_finish" if rev else "rglru_fwd",
    )(*args)


def _lru_mixer(u, consts, *, nb, seq, nc, nl):
    n_lat = nb * seq
    rows = seq // GRID_W
    ul = u[:, C_LRU_GATE:C_LRU_GATE + 2 * GW]
    lat = ul[:n_lat].reshape(nb, rows, GRID_W, 2 * GW).transpose(0, 2, 1, 3).reshape(n_lat, 2 * GW)
    ul = jnp.concatenate([lat, ul[n_lat:]], axis=0)
    o = _lru_pass(ul, consts, None, rev=False, nb=nb, nc=nc, nl=nl)
    y = _lru_pass(ul, consts, o, rev=True, nb=nb, nc=nc, nl=nl)
    lat = y[:n_lat].reshape(nb, GRID_W, rows, GW).transpose(0, 2, 1, 3).reshape(n_lat, GW)
    return jnp.concatenate([lat, y[n_lat:]], axis=0)


def _outproj_kernel(yh_ref, ys_ref, yl_ref, ym_ref, w_ref, x_ref, mod_ref, g_ref, wr_ref,
                    xn_ref, h_ref, lg_ref):
    acc = _dot(yh_ref[...], w_ref[0])
    acc = acc + _dot(ys_ref[...], w_ref[1])
    acc = acc + _dot(yl_ref[...], w_ref[2])
    acc = acc + _dot(ym_ref[...], w_ref[3])
    xn = x_ref[...] + mod_ref[0, 2:3, :] * acc
    xn_ref[...] = xn
    h = _rms_mod(xn, g_ref[...], mod_ref[0, 3:4, :], mod_ref[0, 4:5, :])
    h1 = _bf(h)
    tm, d = h.shape
    bits = lax.bitcast_convert_type(h1.astype(F32), jnp.uint32)
    packed = (bits[:, :d // 2] >> 16) | (bits[:, d // 2:] & jnp.uint32(0xFFFF0000))
    nch = d // 2 // LANE
    for j in range(nch):
        h_ref[pl.ds(j, tm, stride=nch), :] = packed[:, j * LANE:(j + 1) * LANE]
    h2 = _bf(h - h1.astype(F32))
    p = _dot(h1, wr_ref[...])
    lg_ref[...] = p[:, :LANE] + p[:, LANE:] + _dot(h2, wr_ref[:, :LANE])


def _out_proj(ys, w_out4, x, mod, g2, w_router, *, n_lat, seq, b):
    ntot, d = x.shape
    tm = 512
    row = lambda i: (i, 0)
    mrow = _mod_row_map(tm, n_lat, seq, b)
    in_specs = [pl.BlockSpec((tm, GW), row)] * 4 + [
        pl.BlockSpec((4, GW, d), lambda i: (0, 0, 0)),
        pl.BlockSpec((tm, d), row), pl.BlockSpec((1, 6, d), mrow),
        pl.BlockSpec((1, d), lambda i: (0, 0)), pl.BlockSpec((d, 2 * LANE), lambda i: (0, 0))]
    return pl.pallas_call(
        _outproj_kernel,
        grid=(ntot // tm,),
        in_specs=in_specs,
        out_specs=[pl.BlockSpec((tm, d), row), pl.BlockSpec((tm * (d // 2 // LANE), LANE), row),
                   pl.BlockSpec((tm, LANE), row)],
        out_shape=[jax.ShapeDtypeStruct((ntot, d), F32),
                   jax.ShapeDtypeStruct((ntot * (d // 2 // LANE), LANE), jnp.uint32),
                   jax.ShapeDtypeStruct((ntot, LANE), F32)],
        compiler_params=_cparams(("parallel",)),
        name="out_proj_norm2_router",
    )(*ys, w_out4, x, mod, g2, w_router)


def _row_copy(h_hbm, xbuf, sem, tok, slot, r):
    nch = xbuf.shape[1] // MOE_BLOCK
    return pltpu.make_async_copy(h_hbm.at[pl.ds(tok * nch, nch), :], xbuf.at[slot, pl.ds(r * nch, nch), :],
                                 sem.at[slot])


def _moe_kernel(be_ref, tok_ref, na_ref, h_hbm, wg_ref, wu_ref, wd_ref, o_ref, xbuf, sem, wgu_scr, wd_scr):
    i = pl.program_id(0)
    n_active = na_ref[0]
    de = wg_ref.shape[-1]
    slot = i % 2
    nch = xbuf.shape[1] // MOE_BLOCK

    def start_gather(blk, s):
        for r in range(MOE_BLOCK):
            _row_copy(h_hbm, xbuf, sem, tok_ref[blk * MOE_BLOCK + r], s, r).start()

    @pl.when(i == 0)
    def _():
        start_gather(0, 0)

    @pl.when(i < n_active)
    def _():
        @pl.when(jnp.logical_or(i == 0, be_ref[i] != be_ref[jnp.maximum(i - 1, 0)]))
        def _():
            wgu_scr[:, :de] = _bf(wg_ref[0])
            wgu_scr[:, de:] = _bf(wu_ref[0])
            wd_scr[...] = _bf(wd_ref[0])

        for r in range(MOE_BLOCK):
            _row_copy(h_hbm, xbuf, sem, 0, slot, r).wait()

        @pl.when(i + 1 < n_active)
        def _():
            start_gather(i + 1, 1 - slot)

        lo, hi = [], []
        for j in range(nch):
            wv = xbuf[slot, pl.ds(j, MOE_BLOCK, stride=nch), :]
            lo.append(_bf(lax.bitcast_convert_type(wv << 16, F32)))
            hi.append(_bf(lax.bitcast_convert_type(wv & jnp.uint32(0xFFFF0000), F32)))
        xb = jnp.concatenate(lo + hi, axis=-1)
        gu = _dot(xb, wgu_scr[...])
        g = gu[:, :de]
        hid = (g * _sigmoid(g)) * gu[:, de:]
        o_ref[...] = _bf(_dot(_bf(hid), wd_scr[...]))

    @pl.when(i >= n_active)
    def _():
        o_ref[...] = jnp.zeros_like(o_ref)


def _moe_blocks(blk_exp, slot_tok, n_active, h, wg, wu, wd):
    nslot = slot_tok.shape[0]
    d = wg.shape[1]
    nch = d // 2 // LANE
    de = wg.shape[-1]
    nblk = nslot // MOE_BLOCK
    return pl.pallas_call(
        _moe_kernel,
        grid_spec=pltpu.PrefetchScalarGridSpec(
            num_scalar_prefetch=3,
            grid=(nblk,),
            in_specs=[pl.BlockSpec(memory_space=pl.ANY),
                      pl.BlockSpec((1, d, de), lambda i, be, tk, na: (be[i], 0, 0)),
                      pl.BlockSpec((1, d, de), lambda i, be, tk, na: (be[i], 0, 0)),
                      pl.BlockSpec((1, de, d), lambda i, be, tk, na: (be[i], 0, 0))],
            out_specs=pl.BlockSpec((MOE_BLOCK, d), lambda i, be, tk, na: (i, 0)),
            scratch_shapes=[pltpu.VMEM((2, MOE_BLOCK * nch, LANE), jnp.uint32), pltpu.SemaphoreType.DMA((2,)),
                            pltpu.VMEM((d, 2 * de), BF16), pltpu.VMEM((de, d), BF16)]),
        out_shape=jax.ShapeDtypeStruct((nslot, d), BF16),
        compiler_params=_cparams(("arbitrary",)),
        name="moe_expert_blocks",
    )(blk_exp, slot_tok, n_active, h, wg, wu, wd)


def _route(logits):
    n_tok = logits.shape[0]
    n_exp = N_EXPERT_GROUPS * EXPERTS_PER_GROUP
    g_logits = logits[:, :N_EXPERT_GROUPS]
    g_prob = jax.nn.softmax(g_logits, axis=-1)
    g_sel = jnp.argmax(g_logits, axis=-1)
    g_hot = g_sel[:, None] == jnp.arange(N_EXPERT_GROUPS)[None, :]
    e_logits = logits[:, N_EXPERT_GROUPS:N_EXPERT_GROUPS + n_exp].reshape(
        n_tok, N_EXPERT_GROUPS, EXPERTS_PER_GROUP)
    within = jnp.sum(jnp.where(g_hot[:, :, None], e_logits, 0.0), axis=1)
    top_v, top_i = lax.top_k(within, TOP_K)
    wts = jax.nn.softmax(top_v, axis=-1) * jnp.sum(jnp.where(g_hot, g_prob, 0.0), axis=1, keepdims=True)
    eid = (g_sel[:, None] * EXPERTS_PER_GROUP + top_i).astype(jnp.int32)

    n_assign = n_tok * TOP_K
    flat_e = eid.reshape(-1)
    onehot = (flat_e[:, None] == jnp.arange(n_exp, dtype=jnp.int32)[None, :]).astype(jnp.int32)
    csum = jnp.cumsum(onehot, axis=0)
    counts = csum[-1]
    padded = (counts + MOE_BLOCK - 1) // MOE_BLOCK * MOE_BLOCK
    pend = jnp.cumsum(padded)
    pstart = pend - padded
    dest = jnp.sum(onehot * (csum - 1 + pstart[None, :]), axis=1)
    n_blk = -(-n_assign // MOE_BLOCK) + n_exp
    tok = jnp.arange(n_assign, dtype=jnp.int32) // TOP_K
    slot_tok = jnp.zeros((n_blk * MOE_BLOCK,), jnp.int32).at[dest].set(tok)
    blk_start = jnp.arange(n_blk, dtype=jnp.int32) * MOE_BLOCK
    blk_exp = jnp.minimum(jnp.sum(blk_start[:, None] >= pend[None, :], axis=1), n_exp - 1).astype(jnp.int32)
    n_active = (pend[-1:] // MOE_BLOCK).astype(jnp.int32)
    return wts, dest, slot_tok, blk_exp, n_active


def _moe(h2, logits, layer, wg, wu, wd):
    n_tok = logits.shape[0]
    n_exp = wg.shape[1]
    wts, dest, slot_tok, blk_exp, n_active = _route(logits)
    flat = lambda w: w.reshape((w.shape[0] * n_exp,) + w.shape[2:])
    yb = _moe_blocks(blk_exp + layer * n_exp, slot_tok, n_active, h2, flat(wg), flat(wu), flat(wd))
    dest = dest.reshape(n_tok, TOP_K)
    take = functools.partial(jnp.take, axis=0, mode="clip")
    return take(yb, dest[:, 0]), take(yb, dest[:, 1]), wts


def _final_kernel(x_ref, y0_ref, y1_ref, wt_ref, pmod_ref, g_ref, o_ref):
    y = wt_ref[:, 0:1] * y0_ref[...].astype(F32) + wt_ref[:, 1:2] * y1_ref[...].astype(F32)
    x = x_ref[...] + pmod_ref[0, 5:6, :] * y
    o_ref[...] = x * lax.rsqrt(jnp.mean(x * x, axis=-1, keepdims=True) + NORM_EPS) * g_ref[...]


def _final(x, ymoe, pmod, g, *, n_lat, seq, b):
    d = x.shape[1]
    tm = 512
    row = lambda i: (i, 0)
    y0, y1, wts = ymoe
    return pl.pallas_call(
        _final_kernel,
        grid=(n_lat // tm,),
        in_specs=[pl.BlockSpec((tm, d), row), pl.BlockSpec((tm, d), row), pl.BlockSpec((tm, d), row),
                  pl.BlockSpec((tm, TOP_K), row),
                  pl.BlockSpec((1, 6, d), _mod_row_map(tm, n_lat, seq, b)),
                  pl.BlockSpec((1, d), lambda i: (0, 0))],
        out_specs=pl.BlockSpec((tm, d), row),
        out_shape=jax.ShapeDtypeStruct((n_lat, d), F32),
        compiler_params=_cparams(("parallel",)),
        name="final_residual_norm",
    )(x, y0, y1, wts, pmod, g)


def _arrange_w_in(w_in_l):
    d = w_in_l.shape[0]
    hg = w_in_l[:, :5 * GW]
    o = 5 * GW
    ssd_z = w_in_l[:, o:o + GW]
    ssd_xbc = w_in_l[:, o + GW:o + GW + 1024]
    ssd_dt = w_in_l[:, o + GW + 1024:o + GW + 1024 + 2 * SSD_HEADS]
    o += GW + 1024 + 2 * SSD_HEADS
    lru = w_in_l[:, o:o + 2 * GW]
    o += 2 * GW
    ml_main = w_in_l[:, o:o + 4 * GW]
    ml_gates = w_in_l[:, o + 4 * GW:o + 4 * GW + 4 * ML_HEADS]
    small = jnp.concatenate([ssd_dt, ml_gates], axis=1)
    pad = jnp.zeros((d, NCOLS - C_SMALL - small.shape[1]), w_in_l.dtype)
    return jnp.concatenate([hg, ssd_z, ssd_xbc, lru, ml_main, small, pad], axis=1)


def _blockdiag_dense(w):
    n, bw, _ = w.shape
    eye = jnp.eye(n, dtype=w.dtype)
    return (eye[:, None, :, None] * w[:, :, None, :]).reshape(n * bw, n * bw)


def _pad_lanes(v):
    return jnp.concatenate([v, jnp.zeros((LANE - v.shape[0],), v.dtype)])[None, :]


def kernel(x, c, ctx, c_ctx, ada_w, ada_b, norm1_g, norm2_g, w_in, w_out, hg_lb_logits, ssd_conv_w, ssd_a_log, ssd_dt_bias, ssd_d, ssd_norm_g, lru_conv_w, lru_wa, lru_ba, lru_wx, lru_bx, lru_lambda, ml_b_i, ml_b_f, router_group_w, router_expert_w, moe_w_gate, moe_w_up, moe_w_down, final_norm_g):
    nb, seq, d = x.shape
    nctx = ctx.shape[1]
    depth = ada_w.shape[0]
    n_lat = nb * seq
    assert nctx % T_BLOCK == 0 and seq % T_BLOCK == 0
    nc, nl = nctx // T_BLOCK, seq // T_BLOCK
    kw = dict(nb=nb, nc=nc, nl=nl)
    lay = dict(n_lat=n_lat, seq=seq, b=nb)

    p = jax.nn.softmax(hg_lb_logits.astype(F32), axis=0)
    lb_all = (jnp.cumsum(p, axis=0) - p[:1])[:, :, None, :]

    c_all = jnp.concatenate([c, c_ctx[None, :], jnp.zeros((SUB - nb - 1, d), F32)], axis=0)
    mods = _modulation(c_all, ada_w, ada_b).reshape(depth, SUB, 6, d)

    xall = jnp.concatenate([x.reshape(n_lat, d), ctx.reshape(nb * nctx, d)], axis=0)
    y_moe = None
    tn = 512
    for l in range(depth):
        w_l = _bf(_arrange_w_in(w_in[l])).reshape(d, NCOLS // tn, tn).transpose(1, 0, 2)
        xall, h1 = _res_norm(xall, y_moe, mods[l - 1] if l else None, norm1_g[l][None, :], mods[l], **lay)
        u = _in_proj(h1, w_l)

        o = _hg_pass(u, lb_all[l], None, rev=False, **kw)
        y_hg = _hg_pass(u, lb_all[l], o, rev=True, **kw)

        sbias = _pad_lanes(jnp.concatenate([ssd_dt_bias[l, 0], ssd_dt_bias[l, 1],
                                            ml_b_i[l, 0], ml_b_i[l, 1], ml_b_f[l, 0], ml_b_f[l, 1]]))
        arow = _pad_lanes(jnp.concatenate([-jnp.exp(ssd_a_log[l, 0]), -jnp.exp(ssd_a_log[l, 1])]))
        dsk = jnp.repeat(ssd_d[l], SSD_HEADDIM)[None, :]
        o = _ssd_pass(u, sbias, arow, ssd_conv_w[l], None, None, rev=False, **kw)
        y_ssd = _ssd_pass(u, sbias, arow, ssd_conv_w[l], (dsk, ssd_norm_g[l][None, :]), o, rev=True, **kw)

        w2 = _bf(jnp.stack([jnp.concatenate([_blockdiag_dense(lru_wa[l, dd]), _blockdiag_dense(lru_wx[l, dd])],
                                            axis=1) for dd in range(2)]))
        b2 = jnp.stack([jnp.concatenate([lru_ba[l, dd], lru_bx[l, dd]])[None, :] for dd in range(2)])
        crow = (-LRU_C * jax.nn.softplus(-lru_lambda[l]))[:, None, :]
        y_lru = _lru_mixer(u, (lru_conv_w[l], w2, b2, crow), nb=nb, seq=seq, nc=nc, nl=nl)

        o = _ml_pass(u, sbias, None, rev=False, **kw)
        y_ml = _ml_pass(u, sbias, o, rev=True, **kw)

        w_router = jnp.concatenate(
            [router_group_w[l], router_expert_w[l],
             jnp.zeros((d, LANE - N_EXPERT_GROUPS * (1 + EXPERTS_PER_GROUP)), F32)], axis=1)
        wr1 = _bf(w_router)
        wr12 = jnp.concatenate([wr1, _bf(w_router - wr1.astype(F32))], axis=1)
        xall, h2, logits = _out_proj((y_hg, y_ssd, y_lru, y_ml), _bf(w_out[l]).reshape(4, GW, d), xall,
                                     mods[l], norm2_g[l][None, :], wr12, **lay)
        y_moe = _moe(h2, logits, l, moe_w_gate, moe_w_up, moe_w_down)

    out = _final(xall, y_moe, mods[depth - 1], final_norm_g[None, :], **lay)
    return out.reshape(nb, seq, d)
```

```python
import functools
import math

import jax
import jax.numpy as jnp
from jax import lax
from jax.experimental import pallas as pl
from jax.experimental.pallas import tpu as pltpu

F32 = jnp.float32
BF16 = jnp.bfloat16

GRID_W = 64
HG_HEADS, HG_DK = 4, 128
SSD_HEADS, SSD_HEADDIM, SSD_GROUPS, SSD_STATE = 8, 64, 2, 128
LRU_C = 8.0
ML_HEADS, ML_DH = 4, 128
N_EXPERT_GROUPS, EXPERTS_PER_GROUP, TOP_K = 4, 8, 2
MOE_BLOCK = 256
NORM_EPS = 1e-6
NEG = -1e30

GW = 512
C_HG_Q, C_HG_I, C_HG_G, C_HG_ZF, C_HG_ZB = 0, 512, 1024, 1536, 2048
C_SSD_Z, C_SSD_XBC = 2560, 3072
C_LRU_GATE, C_LRU_XR = 4096, 4608
C_ML_Q, C_ML_K, C_ML_V, C_ML_O = 5120, 5632, 6144, 6656
C_SMALL = 7168
SM_DT, SM_IG, SM_FG = 0, 16, 24
NCOLS = 7680

LANE = 128
SUB = 8
VMEM_LIMIT = 48 * 1024 * 1024

T_CHUNK = 128
T_BLOCK = 2 * T_CHUNK


def _cparams(sem):
    return pltpu.CompilerParams(dimension_semantics=sem, vmem_limit_bytes=VMEM_LIMIT)


def _dot(a, b):
    return jnp.dot(a, b, preferred_element_type=F32)


def _dot_nt(a, b):
    return lax.dot_general(a, b, (((1,), (1,)), ((), ())), preferred_element_type=F32)


def _dot_tn(a, b):
    return lax.dot_general(a, b, (((0,), (0,)), ((), ())), preferred_element_type=F32)


def _bf(x):
    return x.astype(BF16)


def _sigmoid(x):
    return 0.5 * jnp.tanh(0.5 * x) + 0.5


def _softplus(x):
    return jnp.maximum(x, 0.0) + jnp.log1p(jnp.exp(-jnp.abs(x)))


def _split_dot(mask_bf, x):
    hi = _bf(x)
    r1 = x - hi.astype(F32)
    mid = _bf(r1)
    lo = _bf(r1 - mid.astype(F32))
    return _dot(mask_bf, hi) + _dot(mask_bf, mid) + _dot(mask_bf, lo)


def _rollb(x, n, rev):
    if n == 0:
        return x
    t = x.shape[0]
    return pltpu.roll(x, (t - n) if rev else n, 0)


def _iota2(t):
    return (lax.broadcasted_iota(jnp.int32, (t, t), 0), lax.broadcasted_iota(jnp.int32, (t, t), 1))


def _mod_kernel(c_ref, w_ref, b_ref, o_ref):
    c = c_ref[...]
    s = c * _sigmoid(c)
    o_ref[0] = jnp.dot(s, w_ref[0], precision=lax.Precision.HIGHEST,
                       preferred_element_type=F32) + b_ref[0]


def _modulation(c_all, ada_w, ada_b):
    depth, d, n6 = ada_w.shape
    tn = 1024
    return pl.pallas_call(
        _mod_kernel,
        grid=(depth, n6 // tn),
        in_specs=[pl.BlockSpec((SUB, d), lambda l, j: (0, 0)),
                  pl.BlockSpec((1, d, tn), lambda l, j: (l, 0, j)),
                  pl.BlockSpec((1, 1, tn), lambda l, j: (l, 0, j))],
        out_specs=pl.BlockSpec((1, SUB, tn), lambda l, j: (l, 0, j)),
        out_shape=jax.ShapeDtypeStruct((depth, SUB, n6), F32),
        compiler_params=_cparams(("parallel", "parallel")),
        name="adaln_modulation",
    )(c_all, ada_w, ada_b.reshape(depth, 1, n6))


def _rms_mod(x, g, shift, scale):
    y = x * lax.rsqrt(jnp.mean(x * x, axis=-1, keepdims=True) + NORM_EPS) * g
    return y * (1.0 + scale) + shift


def _mod_row_map(tm, n_lat, seq, b):
    def f(i, *_):
        return (jnp.where(i * tm < n_lat, (i * tm) // seq, b), 0, 0)
    return f


def _resnorm_kernel(*refs, residual):
    if residual:
        x_ref, y0_ref, y1_ref, wt_ref, pmod_ref, g_ref, mod_ref, xn_ref, h_ref = refs
        y = (wt_ref[:, 0:1] * y0_ref[...].astype(F32) + wt_ref[:, 1:2] * y1_ref[...].astype(F32))
        x = x_ref[...] + pmod_ref[0, 5:6, :] * y
        xn_ref[...] = x
    else:
        x_ref, g_ref, mod_ref, h_ref = refs
        x = x_ref[...]
    h_ref[...] = _bf(_rms_mod(x, g_ref[...], mod_ref[0, 0:1, :], mod_ref[0, 1:2, :]))


def _res_norm(x, ymoe, pmod, g1, mod, *, n_lat, seq, b):
    ntot, d = x.shape
    tm = 512
    residual = ymoe is not None
    row = lambda i: (i, 0)
    mrow = _mod_row_map(tm, n_lat, seq, b)
    in_specs = [pl.BlockSpec((tm, d), row)]
    args = [x]
    if residual:
        y0, y1, wts = ymoe
        in_specs += [pl.BlockSpec((tm, d), row), pl.BlockSpec((tm, d), row),
                     pl.BlockSpec((tm, TOP_K), row), pl.BlockSpec((1, 6, d), mrow)]
        args += [y0, y1, wts, pmod]
    in_specs += [pl.BlockSpec((1, d), lambda i: (0, 0)), pl.BlockSpec((1, 6, d), mrow)]
    args += [g1, mod]
    out_specs = [pl.BlockSpec((tm, d), row)]
    out_shape = [jax.ShapeDtypeStruct((ntot, d), BF16)]
    if residual:
        out_specs = [pl.BlockSpec((tm, d), row)] + out_specs
        out_shape = [jax.ShapeDtypeStruct((ntot, d), F32)] + out_shape
    res = pl.pallas_call(
        functools.partial(_resnorm_kernel, residual=residual),
        grid=(ntot // tm,),
        in_specs=in_specs, out_specs=out_specs, out_shape=out_shape,
        compiler_params=_cparams(("parallel",)),
        name="residual_norm1",
    )(*args)
    return (res[0], res[1]) if residual else (x, res[0])


def _inproj_kernel(h_ref, w_ref, u_ref):
    u_ref[...] = _dot(h_ref[...], w_ref[0])


def _in_proj(h, w_tiles):
    ntot, d = h.shape
    nt, _, tn = w_tiles.shape
    tm = 1536
    assert ntot % tm == 0
    return pl.pallas_call(
        _inproj_kernel,
        grid=(ntot // tm, nt),
        in_specs=[pl.BlockSpec((tm, d), lambda i, j: (i, 0)),
                  pl.BlockSpec((1, d, tn), lambda i, j: (j, 0, 0))],
        out_specs=pl.BlockSpec((tm, tn), lambda i, j: (i, j)),
        out_shape=jax.ShapeDtypeStruct((ntot, nt * tn), F32),
        compiler_params=_cparams(("parallel", "arbitrary")),
        name="in_proj",
    )(h, w_tiles)


def _chunk_nat(i, rev, nc, nl):
    is_ctx = i < nc
    if rev:
        c = jnp.where(is_ctx, nc - 1 - i, nl - 1 - (i - nc))
    else:
        c = jnp.where(is_ctx, i, i - nc)
    return is_ctx, c


def _row_block(b, i, *, rev, nc, nl, nb):
    is_ctx, c = _chunk_nat(i, rev, nc, nl)
    return jnp.where(is_ctx, nb * nl + b * nc + c, b * nl + c)


def _col_spec(t, width, col, rb):
    cb = col // width
    return pl.BlockSpec((t, width), lambda b, i: (rb(b, i), cb))


def _hg_kernel(*refs, rev, finish):
    if finish:
        q_ref, v_ref, z_ref, lb_ref, g_ref, of_ref, o_ref, st_scr = refs
    else:
        q_ref, v_ref, z_ref, lb_ref, o_ref, st_scr = refs
    t = T_CHUNK
    dk = HG_DK
    w = q_ref.shape[1]

    @pl.when(pl.program_id(1) == 0)
    def _():
        st_scr[...] = jnp.zeros_like(st_scr)

    lb = lb_ref[0]
    row, col = _iota2(t)
    valid = (col >= row) if rev else (col <= row)
    lvl = jnp.where(valid, 31 - lax.clz(row ^ col), -2)
    valid_bf = _bf(valid.astype(F32))
    tpos = lax.broadcasted_iota(jnp.int32, (t, w), 0)
    r8 = lax.broadcasted_iota(jnp.int32, (t // SUB, SUB, w), 1)
    for rs in _sub_chunks(q_ref.shape[0], rev):
        _hg_chunk(rs, rev, finish, refs, lb, valid_bf, lvl, tpos, r8)


def _sub_chunks(block_rows, rev):
    n = block_rows // T_CHUNK
    order = range(n - 1, -1, -1) if rev else range(n)
    return [slice(i * T_CHUNK, (i + 1) * T_CHUNK) for i in order]


def _hg_chunk(rs, rev, finish, refs, lb, valid_bf, lvl, tpos, r8):
    if finish:
        q_ref, v_ref, z_ref, lb_ref, g_ref, of_ref, o_ref, st_scr = refs
    else:
        q_ref, v_ref, z_ref, lb_ref, o_ref, st_scr = refs
    t = T_CHUNK
    dk = HG_DK
    w = q_ref.shape[1]
    sig = _sigmoid(z_ref[rs, :])
    f = lb + (1.0 - lb) * sig
    lf = jnp.log(f)
    k = (1.0 - lb) * (1.0 - sig)
    q = q_ref[rs, :]
    v = v_ref[rs, :]

    b = _split_dot(valid_bf, lf)
    total = jnp.sum(lf, axis=0, keepdims=True)

    b8 = b.reshape(t // SUB, SUB, w)
    ops = [(-1, _bf(q), _bf(k))]
    h = 1
    while h < t:
        blk = 2 * h
        lg = int(math.log2(h))
        is_q = ((tpos & h) == 0) if rev else ((tpos & h) != 0)
        if h == 1:
            e = f
        else:
            off = h if rev else h - 1
            if blk <= SUB:
                ref = b8[:, off:off + 1, :]
                for sb in range(1, SUB // blk):
                    ref = jnp.where(r8 >= sb * blk, b8[:, sb * blk + off:sb * blk + off + 1, :], ref)
                ref = jnp.broadcast_to(ref, b8.shape).reshape(t, w)
            else:
                b3 = b.reshape(t // blk, blk, w)
                ref = jnp.broadcast_to(b3[:, off:off + 1, :], b3.shape).reshape(t, w)
            e = jnp.exp(-jnp.abs(b - ref))
        if h == 1:
            m = _bf(jnp.where(is_q, q * e, k))
        else:
            m = _bf(jnp.where(is_q, q, k) * e)
        ops.append((lg, m, m))
        h = blk

    qb = _bf(q * jnp.exp(b))
    kw = _bf(k * jnp.exp(total - b))
    vb = _bf(v)
    dec_tot = jnp.exp(total)

    masks = [lvl == lg for lg, _, _ in ops]
    outs = []
    for hd in range(HG_HEADS):
        sl = slice(hd * dk, (hd + 1) * dk)
        a = jnp.zeros((t, t), F32)
        for msk, (lg, mq, mk) in zip(masks, ops):
            a = jnp.where(msk, _dot_nt(mq[:, sl], mk[:, sl]), a)
        st = st_scr[hd]
        o = _dot(_bf(a), vb[:, sl]) + _dot_nt(qb[:, sl], _bf(st))
        st_scr[hd] = st * dec_tot[:, sl] + _dot_tn(vb[:, sl], kw[:, sl])
        outs.append(o)

    if finish:
        for hd in range(HG_HEADS):
            sl = slice(hd * dk, (hd + 1) * dk)
            tot = outs[hd] + of_ref[rs, sl]
            y = tot * lax.rsqrt(jnp.mean(tot * tot, axis=-1, keepdims=True) + NORM_EPS)
            g = g_ref[rs, sl]
            o_ref[rs, sl] = (y * (g * _sigmoid(g))).astype(o_ref.dtype)
    else:
        for hd in range(HG_HEADS):
            o_ref[rs, hd * dk:(hd + 1) * dk] = outs[hd]


def _hg_pass(u, lb, o_f, *, rev, nb, nc, nl):
    t = T_BLOCK
    ntot = u.shape[0]
    finish = o_f is not None
    rb = functools.partial(_row_block, rev=rev, nc=nc, nl=nl, nb=nb)
    d = 1 if rev else 0
    in_specs = [_col_spec(t, GW, C_HG_Q, rb), _col_spec(t, GW, C_HG_I, rb),
                _col_spec(t, GW, C_HG_ZB if rev else C_HG_ZF, rb),
                pl.BlockSpec((1, 1, GW), lambda b, i: (d, 0, 0))]
    args = [u, u, u, lb]
    if finish:
        in_specs += [_col_spec(t, GW, C_HG_G, rb), _col_spec(t, GW, 0, rb)]
        args += [u, o_f]
    return pl.pallas_call(
        functools.partial(_hg_kernel, rev=rev, finish=finish),
        grid=(nb, nc + nl),
        in_specs=in_specs,
        out_specs=_col_spec(t, GW, 0, rb),
        out_shape=jax.ShapeDtypeStruct((ntot, GW), BF16 if finish else F32),
        scratch_shapes=[pltpu.VMEM((HG_HEADS, HG_DK, HG_DK), F32)],
        compiler_params=_cparams(("parallel", "arbitrary")),
        name="hgrn2_bwd_finish" if rev else "hgrn2_fwd",
    )(*args)


def _ml_kernel(*refs, rev, finish):
    ct_scr, m_scr = refs[-2:]
    t = T_CHUNK

    @pl.when(pl.program_id(1) == 0)
    def _():
        ct_scr[...] = jnp.zeros_like(ct_scr)
        m_scr[...] = jnp.full(m_scr.shape, NEG, F32)

    row, col = _iota2(t)
    valid = (col >= row) if rev else (col <= row)
    valid_bf = _bf(valid.astype(F32))
    for rs in _sub_chunks(refs[0].shape[0], rev):
        _ml_chunk(rs, rev, finish, refs, valid, valid_bf)


def _ml_chunk(rs, rev, finish, refs, valid, valid_bf):
    if finish:
        q_ref, k_ref, v_ref, sm_ref, sb_ref, og_ref, of_ref, o_ref, ct_scr, m_scr = refs
    else:
        q_ref, k_ref, v_ref, sm_ref, sb_ref, o_ref, ct_scr, m_scr = refs
    t = T_CHUNK
    dh = ML_DH
    d = 1 if rev else 0

    sm = sm_ref[rs, :] + sb_ref[...]
    lsg = jnp.minimum(sm, 0.0) - jnp.log1p(jnp.exp(-jnp.abs(sm)))
    bcol = _split_dot(valid_bf, lsg)
    tot_row = jnp.sum(lsg, axis=0, keepdims=True)
    b_t = bcol.T
    sm_t = sm.T

    ks = k_ref[rs, :] * (dh ** -0.5)
    qb, kb, vb = _bf(q_ref[rs, :]), _bf(ks), _bf(v_ref[rs, :])
    ones = jnp.ones((t, dh), BF16)

    for hd in range(ML_HEADS):
        sl = slice(hd * dh, (hd + 1) * dh)
        ci = SM_IG + ML_HEADS * d + hd
        cf = SM_FG + ML_HEADS * d + hd
        bc = bcol[:, cf:cf + 1]
        br = b_t[cf:cf + 1, :]
        lir = sm_t[ci:ci + 1, :]
        lic = sm[:, ci:ci + 1]
        m_prev = m_scr[hd:hd + 1, 0:1]
        dmat = jnp.where(valid, bc - br + lir, NEG)
        g = bc + m_prev
        mt = jnp.maximum(g, jnp.max(dmat, axis=-1, keepdims=True))
        w_intra = jnp.exp(dmat - mt)
        w_inter = jnp.exp(g - mt)
        s = _dot_nt(qb[:, sl], kb[:, sl]) * w_intra
        vaug = jnp.concatenate([vb[:, sl], ones], axis=-1)
        ct = ct_scr[hd]
        nd = _dot(_bf(s), vaug) + w_inter * _dot_nt(qb[:, sl], _bf(ct))
        hout = nd[:, :dh] / jnp.maximum(jnp.abs(nd[:, dh:]), jnp.exp(-mt))
        tot = tot_row[:, cf:cf + 1]
        dl = tot - bc + lic
        m_new = jnp.maximum(tot + m_prev, jnp.max(dl, axis=0, keepdims=True))
        ws = jnp.exp(dl - m_new)
        wc = jnp.exp(tot + m_prev - m_new)
        ct_scr[hd] = wc * ct + _dot_tn(vaug, _bf(ks[:, sl] * ws))
        m_scr[hd:hd + 1, :] = jnp.broadcast_to(m_new, (1, m_scr.shape[1]))
        if finish:
            tot_o = hout + of_ref[rs, sl]
            y = tot_o * lax.rsqrt(jnp.mean(tot_o * tot_o, axis=-1, keepdims=True) + NORM_EPS)
            o_ref[rs, sl] = (y * _sigmoid(og_ref[rs, sl])).astype(o_ref.dtype)
        else:
            o_ref[rs, sl] = hout


def _ml_pass(u, sbias, o_f, *, rev, nb, nc, nl):
    t = T_BLOCK
    ntot = u.shape[0]
    finish = o_f is not None
    rb = functools.partial(_row_block, rev=rev, nc=nc, nl=nl, nb=nb)
    in_specs = [_col_spec(t, GW, C_ML_Q, rb), _col_spec(t, GW, C_ML_K, rb),
                _col_spec(t, GW, C_ML_V, rb), _col_spec(t, LANE, C_SMALL, rb),
                pl.BlockSpec((1, LANE), lambda b, i: (0, 0))]
    args = [u, u, u, u, sbias]
    if finish:
        in_specs += [_col_spec(t, GW, C_ML_O, rb), _col_spec(t, GW, 0, rb)]
        args += [u, o_f]
    return pl.pallas_call(
        functools.partial(_ml_kernel, rev=rev, finish=finish),
        grid=(nb, nc + nl),
        in_specs=in_specs,
        out_specs=_col_spec(t, GW, 0, rb),
        out_shape=jax.ShapeDtypeStruct((ntot, GW), BF16 if finish else F32),
        scratch_shapes=[pltpu.VMEM((ML_HEADS, 2 * ML_DH, ML_DH), F32),
                        pltpu.VMEM((SUB, LANE), F32)],
        compiler_params=_cparams(("parallel", "arbitrary")),
        name="mlstm_bwd_finish" if rev else "mlstm_fwd",
    )(*args)


def _conv4(xe_scr, x, prev8, next8, w, first, last):
    t = x.shape[0]
    xe_scr[0:SUB, :] = jnp.where(first, 0.0, prev8)
    xe_scr[SUB:SUB + t, :] = x
    xe_scr[SUB + t:SUB + t + SUB, :] = jnp.where(last, 0.0, next8)
    out = w[0:1, :] * xe_scr[SUB - 2:SUB - 2 + t, :]
    for j in range(1, 4):
        out = out + w[j:j + 1, :] * xe_scr[SUB - 2 + j:SUB - 2 + j + t, :]
    return out


def _ssd_kernel(*refs, rev, finish, nc, nl):
    x_ref, xp_ref, xn_ref = refs[:3]
    cw_ref = refs[6]
    st_scr, xe_scr, xc_scr = refs[-3:]
    t = T_CHUNK
    i = pl.program_id(1)

    @pl.when(i == 0)
    def _():
        st_scr[...] = jnp.zeros_like(st_scr)

    is_ctx, c = _chunk_nat(i, rev, nc, nl)
    first = c == 0
    last = c == jnp.where(is_ctx, nc, nl) - 1
    xc = _conv4(xe_scr, x_ref[...], xp_ref[...], xn_ref[...], cw_ref[...], first, last)
    xc_scr[...] = xc * _sigmoid(xc)

    row, col = _iota2(t)
    valid = (col >= row) if rev else (col <= row)
    valid_bf = _bf(valid.astype(F32))
    for rs in _sub_chunks(x_ref.shape[0], rev):
        _ssd_chunk(rs, rev, finish, refs, valid, valid_bf)


def _ssd_chunk(rs, rev, finish, refs, valid, valid_bf):
    if finish:
        (x_ref, xp_ref, xn_ref, sm_ref, sb_ref, arow_ref, cw_ref,
         z_ref, of_ref, dsk_ref, ng_ref, o_ref, st_scr, xe_scr, xc_scr) = refs
    else:
        x_ref, xp_ref, xn_ref, sm_ref, sb_ref, arow_ref, cw_ref, o_ref, st_scr, xe_scr, xc_scr = refs
    p = SSD_HEADDIM
    n = SSD_STATE
    d = 1 if rev else 0
    xs = xc_scr[rs, :SSD_HEADS * p]
    bm = xc_scr[rs, SSD_HEADS * p:SSD_HEADS * p + SSD_GROUPS * n]
    cm = xc_scr[rs, SSD_HEADS * p + SSD_GROUPS * n:]

    dl = _softplus(sm_ref[rs, :] + sb_ref[...])
    lfm = dl * arow_ref[...]
    bcol = _split_dot(valid_bf, lfm)
    tot_row = jnp.sum(lfm, axis=0, keepdims=True)
    b_t = bcol.T
    dl_t = dl.T

    xsb = _bf(xs)
    rep = SSD_HEADS // SSD_GROUPS
    outs = []
    for gi in range(SSD_GROUPS):
        bg = bm[:, gi * n:(gi + 1) * n]
        cgb = _bf(cm[:, gi * n:(gi + 1) * n])
        gmat = _dot_nt(cgb, _bf(bg))
        for hh in range(rep):
            hd = gi * rep + hh
            cc = SM_DT + SSD_HEADS * d + hd
            bc = bcol[:, cc:cc + 1]
            br = b_t[cc:cc + 1, :]
            dr = dl_t[cc:cc + 1, :]
            dcol = dl[:, cc:cc + 1]
            tot = tot_row[:, cc:cc + 1]
            att = gmat * jnp.exp(jnp.where(valid, bc - br, NEG)) * dr
            xh = xsb[:, hd * p:(hd + 1) * p]
            st = st_scr[hd]
            o = _dot(_bf(att), xh) + jnp.exp(bc) * _dot_nt(cgb, _bf(st))
            wk = dcol * jnp.exp(tot - bc)
            st_scr[hd] = jnp.exp(tot) * st + _dot_tn(xh, _bf(bg * wk))
            outs.append(o)
    o_all = jnp.concatenate(outs, axis=-1)

    if finish:
        y = (o_all + of_ref[rs, :] + dsk_ref[...] * xs)
        z = z_ref[rs, :]
        y = y * (z * _sigmoid(z))
        gw = SSD_HEADS * p // SSD_GROUPS
        for gi in range(SSD_GROUPS):
            sl = slice(gi * gw, (gi + 1) * gw)
            yg = y[:, sl]
            yg = yg * lax.rsqrt(jnp.mean(yg * yg, axis=-1, keepdims=True) + NORM_EPS)
            o_ref[rs, sl] = (yg * ng_ref[:, sl]).astype(o_ref.dtype)
    else:
        o_ref[rs, :] = o_all


def _ssd_pass(u, sbias, arow, conv_w, z_extra, o_f, *, rev, nb, nc, nl):
    t = T_BLOCK
    ntot = u.shape[0]
    finish = o_f is not None
    rb = functools.partial(_row_block, rev=rev, nc=nc, nl=nl, nb=nb)
    r8 = t // SUB
    nblk8 = ntot // SUB
    cb = C_SSD_XBC // 1024

    def prev_map(b, i):
        return (jnp.maximum(rb(b, i) * r8 - 1, 0), cb)

    def next_map(b, i):
        return (jnp.minimum((rb(b, i) + 1) * r8, nblk8 - 1), cb)

    const2 = lambda b, i: (0, 0)
    in_specs = [_col_spec(t, 1024, C_SSD_XBC, rb),
                pl.BlockSpec((SUB, 1024), prev_map), pl.BlockSpec((SUB, 1024), next_map),
                _col_spec(t, LANE, C_SMALL, rb),
                pl.BlockSpec((1, LANE), const2), pl.BlockSpec((1, LANE), const2),
                pl.BlockSpec((4, 1024), const2)]
    args = [u, u, u, u, sbias, arow, conv_w]
    if finish:
        dsk, ng = z_extra
        in_specs += [_col_spec(t, GW, C_SSD_Z, rb), _col_spec(t, GW, 0, rb),
                     pl.BlockSpec((1, GW), const2), pl.BlockSpec((1, GW), const2)]
        args += [u, o_f, dsk, ng]
    return pl.pallas_call(
        functools.partial(_ssd_kernel, rev=rev, finish=finish, nc=nc, nl=nl),
        grid=(nb, nc + nl),
        in_specs=in_specs,
        out_specs=_col_spec(t, GW, 0, rb),
        out_shape=jax.ShapeDtypeStruct((ntot, GW), BF16 if finish else F32),
        scratch_shapes=[pltpu.VMEM((SSD_HEADS, SSD_HEADDIM, SSD_STATE), F32),
                        pltpu.VMEM((t + 2 * SUB, 1024), F32), pltpu.VMEM((t, 1024), F32)],
        compiler_params=_cparams(("parallel", "arbitrary")),
        name="ssd_bwd_finish" if rev else "ssd_fwd",
    )(*args)


def _lru_scan(xr, w2, b2, crow, h, rev):
    t, w = xr.shape
    ng = t // SUB
    pre = _dot(_bf(xr), w2) + b2
    r = _sigmoid(pre[:, :w])
    ig = _sigmoid(pre[:, w:])
    log_a = crow * r
    a = jnp.exp(log_a)
    y = jnp.tanh(-log_a) * (1.0 + a * a)
    bb = jnp.where(y > 0.0, y * lax.rsqrt(y), 0.0) * (ig * xr)

    a3 = a.reshape(ng, SUB, w)
    b3 = bb.reshape(ng, SUB, w)
    r8 = lax.broadcasted_iota(jnp.int32, (ng, SUB, w), 1)
    pos = (SUB - 1 - r8) if rev else r8
    step = 1
    while step < SUB:
        sh = (SUB - step) if rev else step
        a_s = pltpu.roll(a3, sh, 1)
        b_s = pltpu.roll(b3, sh, 1)
        ok = pos >= step
        b3 = jnp.where(ok, a3 * b_s + b3, b3)
        a3 = jnp.where(ok, a3 * a_s, a3)
        step *= 2
    groups = [None] * ng
    for g in (range(ng - 1, -1, -1) if rev else range(ng)):
        hg = b3[g] + a3[g] * h
        groups[g] = hg
        h = hg[0:1, :] if rev else hg[SUB - 1:SUB, :]
    return jnp.concatenate(groups, axis=0), h


def _gelu_tanh(g):
    return 0.5 * g * (1.0 + jnp.tanh(math.sqrt(2.0 / math.pi) * (g + 0.044715 * (g * g * g))))


LRU_COLS_PER_STEP = SUB


def _lru_kernel(*refs, rev, finish, latent, nsteps):
    if finish:
        (x_ref, xp_ref, xn_ref, cw_ref, w2_ref, b2_ref, crow_ref, h0_ref,
         gate_ref, of_ref, o_ref, ht_ref, h_scr, xe_scr) = refs
    else:
        x_ref, xp_ref, xn_ref, cw_ref, w2_ref, b2_ref, crow_ref, h0_ref, o_ref, ht_ref, h_scr, xe_scr = refs
    i = pl.program_id(1)

    @pl.when(i == 0)
    def _():
        h_scr[...] = h0_ref[0]

    c = (nsteps - 1 - i) if rev else i
    cw, w2, b2, crow = cw_ref[...], w2_ref[0], b2_ref[0], crow_ref[0]
    h = h_scr[...]
    if not latent:
        xr = _conv4(xe_scr, x_ref[...], xp_ref[...], xn_ref[...], cw, c == 0, c == nsteps - 1)
        hs, h = _lru_scan(xr, w2, b2, crow, h, rev)
        if finish:
            o_ref[...] = ((hs + of_ref[...]) * _gelu_tanh(gate_ref[...])).astype(o_ref.dtype)
        else:
            o_ref[...] = hs
    else:
        rows, ncol = x_ref.shape[0], x_ref.shape[1]
        for cl in (range(ncol - 1, -1, -1) if rev else range(ncol)):
            prev8 = x_ref[rows - SUB:rows, cl - 1, :] if cl > 0 else xp_ref[:, ncol - 1, :]
            next8 = x_ref[0:SUB, cl + 1, :] if cl < ncol - 1 else xn_ref[:, 0, :]
            first = (c == 0) if cl == 0 else False
            last = (c == nsteps - 1) if cl == ncol - 1 else False
            xr = _conv4(xe_scr, x_ref[:, cl, :], prev8, next8, cw, first, last)
            hs, h = _lru_scan(xr, w2, b2, crow, h, rev)
            if finish:
                o_ref[:, cl, :] = ((hs + of_ref[:, cl, :]) * _gelu_tanh(gate_ref[:, cl, :])).astype(o_ref.dtype)
            else:
                o_ref[:, cl, :] = hs
    h_scr[...] = h

    @pl.when(i == nsteps - 1)
    def _():
        ht_ref[0] = h


def _lru_pass(arr, views, consts, h0, o_f, *, rev, nb, nsteps, latent, out_shape, rows=None):
    finish = o_f is not None
    d = 1 if rev else 0
    main_map, prev_map, next_map, gate_map, out_map = views
    cw, w2, b2, crow = consts
    if latent:
        blk, halo = (rows, LRU_COLS_PER_STEP, GW), (SUB, LRU_COLS_PER_STEP, GW)
        t = rows
    else:
        blk, halo = (T_BLOCK, GW), (SUB, GW)
        t = T_BLOCK
    const2 = lambda b, i: (0, 0)
    dsel = lambda b, i: (d, 0, 0)
    hsel = lambda b, i: (b, 0, 0)
    in_specs = [pl.BlockSpec(blk, main_map), pl.BlockSpec(halo, prev_map), pl.BlockSpec(halo, next_map),
                pl.BlockSpec((4, GW), const2),
                pl.BlockSpec((1, GW, 2 * GW), dsel), pl.BlockSpec((1, 1, 2 * GW), dsel),
                pl.BlockSpec((1, 1, GW), dsel), pl.BlockSpec((1, 1, GW), hsel)]
    args = [arr, arr, arr, cw, w2, b2, crow, h0]
    if finish:
        in_specs += [pl.BlockSpec(blk, gate_map), pl.BlockSpec(blk, out_map)]
        args += [arr, o_f]
    seg = "lat" if latent else "ctx"
    return pl.pallas_call(
        functools.partial(_lru_kernel, rev=rev, finish=finish, latent=latent, nsteps=nsteps),
        grid=(nb, nsteps),
        in_specs=in_specs,
        out_specs=[pl.BlockSpec(blk, out_map), pl.BlockSpec((1, 1, GW), hsel)],
        out_shape=[jax.ShapeDtypeStruct(out_shape, BF16 if (finish and not latent) else F32),
                   jax.ShapeDtypeStruct((nb, 1, GW), F32)],
        scratch_shapes=[pltpu.VMEM((1, GW), F32), pltpu.VMEM((t + 2 * SUB, GW), F32)],
        compiler_params=_cparams(("parallel", "arbitrary")),
        name=f"rglru_{seg}_bwd_finish" if rev else f"rglru_{seg}_fwd",
    )(*args)


def _lru_mixer(u, consts, *, nb, seq, ctx):
    ntot, ncols = u.shape
    n_lat = nb * seq
    rows = seq // GRID_W
    ncg = GRID_W // LRU_COLS_PER_STEP
    ncb = ctx // T_BLOCK
    xr_cb, gate_cb = C_LRU_XR // GW, C_LRU_GATE // GW
    u3 = u.reshape(ntot // GRID_W, GRID_W, ncols)
    r8, c8 = rows // SUB, T_BLOCK // SUB
    base, base8 = n_lat // T_BLOCK, n_lat // SUB
    h = jnp.zeros((nb, 1, GW), F32)
    o_c = o_l = None
    for rev in (False, True):
        nat_c = (lambda i: ncb - 1 - i) if rev else (lambda i: i)
        nat_l = (lambda i: ncg - 1 - i) if rev else (lambda i: i)
        cviews = (lambda b, i: (base + b * ncb + nat_c(i), xr_cb),
                  lambda b, i: (jnp.maximum(base8 + (b * ncb + nat_c(i)) * c8 - 1, 0), xr_cb),
                  lambda b, i: (jnp.minimum(base8 + (b * ncb + nat_c(i) + 1) * c8, ntot // SUB - 1), xr_cb),
                  lambda b, i: (base + b * ncb + nat_c(i), gate_cb),
                  lambda b, i: (b * ncb + nat_c(i), 0))
        o_c, h_c = _lru_pass(u, cviews, consts, h, o_c, rev=rev, nb=nb, nsteps=ncb, latent=False,
                             out_shape=(nb * ctx, GW))
        lviews = (lambda b, i: (b, nat_l(i), xr_cb),
                  lambda b, i: (b * r8 + r8 - 1, jnp.maximum(nat_l(i) - 1, 0), xr_cb),
                  lambda b, i: (b * r8, jnp.minimum(nat_l(i) + 1, ncg - 1), xr_cb),
                  lambda b, i: (b, nat_l(i), gate_cb),
                  lambda b, i: (b, nat_l(i), 0))
        o_l, _ = _lru_pass(u3, lviews, consts, h_c, o_l, rev=rev, nb=nb, nsteps=ncg, latent=True,
                           out_shape=(n_lat // GRID_W, GRID_W, GW), rows=rows)
    return jnp.concatenate([_bf(o_l).reshape(n_lat, GW), o_c], axis=0)


def _outproj_kernel(yh_ref, ys_ref, yl_ref, ym_ref, w_ref, x_ref, mod_ref, g_ref, wr_ref,
                    xn_ref, h_ref, lg_ref):
    acc = _dot(yh_ref[...], w_ref[0])
    acc = acc + _dot(ys_ref[...], w_ref[1])
    acc = acc + _dot(yl_ref[...], w_ref[2])
    acc = acc + _dot(ym_ref[...], w_ref[3])
    xn = x_ref[...] + mod_ref[0, 2:3, :] * acc
    xn_ref[...] = xn
    h = _rms_mod(xn, g_ref[...], mod_ref[0, 3:4, :], mod_ref[0, 4:5, :])
    h1 = _bf(h)
    tm, d = h.shape
    bits = lax.bitcast_convert_type(h1.astype(F32), jnp.uint32)
    packed = (bits[:, :d // 2] >> 16) | (bits[:, d // 2:] & jnp.uint32(0xFFFF0000))
    nch = d // 2 // LANE
    for j in range(nch):
        h_ref[pl.ds(j, tm, stride=nch), :] = packed[:, j * LANE:(j + 1) * LANE]
    h2 = _bf(h - h1.astype(F32))
    p = _dot(h1, wr_ref[...])
    lg_ref[...] = p[:, :LANE] + p[:, LANE:] + _dot(h2, wr_ref[:, :LANE])


def _out_proj(ys, w_out4, x, mod, g2, w_router, *, n_lat, seq, b):
    ntot, d = x.shape
    tm = 512
    row = lambda i: (i, 0)
    mrow = _mod_row_map(tm, n_lat, seq, b)
    in_specs = [pl.BlockSpec((tm, GW), row)] * 4 + [
        pl.BlockSpec((4, GW, d), lambda i: (0, 0, 0)),
        pl.BlockSpec((tm, d), row), pl.BlockSpec((1, 6, d), mrow),
        pl.BlockSpec((1, d), lambda i: (0, 0)), pl.BlockSpec((d, 2 * LANE), lambda i: (0, 0))]
    return pl.pallas_call(
        _outproj_kernel,
        grid=(ntot // tm,),
        in_specs=in_specs,
        out_specs=[pl.BlockSpec((tm, d), row), pl.BlockSpec((tm * (d // 2 // LANE), LANE), row),
                   pl.BlockSpec((tm, LANE), row)],
        out_shape=[jax.ShapeDtypeStruct((ntot, d), F32),
                   jax.ShapeDtypeStruct((ntot * (d // 2 // LANE), LANE), jnp.uint32),
                   jax.ShapeDtypeStruct((ntot, LANE), F32)],
        compiler_params=_cparams(("parallel",)),
        name="out_proj_norm2_router",
    )(*ys, w_out4, x, mod, g2, w_router)


def _row_copy(h_hbm, xbuf, sem, tok, slot, r):
    nch = xbuf.shape[1] // MOE_BLOCK
    return pltpu.make_async_copy(h_hbm.at[pl.ds(tok * nch, nch), :], xbuf.at[slot, pl.ds(r * nch, nch), :],
                                 sem.at[slot])


def _moe_kernel(be_ref, tok_ref, na_ref, h_hbm, wg_ref, wu_ref, wd_ref, o_ref, xbuf, sem, wgu_scr, wd_scr):
    i = pl.program_id(0)
    n_active = na_ref[0]
    de = wg_ref.shape[-1]
    slot = i % 2
    nch = xbuf.shape[1] // MOE_BLOCK

    def start_gather(blk, s):
        for r in range(MOE_BLOCK):
            _row_copy(h_hbm, xbuf, sem, tok_ref[blk * MOE_BLOCK + r], s, r).start()

    @pl.when(i == 0)
    def _():
        start_gather(0, 0)

    @pl.when(i < n_active)
    def _():
        @pl.when(jnp.logical_or(i == 0, be_ref[i] != be_ref[jnp.maximum(i - 1, 0)]))
        def _():
            wgu_scr[:, :de] = _bf(wg_ref[0])
            wgu_scr[:, de:] = _bf(wu_ref[0])
            wd_scr[...] = _bf(wd_ref[0])

        for r in range(MOE_BLOCK):
            _row_copy(h_hbm, xbuf, sem, 0, slot, r).wait()

        @pl.when(i + 1 < n_active)
        def _():
            start_gather(i + 1, 1 - slot)

        lo, hi = [], []
        for j in range(nch):
            wv = xbuf[slot, pl.ds(j, MOE_BLOCK, stride=nch), :]
            lo.append(_bf(lax.bitcast_convert_type(wv << 16, F32)))
            hi.append(_bf(lax.bitcast_convert_type(wv & jnp.uint32(0xFFFF0000), F32)))
        xb = jnp.concatenate(lo + hi, axis=-1)
        gu = _dot(xb, wgu_scr[...])
        g = gu[:, :de]
        hid = (g * _sigmoid(g)) * gu[:, de:]
        o_ref[...] = _bf(_dot(_bf(hid), wd_scr[...]))

    @pl.when(i >= n_active)
    def _():
        o_ref[...] = jnp.zeros_like(o_ref)


def _moe_blocks(blk_exp, slot_tok, n_active, h, wg, wu, wd):
    nslot = slot_tok.shape[0]
    d = wg.shape[1]
    nch = d // 2 // LANE
    de = wg.shape[-1]
    nblk = nslot // MOE_BLOCK
    return pl.pallas_call(
        _moe_kernel,
        grid_spec=pltpu.PrefetchScalarGridSpec(
            num_scalar_prefetch=3,
            grid=(nblk,),
            in_specs=[pl.BlockSpec(memory_space=pl.ANY),
                      pl.BlockSpec((1, d, de), lambda i, be, tk, na: (be[i], 0, 0)),
                      pl.BlockSpec((1, d, de), lambda i, be, tk, na: (be[i], 0, 0)),
                      pl.BlockSpec((1, de, d), lambda i, be, tk, na: (be[i], 0, 0))],
            out_specs=pl.BlockSpec((MOE_BLOCK, d), lambda i, be, tk, na: (i, 0)),
            scratch_shapes=[pltpu.VMEM((2, MOE_BLOCK * nch, LANE), jnp.uint32), pltpu.SemaphoreType.DMA((2,)),
                            pltpu.VMEM((d, 2 * de), BF16), pltpu.VMEM((de, d), BF16)]),
        out_shape=jax.ShapeDtypeStruct((nslot, d), BF16),
        compiler_params=_cparams(("arbitrary",)),
        name="moe_expert_blocks",
    )(blk_exp, slot_tok, n_active, h, wg, wu, wd)


def _route(logits):
    n_tok = logits.shape[0]
    n_exp = N_EXPERT_GROUPS * EXPERTS_PER_GROUP
    g_logits = logits[:, :N_EXPERT_GROUPS]
    g_prob = jax.nn.softmax(g_logits, axis=-1)
    g_sel = jnp.argmax(g_logits, axis=-1)
    g_hot = g_sel[:, None] == jnp.arange(N_EXPERT_GROUPS)[None, :]
    e_logits = logits[:, N_EXPERT_GROUPS:N_EXPERT_GROUPS + n_exp].reshape(
        n_tok, N_EXPERT_GROUPS, EXPERTS_PER_GROUP)
    within = jnp.sum(jnp.where(g_hot[:, :, None], e_logits, 0.0), axis=1)
    top_v, top_i = lax.top_k(within, TOP_K)
    wts = jax.nn.softmax(top_v, axis=-1) * jnp.sum(jnp.where(g_hot, g_prob, 0.0), axis=1, keepdims=True)
    eid = (g_sel[:, None] * EXPERTS_PER_GROUP + top_i).astype(jnp.int32)

    n_assign = n_tok * TOP_K
    flat_e = eid.reshape(-1)
    onehot = (flat_e[:, None] == jnp.arange(n_exp, dtype=jnp.int32)[None, :]).astype(jnp.int32)
    csum = jnp.cumsum(onehot, axis=0)
    counts = csum[-1]
    padded = (counts + MOE_BLOCK - 1) // MOE_BLOCK * MOE_BLOCK
    pend = jnp.cumsum(padded)
    pstart = pend - padded
    dest = jnp.sum(onehot * (csum - 1 + pstart[None, :]), axis=1)
    n_blk = -(-n_assign // MOE_BLOCK) + n_exp
    tok = jnp.arange(n_assign, dtype=jnp.int32) // TOP_K
    slot_tok = jnp.zeros((n_blk * MOE_BLOCK,), jnp.int32).at[dest].set(tok)
    blk_start = jnp.arange(n_blk, dtype=jnp.int32) * MOE_BLOCK
    blk_exp = jnp.minimum(jnp.sum(blk_start[:, None] >= pend[None, :], axis=1), n_exp - 1).astype(jnp.int32)
    n_active = (pend[-1:] // MOE_BLOCK).astype(jnp.int32)
    return wts, dest, slot_tok, blk_exp, n_active


def _moe(h2, logits, layer, wg, wu, wd):
    n_tok = logits.shape[0]
    n_exp = wg.shape[1]
    wts, dest, slot_tok, blk_exp, n_active = _route(logits)
    flat = lambda w: w.reshape((w.shape[0] * n_exp,) + w.shape[2:])
    yb = _moe_blocks(blk_exp + layer * n_exp, slot_tok, n_active, h2, flat(wg), flat(wu), flat(wd))
    dest = dest.reshape(n_tok, TOP_K)
    take = functools.partial(jnp.take, axis=0, mode="clip")
    return take(yb, dest[:, 0]), take(yb, dest[:, 1]), wts


def _final_kernel(x_ref, y0_ref, y1_ref, wt_ref, pmod_ref, g_ref, o_ref):
    y = wt_ref[:, 0:1] * y0_ref[...].astype(F32) + wt_ref[:, 1:2] * y1_ref[...].astype(F32)
    x = x_ref[...] + pmod_ref[0, 5:6, :] * y
    o_ref[...] = x * lax.rsqrt(jnp.mean(x * x, axis=-1, keepdims=True) + NORM_EPS) * g_ref[...]


def _final(x, ymoe, pmod, g, *, n_lat, seq, b):
    d = x.shape[1]
    tm = 512
    row = lambda i: (i, 0)
    y0, y1, wts = ymoe
    return pl.pallas_call(
        _final_kernel,
        grid=(n_lat // tm,),
        in_specs=[pl.BlockSpec((tm, d), row), pl.BlockSpec((tm, d), row), pl.BlockSpec((tm, d), row),
                  pl.BlockSpec((tm, TOP_K), row),
                  pl.BlockSpec((1, 6, d), _mod_row_map(tm, n_lat, seq, b)),
                  pl.BlockSpec((1, d), lambda i: (0, 0))],
        out_specs=pl.BlockSpec((tm, d), row),
        out_shape=jax.ShapeDtypeStruct((n_lat, d), F32),
        compiler_params=_cparams(("parallel",)),
        name="final_residual_norm",
    )(x, y0, y1, wts, pmod, g)


def _arrange_w_in(w_in_l):
    d = w_in_l.shape[0]
    hg = w_in_l[:, :5 * GW]
    o = 5 * GW
    ssd_z = w_in_l[:, o:o + GW]
    ssd_xbc = w_in_l[:, o + GW:o + GW + 1024]
    ssd_dt = w_in_l[:, o + GW + 1024:o + GW + 1024 + 2 * SSD_HEADS]
    o += GW + 1024 + 2 * SSD_HEADS
    lru = w_in_l[:, o:o + 2 * GW]
    o += 2 * GW
    ml_main = w_in_l[:, o:o + 4 * GW]
    ml_gates = w_in_l[:, o + 4 * GW:o + 4 * GW + 4 * ML_HEADS]
    small = jnp.concatenate([ssd_dt, ml_gates], axis=1)
    pad = jnp.zeros((d, NCOLS - C_SMALL - small.shape[1]), w_in_l.dtype)
    return jnp.concatenate([hg, ssd_z, ssd_xbc, lru, ml_main, small, pad], axis=1)


def _blockdiag_dense(w):
    n, bw, _ = w.shape
    eye = jnp.eye(n, dtype=w.dtype)
    return (eye[:, None, :, None] * w[:, :, None, :]).reshape(n * bw, n * bw)


def _pad_lanes(v):
    return jnp.concatenate([v, jnp.zeros((LANE - v.shape[0],), v.dtype)])[None, :]


def kernel(x, c, ctx, c_ctx, ada_w, ada_b, norm1_g, norm2_g, w_in, w_out, hg_lb_logits, ssd_conv_w, ssd_a_log, ssd_dt_bias, ssd_d, ssd_norm_g, lru_conv_w, lru_wa, lru_ba, lru_wx, lru_bx, lru_lambda, ml_b_i, ml_b_f, router_group_w, router_expert_w, moe_w_gate, moe_w_up, moe_w_down, final_norm_g):
    nb, seq, d = x.shape
    nctx = ctx.shape[1]
    depth = ada_w.shape[0]
    n_lat = nb * seq
    assert nctx % T_BLOCK == 0 and seq % T_BLOCK == 0
    nc, nl = nctx // T_BLOCK, seq // T_BLOCK
    kw = dict(nb=nb, nc=nc, nl=nl)
    lay = dict(n_lat=n_lat, seq=seq, b=nb)

    p = jax.nn.softmax(hg_lb_logits.astype(F32), axis=0)
    lb_all = (jnp.cumsum(p, axis=0) - p[:1])[:, :, None, :]

    c_all = jnp.concatenate([c, c_ctx[None, :], jnp.zeros((SUB - nb - 1, d), F32)], axis=0)
    mods = _modulation(c_all, ada_w, ada_b).reshape(depth, SUB, 6, d)

    xall = jnp.concatenate([x.reshape(n_lat, d), ctx.reshape(nb * nctx, d)], axis=0)
    y_moe = None
    tn = 512
    for l in range(depth):
        w_l = _bf(_arrange_w_in(w_in[l])).reshape(d, NCOLS // tn, tn).transpose(1, 0, 2)
        xall, h1 = _res_norm(xall, y_moe, mods[l - 1] if l else None, norm1_g[l][None, :], mods[l], **lay)
        u = _in_proj(h1, w_l)

        o = _hg_pass(u, lb_all[l], None, rev=False, **kw)
        y_hg = _hg_pass(u, lb_all[l], o, rev=True, **kw)

        sbias = _pad_lanes(jnp.concatenate([ssd_dt_bias[l, 0], ssd_dt_bias[l, 1],
                                            ml_b_i[l, 0], ml_b_i[l, 1], ml_b_f[l, 0], ml_b_f[l, 1]]))
        arow = _pad_lanes(jnp.concatenate([-jnp.exp(ssd_a_log[l, 0]), -jnp.exp(ssd_a_log[l, 1])]))
        dsk = jnp.repeat(ssd_d[l], SSD_HEADDIM)[None, :]
        o = _ssd_pass(u, sbias, arow, ssd_conv_w[l], None, None, rev=False, **kw)
        y_ssd = _ssd_pass(u, sbias, arow, ssd_conv_w[l], (dsk, ssd_norm_g[l][None, :]), o, rev=True, **kw)

        w2 = _bf(jnp.stack([jnp.concatenate([_blockdiag_dense(lru_wa[l, dd]), _blockdiag_dense(lru_wx[l, dd])],
                                            axis=1) for dd in range(2)]))
        b2 = jnp.stack([jnp.concatenate([lru_ba[l, dd], lru_bx[l, dd]])[None, :] for dd in range(2)])
        crow = (-LRU_C * jax.nn.softplus(-lru_lambda[l]))[:, None, :]
        y_lru = _lru_mixer(u, (lru_conv_w[l], w2, b2, crow), nb=nb, seq=seq, ctx=nctx)

        o = _ml_pass(u, sbias, None, rev=False, **kw)
        y_ml = _ml_pass(u, sbias, o, rev=True, **kw)

        w_router = jnp.concatenate(
            [router_group_w[l], router_expert_w[l],
             jnp.zeros((d, LANE - N_EXPERT_GROUPS * (1 + EXPERTS_PER_GROUP)), F32)], axis=1)
        wr1 = _bf(w_router)
        wr12 = jnp.concatenate([wr1, _bf(w_router - wr1.astype(F32))], axis=1)
        xall, h2, logits = _out_proj((y_hg, y_ssd, y_lru, y_ml), _bf(w_out[l]).reshape(4, GW, d), xall,
                                     mods[l], norm2_g[l][None, :], wr12, **lay)
        y_moe = _moe(h2, logits, l, moe_w_gate, moe_w_up, moe_w_down)

    out = _final(xall, y_moe, mods[depth - 1], final_norm_g[None, :], **lay)
    return out.reshape(nb, seq, d)
```

```python
import functools
import math

import jax
import jax.numpy as jnp
from jax import lax
from jax.experimental import pallas as pl
from jax.experimental.pallas import tpu as pltpu

F32 = jnp.float32
BF16 = jnp.bfloat16

GRID_W = 64
HG_HEADS, HG_DK = 4, 128
SSD_HEADS, SSD_HEADDIM, SSD_GROUPS, SSD_STATE = 8, 64, 2, 128
LRU_C = 8.0
ML_HEADS, ML_DH = 4, 128
N_EXPERT_GROUPS, EXPERTS_PER_GROUP, TOP_K = 4, 8, 2
MOE_BLOCK = 256
NORM_EPS = 1e-6
NEG = -1e30

GW = 512
C_HG_Q, C_HG_I, C_HG_G, C_HG_ZF, C_HG_ZB = 0, 512, 1024, 1536, 2048
C_SSD_Z, C_SSD_XBC = 2560, 3072
C_LRU_GATE, C_LRU_XR = 4096, 4608
C_ML_Q, C_ML_K, C_ML_V, C_ML_O = 5120, 5632, 6144, 6656
C_SMALL = 7168
SM_DT, SM_IG, SM_FG = 0, 16, 24
NCOLS = 7680

LANE = 128
SUB = 8
VMEM_LIMIT = 48 * 1024 * 1024

T_CHUNK = 128
T_BLOCK = 2 * T_CHUNK
HG_T_CHUNK = T_CHUNK


def _cparams(sem):
    return pltpu.CompilerParams(dimension_semantics=sem, vmem_limit_bytes=VMEM_LIMIT)


def _dot(a, b):
    return jnp.dot(a, b, preferred_element_type=F32)


def _dot_nt(a, b):
    return lax.dot_general(a, b, (((1,), (1,)), ((), ())), preferred_element_type=F32)


def _dot_tn(a, b):
    return lax.dot_general(a, b, (((0,), (0,)), ((), ())), preferred_element_type=F32)


def _bf(x):
    return x.astype(BF16)


def _sigmoid(x):
    return 0.5 * jnp.tanh(0.5 * x) + 0.5


def _softplus(x):
    return jnp.maximum(x, 0.0) + jnp.log1p(jnp.exp(-jnp.abs(x)))


def _split_dot(mask_bf, x):
    w = x.shape[1]
    hi = _bf(x)
    r1 = x - hi.astype(F32)
    mid = _bf(r1)
    lo = _bf(r1 - mid.astype(F32))
    p = _dot(mask_bf, jnp.concatenate([hi, mid, lo], axis=-1))
    return p[:, :w] + p[:, w:2 * w] + p[:, 2 * w:]


def _rollb(x, n, rev):
    if n == 0:
        return x
    t = x.shape[0]
    return pltpu.roll(x, (t - n) if rev else n, 0)


def _iota2(t):
    return (lax.broadcasted_iota(jnp.int32, (t, t), 0), lax.broadcasted_iota(jnp.int32, (t, t), 1))


def _mod_kernel(c_ref, w_ref, b_ref, o_ref):
    c = c_ref[...]
    s = c * _sigmoid(c)
    o_ref[0] = jnp.dot(s, w_ref[0], precision=lax.Precision.HIGHEST,
                       preferred_element_type=F32) + b_ref[0]


def _modulation(c_all, ada_w, ada_b):
    depth, d, n6 = ada_w.shape
    tn = 1024
    return pl.pallas_call(
        _mod_kernel,
        grid=(depth, n6 // tn),
        in_specs=[pl.BlockSpec((SUB, d), lambda l, j: (0, 0)),
                  pl.BlockSpec((1, d, tn), lambda l, j: (l, 0, j)),
                  pl.BlockSpec((1, 1, tn), lambda l, j: (l, 0, j))],
        out_specs=pl.BlockSpec((1, SUB, tn), lambda l, j: (l, 0, j)),
        out_shape=jax.ShapeDtypeStruct((depth, SUB, n6), F32),
        compiler_params=_cparams(("parallel", "parallel")),
        name="adaln_modulation",
    )(c_all, ada_w, ada_b.reshape(depth, 1, n6))


def _rms_mod(x, g, shift, scale):
    y = x * lax.rsqrt(jnp.mean(x * x, axis=-1, keepdims=True) + NORM_EPS) * g
    return y * (1.0 + scale) + shift


def _mod_row_map(tm, n_lat, seq, b):
    def f(i, *_):
        return (jnp.where(i * tm < n_lat, (i * tm) // seq, b), 0, 0)
    return f


def _resnorm_kernel(*refs, residual):
    if residual:
        x_ref, y0_ref, y1_ref, wt_ref, pmod_ref, g_ref, mod_ref, xn_ref, h_ref = refs
        y = (wt_ref[:, 0:1] * y0_ref[...].astype(F32) + wt_ref[:, 1:2] * y1_ref[...].astype(F32))
        x = x_ref[...] + pmod_ref[0, 5:6, :] * y
        xn_ref[...] = x
    else:
        x_ref, g_ref, mod_ref, h_ref = refs
        x = x_ref[...]
    h_ref[...] = _bf(_rms_mod(x, g_ref[...], mod_ref[0, 0:1, :], mod_ref[0, 1:2, :]))


def _res_norm(x, ymoe, pmod, g1, mod, *, n_lat, seq, b):
    ntot, d = x.shape
    tm = 512
    residual = ymoe is not None
    row = lambda i: (i, 0)
    mrow = _mod_row_map(tm, n_lat, seq, b)
    in_specs = [pl.BlockSpec((tm, d), row)]
    args = [x]
    if residual:
        y0, y1, wts = ymoe
        in_specs += [pl.BlockSpec((tm, d), row), pl.BlockSpec((tm, d), row),
                     pl.BlockSpec((tm, TOP_K), row), pl.BlockSpec((1, 6, d), mrow)]
        args += [y0, y1, wts, pmod]
    in_specs += [pl.BlockSpec((1, d), lambda i: (0, 0)), pl.BlockSpec((1, 6, d), mrow)]
    args += [g1, mod]
    out_specs = [pl.BlockSpec((tm, d), row)]
    out_shape = [jax.ShapeDtypeStruct((ntot, d), BF16)]
    if residual:
        out_specs = [pl.BlockSpec((tm, d), row)] + out_specs
        out_shape = [jax.ShapeDtypeStruct((ntot, d), F32)] + out_shape
    res = pl.pallas_call(
        functools.partial(_resnorm_kernel, residual=residual),
        grid=(ntot // tm,),
        in_specs=in_specs, out_specs=out_specs, out_shape=out_shape,
        compiler_params=_cparams(("parallel",)),
        name="residual_norm1",
    )(*args)
    return (res[0], res[1]) if residual else (x, res[0])


def _inproj_kernel(h_ref, w_ref, u_ref):
    u_ref[...] = _dot(h_ref[...], w_ref[0])


def _in_proj(h, w_tiles):
    ntot, d = h.shape
    nt, _, tn = w_tiles.shape
    tm = 1536
    assert ntot % tm == 0
    return pl.pallas_call(
        _inproj_kernel,
        grid=(ntot // tm, nt),
        in_specs=[pl.BlockSpec((tm, d), lambda i, j: (i, 0)),
                  pl.BlockSpec((1, d, tn), lambda i, j: (j, 0, 0))],
        out_specs=pl.BlockSpec((tm, tn), lambda i, j: (i, j)),
        out_shape=jax.ShapeDtypeStruct((ntot, nt * tn), F32),
        compiler_params=_cparams(("parallel", "arbitrary")),
        name="in_proj",
    )(h, w_tiles)


def _chunk_nat(i, rev, nc, nl):
    is_ctx = i < nc
    if rev:
        c = jnp.where(is_ctx, nc - 1 - i, nl - 1 - (i - nc))
    else:
        c = jnp.where(is_ctx, i, i - nc)
    return is_ctx, c


def _row_block(b, i, *, rev, nc, nl, nb):
    is_ctx, c = _chunk_nat(i, rev, nc, nl)
    return jnp.where(is_ctx, nb * nl + b * nc + c, b * nl + c)


def _col_spec(t, width, col, rb):
    cb = col // width
    return pl.BlockSpec((t, width), lambda b, i: (rb(b, i), cb))


def _hg_kernel(*refs, rev, finish):
    if finish:
        q_ref, v_ref, z_ref, lb_ref, g_ref, of_ref, o_ref, st_scr = refs
    else:
        q_ref, v_ref, z_ref, lb_ref, o_ref, st_scr = refs
    t = HG_T_CHUNK
    dk = HG_DK
    w = q_ref.shape[1]

    @pl.when(pl.program_id(1) == 0)
    def _():
        st_scr[...] = jnp.zeros_like(st_scr)

    lb = lb_ref[0]
    row, col = _iota2(t)
    valid = (col >= row) if rev else (col <= row)
    lvl = jnp.where(valid, 31 - lax.clz(row ^ col), -2)
    valid_bf = _bf(valid.astype(F32))
    tpos = lax.broadcasted_iota(jnp.int32, (t, w), 0)
    r8 = lax.broadcasted_iota(jnp.int32, (t // SUB, SUB, w), 1)
    for rs in _sub_chunks(q_ref.shape[0], rev, t):
        _hg_chunk(rs, rev, finish, refs, lb, valid_bf, lvl, tpos, r8)


def _sub_chunks(block_rows, rev, t=T_CHUNK):
    n = block_rows // t
    order = range(n - 1, -1, -1) if rev else range(n)
    return [slice(i * t, (i + 1) * t) for i in order]


def _hg_chunk(rs, rev, finish, refs, lb, valid_bf, lvl, tpos, r8):
    if finish:
        q_ref, v_ref, z_ref, lb_ref, g_ref, of_ref, o_ref, st_scr = refs
    else:
        q_ref, v_ref, z_ref, lb_ref, o_ref, st_scr = refs
    t = HG_T_CHUNK
    dk = HG_DK
    w = q_ref.shape[1]
    sig = _sigmoid(z_ref[rs, :])
    f = lb + (1.0 - lb) * sig
    lf = jnp.log(f)
    k = (1.0 - lb) * (1.0 - sig)
    q = q_ref[rs, :]
    v = v_ref[rs, :]

    b = _split_dot(valid_bf, lf)
    total = jnp.sum(lf, axis=0, keepdims=True)

    b8 = b.reshape(t // SUB, SUB, w)
    ops = [(-1, _bf(q), _bf(k))]
    h = 1
    while h < t:
        blk = 2 * h
        lg = int(math.log2(h))
        is_q = ((tpos & h) == 0) if rev else ((tpos & h) != 0)
        if h == 1:
            e = f
        else:
            off = h if rev else h - 1
            if blk <= SUB:
                ref = b8[:, off:off + 1, :]
                for sb in range(1, SUB // blk):
                    ref = jnp.where(r8 >= sb * blk, b8[:, sb * blk + off:sb * blk + off + 1, :], ref)
                ref = jnp.broadcast_to(ref, b8.shape).reshape(t, w)
            else:
                b3 = b.reshape(t // blk, blk, w)
                ref = jnp.broadcast_to(b3[:, off:off + 1, :], b3.shape).reshape(t, w)
            e = jnp.exp(-jnp.abs(b - ref))
        if h == 1:
            m = _bf(jnp.where(is_q, q * e, k))
        else:
            m = _bf(jnp.where(is_q, q, k) * e)
        ops.append((lg, m, m))
        h = blk

    qb = _bf(q * jnp.exp(b))
    kw = _bf(k * jnp.exp(total - b))
    vb = _bf(v)
    dec_tot = jnp.exp(total)

    masks = [lvl == lg for lg, _, _ in ops]
    outs = []
    for hd in range(HG_HEADS):
        sl = slice(hd * dk, (hd + 1) * dk)
        a = jnp.zeros((t, t), F32)
        for msk, (lg, mq, mk) in zip(masks, ops):
            a = jnp.where(msk, _dot_nt(mq[:, sl], mk[:, sl]), a)
        st = st_scr[hd]
        o = _dot(_bf(a), vb[:, sl]) + _dot_nt(qb[:, sl], _bf(st))
        st_scr[hd] = st * dec_tot[:, sl] + _dot_tn(vb[:, sl], kw[:, sl])
        outs.append(o)

    if finish:
        for hd in range(HG_HEADS):
            sl = slice(hd * dk, (hd + 1) * dk)
            tot = outs[hd] + of_ref[rs, sl]
            y = tot * lax.rsqrt(jnp.mean(tot * tot, axis=-1, keepdims=True) + NORM_EPS)
            g = g_ref[rs, sl]
            o_ref[rs, sl] = (y * (g * _sigmoid(g))).astype(o_ref.dtype)
    else:
        for hd in range(HG_HEADS):
            o_ref[rs, hd * dk:(hd + 1) * dk] = outs[hd]


def _hg_pass(u, lb, o_f, *, rev, nb, nc, nl):
    t = T_BLOCK
    ntot = u.shape[0]
    finish = o_f is not None
    rb = functools.partial(_row_block, rev=rev, nc=nc, nl=nl, nb=nb)
    d = 1 if rev else 0
    in_specs = [_col_spec(t, GW, C_HG_Q, rb), _col_spec(t, GW, C_HG_I, rb),
                _col_spec(t, GW, C_HG_ZB if rev else C_HG_ZF, rb),
                pl.BlockSpec((1, 1, GW), lambda b, i: (d, 0, 0))]
    args = [u, u, u, lb]
    if finish:
        in_specs += [_col_spec(t, GW, C_HG_G, rb), _col_spec(t, GW, 0, rb)]
        args += [u, o_f]
    return pl.pallas_call(
        functools.partial(_hg_kernel, rev=rev, finish=finish),
        grid=(nb, nc + nl),
        in_specs=in_specs,
        out_specs=_col_spec(t, GW, 0, rb),
        out_shape=jax.ShapeDtypeStruct((ntot, GW), BF16 if finish else F32),
        scratch_shapes=[pltpu.VMEM((HG_HEADS, HG_DK, HG_DK), F32)],
        compiler_params=_cparams(("parallel", "arbitrary")),
        name="hgrn2_bwd_finish" if rev else "hgrn2_fwd",
    )(*args)


def _ml_kernel(*refs, rev, finish):
    ct_scr, m_scr = refs[-2:]
    t = T_CHUNK

    @pl.when(pl.program_id(1) == 0)
    def _():
        ct_scr[...] = jnp.zeros_like(ct_scr)
        m_scr[...] = jnp.full(m_scr.shape, NEG, F32)

    row, col = _iota2(t)
    valid = (col >= row) if rev else (col <= row)
    valid_bf = _bf(valid.astype(F32))
    for rs in _sub_chunks(refs[0].shape[0], rev):
        _ml_chunk(rs, rev, finish, refs, valid, valid_bf)


def _ml_chunk(rs, rev, finish, refs, valid, valid_bf):
    if finish:
        q_ref, k_ref, v_ref, sm_ref, sb_ref, og_ref, of_ref, o_ref, ct_scr, m_scr = refs
    else:
        q_ref, k_ref, v_ref, sm_ref, sb_ref, o_ref, ct_scr, m_scr = refs
    t = T_CHUNK
    dh = ML_DH
    d = 1 if rev else 0

    sm = sm_ref[rs, :] + sb_ref[...]
    lsg = jnp.minimum(sm, 0.0) - jnp.log1p(jnp.exp(-jnp.abs(sm)))
    bcol = _split_dot(valid_bf, lsg)
    tot_row = jnp.sum(lsg, axis=0, keepdims=True)
    b_t = bcol.T
    sm_t = sm.T

    ks = k_ref[rs, :] * (dh ** -0.5)
    qb, kb, vb = _bf(q_ref[rs, :]), _bf(ks), _bf(v_ref[rs, :])
    ones = jnp.ones((t, dh), BF16)

    for hd in range(ML_HEADS):
        sl = slice(hd * dh, (hd + 1) * dh)
        ci = SM_IG + ML_HEADS * d + hd
        cf = SM_FG + ML_HEADS * d + hd
        bc = bcol[:, cf:cf + 1]
        br = b_t[cf:cf + 1, :]
        lir = sm_t[ci:ci + 1, :]
        lic = sm[:, ci:ci + 1]
        m_prev = m_scr[hd:hd + 1, 0:1]
        dmat = jnp.where(valid, bc - br + lir, NEG)
        g = bc + m_prev
        mt = jnp.maximum(g, jnp.max(dmat, axis=-1, keepdims=True))
        w_intra = jnp.exp(dmat - mt)
        w_inter = jnp.exp(g - mt)
        s = _dot_nt(qb[:, sl], kb[:, sl]) * w_intra
        vaug = jnp.concatenate([vb[:, sl], ones], axis=-1)
        ct = ct_scr[hd]
        nd = _dot(_bf(s), vaug) + w_inter * _dot_nt(qb[:, sl], _bf(ct))
        hout = nd[:, :dh] / jnp.maximum(jnp.abs(nd[:, dh:]), jnp.exp(-mt))
        tot = tot_row[:, cf:cf + 1]
        dl = tot - bc + lic
        m_new = jnp.maximum(tot + m_prev, jnp.max(dl, axis=0, keepdims=True))
        ws = jnp.exp(dl - m_new)
        wc = jnp.exp(tot + m_prev - m_new)
        ct_scr[hd] = wc * ct + _dot_tn(vaug, _bf(ks[:, sl] * ws))
        m_scr[hd:hd + 1, :] = jnp.broadcast_to(m_new, (1, m_scr.shape[1]))
        if finish:
            tot_o = hout + of_ref[rs, sl]
            y = tot_o * lax.rsqrt(jnp.mean(tot_o * tot_o, axis=-1, keepdims=True) + NORM_EPS)
            o_ref[rs, sl] = (y * _sigmoid(og_ref[rs, sl])).astype(o_ref.dtype)
        else:
            o_ref[rs, sl] = hout


def _ml_pass(u, sbias, o_f, *, rev, nb, nc, nl):
    t = T_BLOCK
    ntot = u.shape[0]
    finish = o_f is not None
    rb = functools.partial(_row_block, rev=rev, nc=nc, nl=nl, nb=nb)
    in_specs = [_col_spec(t, GW, C_ML_Q, rb), _col_spec(t, GW, C_ML_K, rb),
                _col_spec(t, GW, C_ML_V, rb), _col_spec(t, LANE, C_SMALL, rb),
                pl.BlockSpec((1, LANE), lambda b, i: (0, 0))]
    args = [u, u, u, u, sbias]
    if finish:
        in_specs += [_col_spec(t, GW, C_ML_O, rb), _col_spec(t, GW, 0, rb)]
        args += [u, o_f]
    return pl.pallas_call(
        functools.partial(_ml_kernel, rev=rev, finish=finish),
        grid=(nb, nc + nl),
        in_specs=in_specs,
        out_specs=_col_spec(t, GW, 0, rb),
        out_shape=jax.ShapeDtypeStruct((ntot, GW), BF16 if finish else F32),
        scratch_shapes=[pltpu.VMEM((ML_HEADS, 2 * ML_DH, ML_DH), F32),
                        pltpu.VMEM((SUB, LANE), F32)],
        compiler_params=_cparams(("parallel", "arbitrary")),
        name="mlstm_bwd_finish" if rev else "mlstm_fwd",
    )(*args)


def _conv4(xe_scr, x, prev8, next8, w, first, last):
    t = x.shape[0]
    xe_scr[0:SUB, :] = jnp.where(first, 0.0, prev8)
    xe_scr[SUB:SUB + t, :] = x
    xe_scr[SUB + t:SUB + t + SUB, :] = jnp.where(last, 0.0, next8)
    out = w[0:1, :] * xe_scr[SUB - 2:SUB - 2 + t, :]
    for j in range(1, 4):
        out = out + w[j:j + 1, :] * xe_scr[SUB - 2 + j:SUB - 2 + j + t, :]
    return out


def _ssd_kernel(*refs, rev, finish, nc, nl):
    x_ref, xp_ref, xn_ref = refs[:3]
    cw_ref = refs[6]
    st_scr, xe_scr, xc_scr = refs[-3:]
    t = T_CHUNK
    i = pl.program_id(1)

    @pl.when(i == 0)
    def _():
        st_scr[...] = jnp.zeros_like(st_scr)

    is_ctx, c = _chunk_nat(i, rev, nc, nl)
    first = c == 0
    last = c == jnp.where(is_ctx, nc, nl) - 1
    xc = _conv4(xe_scr, x_ref[...], xp_ref[...], xn_ref[...], cw_ref[...], first, last)
    xc_scr[...] = xc * _sigmoid(xc)

    row, col = _iota2(t)
    valid = (col >= row) if rev else (col <= row)
    valid_bf = _bf(valid.astype(F32))
    for rs in _sub_chunks(x_ref.shape[0], rev):
        _ssd_chunk(rs, rev, finish, refs, valid, valid_bf)


def _ssd_chunk(rs, rev, finish, refs, valid, valid_bf):
    if finish:
        (x_ref, xp_ref, xn_ref, sm_ref, sb_ref, arow_ref, cw_ref,
         z_ref, of_ref, dsk_ref, ng_ref, o_ref, st_scr, xe_scr, xc_scr) = refs
    else:
        x_ref, xp_ref, xn_ref, sm_ref, sb_ref, arow_ref, cw_ref, o_ref, st_scr, xe_scr, xc_scr = refs
    p = SSD_HEADDIM
    n = SSD_STATE
    d = 1 if rev else 0
    xs = xc_scr[rs, :SSD_HEADS * p]
    bm = xc_scr[rs, SSD_HEADS * p:SSD_HEADS * p + SSD_GROUPS * n]
    cm = xc_scr[rs, SSD_HEADS * p + SSD_GROUPS * n:]

    dl = _softplus(sm_ref[rs, :] + sb_ref[...])
    lfm = dl * arow_ref[...]
    bcol = _split_dot(valid_bf, lfm)
    tot_row = jnp.sum(lfm, axis=0, keepdims=True)
    b_t = bcol.T
    dl_t = dl.T

    xsb = _bf(xs)
    rep = SSD_HEADS // SSD_GROUPS
    outs = []
    for gi in range(SSD_GROUPS):
        bg = bm[:, gi * n:(gi + 1) * n]
        cgb = _bf(cm[:, gi * n:(gi + 1) * n])
        gmat = _dot_nt(cgb, _bf(bg))
        for hh in range(rep):
            hd = gi * rep + hh
            cc = SM_DT + SSD_HEADS * d + hd
            bc = bcol[:, cc:cc + 1]
            br = b_t[cc:cc + 1, :]
            dr = dl_t[cc:cc + 1, :]
            dcol = dl[:, cc:cc + 1]
            tot = tot_row[:, cc:cc + 1]
            att = gmat * jnp.exp(jnp.where(valid, bc - br, NEG)) * dr
            xh = xsb[:, hd * p:(hd + 1) * p]
            st = st_scr[hd]
            o = _dot(_bf(att), xh) + jnp.exp(bc) * _dot_nt(cgb, _bf(st))
            wk = dcol * jnp.exp(tot - bc)
            st_scr[hd] = jnp.exp(tot) * st + _dot_tn(xh, _bf(bg * wk))
            outs.append(o)
    o_all = jnp.concatenate(outs, axis=-1)

    if finish:
        y = (o_all + of_ref[rs, :] + dsk_ref[...] * xs)
        z = z_ref[rs, :]
        y = y * (z * _sigmoid(z))
        gw = SSD_HEADS * p // SSD_GROUPS
        for gi in range(SSD_GROUPS):
            sl = slice(gi * gw, (gi + 1) * gw)
            yg = y[:, sl]
            yg = yg * lax.rsqrt(jnp.mean(yg * yg, axis=-1, keepdims=True) + NORM_EPS)
            o_ref[rs, sl] = (yg * ng_ref[:, sl]).astype(o_ref.dtype)
    else:
        o_ref[rs, :] = o_all


def _ssd_pass(u, sbias, arow, conv_w, z_extra, o_f, *, rev, nb, nc, nl):
    t = T_BLOCK
    ntot = u.shape[0]
    finish = o_f is not None
    rb = functools.partial(_row_block, rev=rev, nc=nc, nl=nl, nb=nb)
    r8 = t // SUB
    nblk8 = ntot // SUB
    cb = C_SSD_XBC // 1024

    def prev_map(b, i):
        return (jnp.maximum(rb(b, i) * r8 - 1, 0), cb)

    def next_map(b, i):
        return (jnp.minimum((rb(b, i) + 1) * r8, nblk8 - 1), cb)

    const2 = lambda b, i: (0, 0)
    in_specs = [_col_spec(t, 1024, C_SSD_XBC, rb),
                pl.BlockSpec((SUB, 1024), prev_map), pl.BlockSpec((SUB, 1024), next_map),
                _col_spec(t, LANE, C_SMALL, rb),
                pl.BlockSpec((1, LANE), const2), pl.BlockSpec((1, LANE), const2),
                pl.BlockSpec((4, 1024), const2)]
    args = [u, u, u, u, sbias, arow, conv_w]
    if finish:
        dsk, ng = z_extra
        in_specs += [_col_spec(t, GW, C_SSD_Z, rb), _col_spec(t, GW, 0, rb),
                     pl.BlockSpec((1, GW), const2), pl.BlockSpec((1, GW), const2)]
        args += [u, o_f, dsk, ng]
    return pl.pallas_call(
        functools.partial(_ssd_kernel, rev=rev, finish=finish, nc=nc, nl=nl),
        grid=(nb, nc + nl),
        in_specs=in_specs,
        out_specs=_col_spec(t, GW, 0, rb),
        out_shape=jax.ShapeDtypeStruct((ntot, GW), BF16 if finish else F32),
        scratch_shapes=[pltpu.VMEM((SSD_HEADS, SSD_HEADDIM, SSD_STATE), F32),
                        pltpu.VMEM((t + 2 * SUB, 1024), F32), pltpu.VMEM((t, 1024), F32)],
        compiler_params=_cparams(("parallel", "arbitrary")),
        name="ssd_bwd_finish" if rev else "ssd_fwd",
    )(*args)


def _lru_scan(xr, w2, b2, crow, h, rev):
    t, w = xr.shape
    ng = t // SUB
    pre = _dot(_bf(xr), w2) + b2
    r = _sigmoid(pre[:, :w])
    ig = _sigmoid(pre[:, w:])
    log_a = crow * r
    a = jnp.exp(log_a)
    y = jnp.tanh(-log_a) * (1.0 + a * a)
    bb = jnp.where(y > 0.0, y * lax.rsqrt(y), 0.0) * (ig * xr)

    a3 = a.reshape(ng, SUB, w)
    b3 = bb.reshape(ng, SUB, w)
    r8 = lax.broadcasted_iota(jnp.int32, (ng, SUB, w), 1)
    pos = (SUB - 1 - r8) if rev else r8
    step = 1
    while step < SUB:
        sh = (SUB - step) if rev else step
        a_s = pltpu.roll(a3, sh, 1)
        b_s = pltpu.roll(b3, sh, 1)
        ok = pos >= step
        b3 = jnp.where(ok, a3 * b_s + b3, b3)
        a3 = jnp.where(ok, a3 * a_s, a3)
        step *= 2
    groups = [None] * ng
    for g in (range(ng - 1, -1, -1) if rev else range(ng)):
        hg = b3[g] + a3[g] * h
        groups[g] = hg
        h = hg[0:1, :] if rev else hg[SUB - 1:SUB, :]
    return jnp.concatenate(groups, axis=0), h


def _gelu_tanh(g):
    return 0.5 * g * (1.0 + jnp.tanh(math.sqrt(2.0 / math.pi) * (g + 0.044715 * (g * g * g))))


LRU_COLS_PER_STEP = SUB


def _lru_kernel(*refs, rev, finish, latent, nsteps):
    if finish:
        (x_ref, xp_ref, xn_ref, cw_ref, w2_ref, b2_ref, crow_ref, h0_ref,
         gate_ref, of_ref, o_ref, ht_ref, h_scr, xe_scr) = refs
    else:
        x_ref, xp_ref, xn_ref, cw_ref, w2_ref, b2_ref, crow_ref, h0_ref, o_ref, ht_ref, h_scr, xe_scr = refs
    i = pl.program_id(1)

    @pl.when(i == 0)
    def _():
        h_scr[...] = h0_ref[0]

    c = (nsteps - 1 - i) if rev else i
    cw, w2, b2, crow = cw_ref[...], w2_ref[0], b2_ref[0], crow_ref[0]
    h = h_scr[...]
    if not latent:
        xr = _conv4(xe_scr, x_ref[...], xp_ref[...], xn_ref[...], cw, c == 0, c == nsteps - 1)
        hs, h = _lru_scan(xr, w2, b2, crow, h, rev)
        if finish:
            o_ref[...] = ((hs + of_ref[...]) * _gelu_tanh(gate_ref[...])).astype(o_ref.dtype)
        else:
            o_ref[...] = hs
    else:
        rows, ncol = x_ref.shape[0], x_ref.shape[1]
        for cl in (range(ncol - 1, -1, -1) if rev else range(ncol)):
            prev8 = x_ref[rows - SUB:rows, cl - 1, :] if cl > 0 else xp_ref[:, ncol - 1, :]
            next8 = x_ref[0:SUB, cl + 1, :] if cl < ncol - 1 else xn_ref[:, 0, :]
            first = (c == 0) if cl == 0 else False
            last = (c == nsteps - 1) if cl == ncol - 1 else False
            xr = _conv4(xe_scr, x_ref[:, cl, :], prev8, next8, cw, first, last)
            hs, h = _lru_scan(xr, w2, b2, crow, h, rev)
            if finish:
                o_ref[:, cl, :] = ((hs + of_ref[:, cl, :]) * _gelu_tanh(gate_ref[:, cl, :])).astype(o_ref.dtype)
            else:
                o_ref[:, cl, :] = hs
    h_scr[...] = h

    @pl.when(i == nsteps - 1)
    def _():
        ht_ref[0] = h


def _lru_pass(arr, views, consts, h0, o_f, *, rev, nb, nsteps, latent, out_shape, rows=None):
    finish = o_f is not None
    d = 1 if rev else 0
    main_map, prev_map, next_map, gate_map, out_map = views
    cw, w2, b2, crow = consts
    if latent:
        blk, halo = (rows, LRU_COLS_PER_STEP, GW), (SUB, LRU_COLS_PER_STEP, GW)
        t = rows
    else:
        blk, halo = (T_BLOCK, GW), (SUB, GW)
        t = T_BLOCK
    const2 = lambda b, i: (0, 0)
    dsel = lambda b, i: (d, 0, 0)
    hsel = lambda b, i: (b, 0, 0)
    in_specs = [pl.BlockSpec(blk, main_map), pl.BlockSpec(halo, prev_map), pl.BlockSpec(halo, next_map),
                pl.BlockSpec((4, GW), const2),
                pl.BlockSpec((1, GW, 2 * GW), dsel), pl.BlockSpec((1, 1, 2 * GW), dsel),
                pl.BlockSpec((1, 1, GW), dsel), pl.BlockSpec((1, 1, GW), hsel)]
    args = [arr, arr, arr, cw, w2, b2, crow, h0]
    if finish:
        in_specs += [pl.BlockSpec(blk, gate_map), pl.BlockSpec(blk, out_map)]
        args += [arr, o_f]
    seg = "lat" if latent else "ctx"
    return pl.pallas_call(
        functools.partial(_lru_kernel, rev=rev, finish=finish, latent=latent, nsteps=nsteps),
        grid=(nb, nsteps),
        in_specs=in_specs,
        out_specs=[pl.BlockSpec(blk, out_map), pl.BlockSpec((1, 1, GW), hsel)],
        out_shape=[jax.ShapeDtypeStruct(out_shape, BF16 if (finish and not latent) else F32),
                   jax.ShapeDtypeStruct((nb, 1, GW), F32)],
        scratch_shapes=[pltpu.VMEM((1, GW), F32), pltpu.VMEM((t + 2 * SUB, GW), F32)],
        compiler_params=_cparams(("parallel", "arbitrary")),
        name=f"rglru_{seg}_bwd_finish" if rev else f"rglru_{seg}_fwd",
    )(*args)


def _lru_mixer(u, consts, *, nb, seq, ctx):
    ntot, ncols = u.shape
    n_lat = nb * seq
    rows = seq // GRID_W
    ncg = GRID_W // LRU_COLS_PER_STEP
    ncb = ctx // T_BLOCK
    xr_cb, gate_cb = C_LRU_XR // GW, C_LRU_GATE // GW
    u3 = u.reshape(ntot // GRID_W, GRID_W, ncols)
    r8, c8 = rows // SUB, T_BLOCK // SUB
    base, base8 = n_lat // T_BLOCK, n_lat // SUB
    h = jnp.zeros((nb, 1, GW), F32)
    o_c = o_l = None
    for rev in (False, True):
        nat_c = (lambda i: ncb - 1 - i) if rev else (lambda i: i)
        nat_l = (lambda i: ncg - 1 - i) if rev else (lambda i: i)
        cviews = (lambda b, i: (base + b * ncb + nat_c(i), xr_cb),
                  lambda b, i: (jnp.maximum(base8 + (b * ncb + nat_c(i)) * c8 - 1, 0), xr_cb),
                  lambda b, i: (jnp.minimum(base8 + (b * ncb + nat_c(i) + 1) * c8, ntot // SUB - 1), xr_cb),
                  lambda b, i: (base + b * ncb + nat_c(i), gate_cb),
                  lambda b, i: (b * ncb + nat_c(i), 0))
        o_c, h_c = _lru_pass(u, cviews, consts, h, o_c, rev=rev, nb=nb, nsteps=ncb, latent=False,
                             out_shape=(nb * ctx, GW))
        lviews = (lambda b, i: (b, nat_l(i), xr_cb),
                  lambda b, i: (b * r8 + r8 - 1, jnp.maximum(nat_l(i) - 1, 0), xr_cb),
                  lambda b, i: (b * r8, jnp.minimum(nat_l(i) + 1, ncg - 1), xr_cb),
                  lambda b, i: (b, nat_l(i), gate_cb),
                  lambda b, i: (b, nat_l(i), 0))
        o_l, _ = _lru_pass(u3, lviews, consts, h_c, o_l, rev=rev, nb=nb, nsteps=ncg, latent=True,
                           out_shape=(n_lat // GRID_W, GRID_W, GW), rows=rows)
    return jnp.concatenate([_bf(o_l).reshape(n_lat, GW), o_c], axis=0)


def _outproj_kernel(yh_ref, ys_ref, yl_ref, ym_ref, w_ref, x_ref, mod_ref, g_ref, wr_ref,
                    xn_ref, h_ref, lg_ref):
    acc = _dot(yh_ref[...], w_ref[0])
    acc = acc + _dot(ys_ref[...], w_ref[1])
    acc = acc + _dot(yl_ref[...], w_ref[2])
    acc = acc + _dot(ym_ref[...], w_ref[3])
    xn = x_ref[...] + mod_ref[0, 2:3, :] * acc
    xn_ref[...] = xn
    h = _rms_mod(xn, g_ref[...], mod_ref[0, 3:4, :], mod_ref[0, 4:5, :])
    h1 = _bf(h)
    tm, d = h.shape
    bits = lax.bitcast_convert_type(h1.astype(F32), jnp.uint32)
    packed = (bits[:, :d // 2] >> 16) | (bits[:, d // 2:] & jnp.uint32(0xFFFF0000))
    nch = d // 2 // LANE
    for j in range(nch):
        h_ref[pl.ds(j, tm, stride=nch), :] = packed[:, j * LANE:(j + 1) * LANE]
    h2 = _bf(h - h1.astype(F32))
    p = _dot(h1, wr_ref[...])
    lg_ref[...] = p[:, :LANE] + p[:, LANE:] + _dot(h2, wr_ref[:, :LANE])


def _out_proj(ys, w_out4, x, mod, g2, w_router, *, n_lat, seq, b):
    ntot, d = x.shape
    tm = 512
    row = lambda i: (i, 0)
    mrow = _mod_row_map(tm, n_lat, seq, b)
    in_specs = [pl.BlockSpec((tm, GW), row)] * 4 + [
        pl.BlockSpec((4, GW, d), lambda i: (0, 0, 0)),
        pl.BlockSpec((tm, d), row), pl.BlockSpec((1, 6, d), mrow),
        pl.BlockSpec((1, d), lambda i: (0, 0)), pl.BlockSpec((d, 2 * LANE), lambda i: (0, 0))]
    return pl.pallas_call(
        _outproj_kernel,
        grid=(ntot // tm,),
        in_specs=in_specs,
        out_specs=[pl.BlockSpec((tm, d), row), pl.BlockSpec((tm * (d // 2 // LANE), LANE), row),
                   pl.BlockSpec((tm, LANE), row)],
        out_shape=[jax.ShapeDtypeStruct((ntot, d), F32),
                   jax.ShapeDtypeStruct((ntot * (d // 2 // LANE), LANE), jnp.uint32),
                   jax.ShapeDtypeStruct((ntot, LANE), F32)],
        compiler_params=_cparams(("parallel",)),
        name="out_proj_norm2_router",
    )(*ys, w_out4, x, mod, g2, w_router)


def _row_copy(h_hbm, xbuf, sem, tok, slot, r):
    nch = xbuf.shape[1] // MOE_BLOCK
    return pltpu.make_async_copy(h_hbm.at[pl.ds(tok * nch, nch), :], xbuf.at[slot, pl.ds(r * nch, nch), :],
                                 sem.at[slot])


MOE_DMA_GROUP = 32


def _moe_kernel(be_ref, tok_ref, nv_ref, na_ref, h_hbm, wg_ref, wu_ref, wd_ref, o_ref,
                xbuf, sem, wgu_scr, wd_scr):
    i = pl.program_id(0)
    n_active = na_ref[0]
    de = wg_ref.shape[-1]
    slot = i % 2
    nch = xbuf.shape[1] // MOE_BLOCK

    def for_filled_groups(blk, fn):
        for g0 in range(0, MOE_BLOCK, MOE_DMA_GROUP):
            @pl.when(g0 < nv_ref[blk])
            def _():
                for r in range(g0, g0 + MOE_DMA_GROUP):
                    fn(r)

    def start_gather(blk, s):
        for_filled_groups(
            blk, lambda r: _row_copy(h_hbm, xbuf, sem, tok_ref[blk * MOE_BLOCK + r], s, r).start())

    @pl.when(i == 0)
    def _():
        xbuf[...] = jnp.zeros_like(xbuf)
        start_gather(0, 0)

    @pl.when(i < n_active)
    def _():
        @pl.when(jnp.logical_or(i == 0, be_ref[i] != be_ref[jnp.maximum(i - 1, 0)]))
        def _():
            wgu_scr[:, :de] = _bf(wg_ref[0])
            wgu_scr[:, de:] = _bf(wu_ref[0])
            wd_scr[...] = _bf(wd_ref[0])

        for_filled_groups(i, lambda r: _row_copy(h_hbm, xbuf, sem, 0, slot, r).wait())

        @pl.when(i + 1 < n_active)
        def _():
            start_gather(i + 1, 1 - slot)

        lo, hi = [], []
        for j in range(nch):
            wv = xbuf[slot, pl.ds(j, MOE_BLOCK, stride=nch), :]
            lo.append(_bf(lax.bitcast_convert_type(wv << 16, F32)))
            hi.append(_bf(lax.bitcast_convert_type(wv & jnp.uint32(0xFFFF0000), F32)))
        xb = jnp.concatenate(lo + hi, axis=-1)
        gu = _dot(xb, wgu_scr[...])
        g = gu[:, :de]
        hid = (g * _sigmoid(g)) * gu[:, de:]
        o_ref[...] = _bf(_dot(_bf(hid), wd_scr[...]))

    @pl.when(i >= n_active)
    def _():
        o_ref[...] = jnp.zeros_like(o_ref)


def _moe_blocks(blk_exp, slot_tok, blk_fill, n_active, h, wg, wu, wd):
    nslot = slot_tok.shape[0]
    d = wg.shape[1]
    nch = d // 2 // LANE
    de = wg.shape[-1]
    nblk = nslot // MOE_BLOCK
    return pl.pallas_call(
        _moe_kernel,
        grid_spec=pltpu.PrefetchScalarGridSpec(
            num_scalar_prefetch=4,
            grid=(nblk,),
            in_specs=[pl.BlockSpec(memory_space=pl.ANY),
                      pl.BlockSpec((1, d, de), lambda i, be, tk, nv, na: (be[i], 0, 0)),
                      pl.BlockSpec((1, d, de), lambda i, be, tk, nv, na: (be[i], 0, 0)),
                      pl.BlockSpec((1, de, d), lambda i, be, tk, nv, na: (be[i], 0, 0))],
            out_specs=pl.BlockSpec((MOE_BLOCK, d), lambda i, be, tk, nv, na: (i, 0)),
            scratch_shapes=[pltpu.VMEM((2, MOE_BLOCK * nch, LANE), jnp.uint32), pltpu.SemaphoreType.DMA((2,)),
                            pltpu.VMEM((d, 2 * de), BF16), pltpu.VMEM((de, d), BF16)]),
        out_shape=jax.ShapeDtypeStruct((nslot, d), BF16),
        compiler_params=_cparams(("arbitrary",)),
        name="moe_expert_blocks",
    )(blk_exp, slot_tok, blk_fill, n_active, h, wg, wu, wd)


def _route(logits):
    n_tok = logits.shape[0]
    n_exp = N_EXPERT_GROUPS * EXPERTS_PER_GROUP
    g_logits = logits[:, :N_EXPERT_GROUPS]
    g_prob = jax.nn.softmax(g_logits, axis=-1)
    g_sel = jnp.argmax(g_logits, axis=-1)
    g_hot = g_sel[:, None] == jnp.arange(N_EXPERT_GROUPS)[None, :]
    e_logits = logits[:, N_EXPERT_GROUPS:N_EXPERT_GROUPS + n_exp].reshape(
        n_tok, N_EXPERT_GROUPS, EXPERTS_PER_GROUP)
    within = jnp.sum(jnp.where(g_hot[:, :, None], e_logits, 0.0), axis=1)
    top_v, top_i = lax.top_k(within, TOP_K)
    wts = jax.nn.softmax(top_v, axis=-1) * jnp.sum(jnp.where(g_hot, g_prob, 0.0), axis=1, keepdims=True)
    eid = (g_sel[:, None] * EXPERTS_PER_GROUP + top_i).astype(jnp.int32)

    n_assign = n_tok * TOP_K
    flat_e = eid.reshape(-1)
    onehot = (flat_e[:, None] == jnp.arange(n_exp, dtype=jnp.int32)[None, :]).astype(jnp.int32)
    csum = jnp.cumsum(onehot, axis=0)
    counts = csum[-1]
    padded = (counts + MOE_BLOCK - 1) // MOE_BLOCK * MOE_BLOCK
    pend = jnp.cumsum(padded)
    pstart = pend - padded
    dest = jnp.sum(onehot * (csum - 1 + pstart[None, :]), axis=1)
    n_blk = -(-n_assign // MOE_BLOCK) + n_exp
    tok = jnp.arange(n_assign, dtype=jnp.int32) // TOP_K
    slot_tok = jnp.zeros((n_blk * MOE_BLOCK,), jnp.int32).at[dest].set(tok)
    blk_start = jnp.arange(n_blk, dtype=jnp.int32) * MOE_BLOCK
    blk_exp = jnp.minimum(jnp.sum(blk_start[:, None] >= pend[None, :], axis=1), n_exp - 1).astype(jnp.int32)
    n_active = (pend[-1:] // MOE_BLOCK).astype(jnp.int32)
    blk_fill = jnp.clip(counts[blk_exp] + pstart[blk_exp] - blk_start, 0, MOE_BLOCK).astype(jnp.int32)
    return wts, dest, slot_tok, blk_exp, blk_fill, n_active


def _moe(h2, logits, layer, wg, wu, wd):
    n_tok = logits.shape[0]
    n_exp = wg.shape[1]
    wts, dest, slot_tok, blk_exp, blk_fill, n_active = _route(logits)
    flat = lambda w: w.reshape((w.shape[0] * n_exp,) + w.shape[2:])
    yb = _moe_blocks(blk_exp + layer * n_exp, slot_tok, blk_fill, n_active, h2, flat(wg), flat(wu), flat(wd))
    dest = dest.reshape(n_tok, TOP_K)
    take = functools.partial(jnp.take, axis=0, mode="clip")
    return take(yb, dest[:, 0]), take(yb, dest[:, 1]), wts


def _final_kernel(x_ref, y0_ref, y1_ref, wt_ref, pmod_ref, g_ref, o_ref):
    y = wt_ref[:, 0:1] * y0_ref[...].astype(F32) + wt_ref[:, 1:2] * y1_ref[...].astype(F32)
    x = x_ref[...] + pmod_ref[0, 5:6, :] * y
    o_ref[...] = x * lax.rsqrt(jnp.mean(x * x, axis=-1, keepdims=True) + NORM_EPS) * g_ref[...]


def _final(x, ymoe, pmod, g, *, n_lat, seq, b):
    d = x.shape[1]
    tm = 512
    row = lambda i: (i, 0)
    y0, y1, wts = ymoe
    return pl.pallas_call(
        _final_kernel,
        grid=(n_lat // tm,),
        in_specs=[pl.BlockSpec((tm, d), row), pl.BlockSpec((tm, d), row), pl.BlockSpec((tm, d), row),
                  pl.BlockSpec((tm, TOP_K), row),
                  pl.BlockSpec((1, 6, d), _mod_row_map(tm, n_lat, seq, b)),
                  pl.BlockSpec((1, d), lambda i: (0, 0))],
        out_specs=pl.BlockSpec((tm, d), row),
        out_shape=jax.ShapeDtypeStruct((n_lat, d), F32),
        compiler_params=_cparams(("parallel",)),
        name="final_residual_norm",
    )(x, y0, y1, wts, pmod, g)


def _arrange_w_in(w_in_l):
    d = w_in_l.shape[0]
    hg = w_in_l[:, :5 * GW]
    o = 5 * GW
    ssd_z = w_in_l[:, o:o + GW]
    ssd_xbc = w_in_l[:, o + GW:o + GW + 1024]
    ssd_dt = w_in_l[:, o + GW + 1024:o + GW + 1024 + 2 * SSD_HEADS]
    o += GW + 1024 + 2 * SSD_HEADS
    lru = w_in_l[:, o:o + 2 * GW]
    o += 2 * GW
    ml_main = w_in_l[:, o:o + 4 * GW]
    ml_gates = w_in_l[:, o + 4 * GW:o + 4 * GW + 4 * ML_HEADS]
    small = jnp.concatenate([ssd_dt, ml_gates], axis=1)
    pad = jnp.zeros((d, NCOLS - C_SMALL - small.shape[1]), w_in_l.dtype)
    return jnp.concatenate([hg, ssd_z, ssd_xbc, lru, ml_main, small, pad], axis=1)


def _blockdiag_dense(w):
    n, bw, _ = w.shape
    eye = jnp.eye(n, dtype=w.dtype)
    return (eye[:, None, :, None] * w[:, :, None, :]).reshape(n * bw, n * bw)


def _pad_lanes(v):
    return jnp.concatenate([v, jnp.zeros((LANE - v.shape[0],), v.dtype)])[None, :]


def kernel(x, c, ctx, c_ctx, ada_w, ada_b, norm1_g, norm2_g, w_in, w_out, hg_lb_logits, ssd_conv_w, ssd_a_log, ssd_dt_bias, ssd_d, ssd_norm_g, lru_conv_w, lru_wa, lru_ba, lru_wx, lru_bx, lru_lambda, ml_b_i, ml_b_f, router_group_w, router_expert_w, moe_w_gate, moe_w_up, moe_w_down, final_norm_g):
    nb, seq, d = x.shape
    nctx = ctx.shape[1]
    depth = ada_w.shape[0]
    n_lat = nb * seq
    assert nctx % T_BLOCK == 0 and seq % T_BLOCK == 0
    nc, nl = nctx // T_BLOCK, seq // T_BLOCK
    kw = dict(nb=nb, nc=nc, nl=nl)
    lay = dict(n_lat=n_lat, seq=seq, b=nb)

    p = jax.nn.softmax(hg_lb_logits.astype(F32), axis=0)
    lb_all = (jnp.cumsum(p, axis=0) - p[:1])[:, :, None, :]

    c_all = jnp.concatenate([c, c_ctx[None, :], jnp.zeros((SUB - nb - 1, d), F32)], axis=0)
    mods = _modulation(c_all, ada_w, ada_b).reshape(depth, SUB, 6, d)

    xall = jnp.concatenate([x.reshape(n_lat, d), ctx.reshape(nb * nctx, d)], axis=0)
    y_moe = None
    tn = 512
    for l in range(depth):
        w_l = _bf(_arrange_w_in(w_in[l])).reshape(d, NCOLS // tn, tn).transpose(1, 0, 2)
        xall, h1 = _res_norm(xall, y_moe, mods[l - 1] if l else None, norm1_g[l][None, :], mods[l], **lay)
        u = _in_proj(h1, w_l)

        o = _hg_pass(u, lb_all[l], None, rev=False, **kw)
        y_hg = _hg_pass(u, lb_all[l], o, rev=True, **kw)

        sbias = _pad_lanes(jnp.concatenate([ssd_dt_bias[l, 0], ssd_dt_bias[l, 1],
                                            ml_b_i[l, 0], ml_b_i[l, 1], ml_b_f[l, 0], ml_b_f[l, 1]]))
        arow = _pad_lanes(jnp.concatenate([-jnp.exp(ssd_a_log[l, 0]), -jnp.exp(ssd_a_log[l, 1])]))
        dsk = jnp.repeat(ssd_d[l], SSD_HEADDIM)[None, :]
        o = _ssd_pass(u, sbias, arow, ssd_conv_w[l], None, None, rev=False, **kw)
        y_ssd = _ssd_pass(u, sbias, arow, ssd_conv_w[l], (dsk, ssd_norm_g[l][None, :]), o, rev=True, **kw)

        w2 = _bf(jnp.stack([jnp.concatenate([_blockdiag_dense(lru_wa[l, dd]), _blockdiag_dense(lru_wx[l, dd])],
                                            axis=1) for dd in range(2)]))
        b2 = jnp.stack([jnp.concatenate([lru_ba[l, dd], lru_bx[l, dd]])[None, :] for dd in range(2)])
        crow = (-LRU_C * jax.nn.softplus(-lru_lambda[l]))[:, None, :]
        y_lru = _lru_mixer(u, (lru_conv_w[l], w2, b2, crow), nb=nb, seq=seq, ctx=nctx)

        o = _ml_pass(u, sbias, None, rev=False, **kw)
        y_ml = _ml_pass(u, sbias, o, rev=True, **kw)

        w_router = jnp.concatenate(
            [router_group_w[l], router_expert_w[l],
             jnp.zeros((d, LANE - N_EXPERT_GROUPS * (1 + EXPERTS_PER_GROUP)), F32)], axis=1)
        wr1 = _bf(w_router)
        wr12 = jnp.concatenate([wr1, _bf(w_router - wr1.astype(F32))], axis=1)
        xall, h2, logits = _out_proj((y_hg, y_ssd, y_lru, y_ml), _bf(w_out[l]).reshape(4, GW, d), xall,
                                     mods[l], norm2_g[l][None, :], wr12, **lay)
        y_moe = _moe(h2, logits, l, moe_w_gate, moe_w_up, moe_w_down)

    out = _final(xall, y_moe, mods[depth - 1], final_norm_g[None, :], **lay)
    return out.reshape(nb, seq, d)
```

```python
import functools
import math

import jax
import jax.numpy as jnp
from jax import lax
from jax.experimental import pallas as pl
from jax.experimental.pallas import tpu as pltpu

F32 = jnp.float32
BF16 = jnp.bfloat16

GRID_W = 64
HG_HEADS, HG_DK = 4, 128
SSD_HEADS, SSD_HEADDIM, SSD_GROUPS, SSD_STATE = 8, 64, 2, 128
LRU_C = 8.0
ML_HEADS, ML_DH = 4, 128
N_EXPERT_GROUPS, EXPERTS_PER_GROUP, TOP_K = 4, 8, 2
MOE_BLOCK = 256
NORM_EPS = 1e-6
NEG = -1e30

GW = 512
C_HG_Q, C_HG_I, C_HG_G, C_HG_ZF, C_HG_ZB = 0, 512, 1024, 1536, 2048
C_SSD_Z, C_SSD_XBC = 2560, 3072
C_LRU_GATE, C_LRU_XR = 4096, 4608
C_ML_Q, C_ML_K, C_ML_V, C_ML_O = 5120, 5632, 6144, 6656
C_SMALL = 7168
SM_DT, SM_IG, SM_FG = 0, 16, 24
NCOLS = 7680

LANE = 128
SUB = 8
VMEM_LIMIT = 48 * 1024 * 1024

T_CHUNK = 128
T_BLOCK = 2 * T_CHUNK
HG_T_CHUNK = T_CHUNK
IN_PROJ_TM_MAX = 2816


def _cparams(sem):
    return pltpu.CompilerParams(dimension_semantics=sem, vmem_limit_bytes=VMEM_LIMIT)


def _dot(a, b):
    return jnp.dot(a, b, preferred_element_type=F32)


def _dot_nt(a, b):
    return lax.dot_general(a, b, (((1,), (1,)), ((), ())), preferred_element_type=F32)


def _dot_tn(a, b):
    return lax.dot_general(a, b, (((0,), (0,)), ((), ())), preferred_element_type=F32)


def _bf(x):
    return x.astype(BF16)


def _sigmoid(x):
    return 0.5 * jnp.tanh(0.5 * x) + 0.5


def _softplus(x):
    return jnp.maximum(x, 0.0) + jnp.log1p(jnp.exp(-jnp.abs(x)))


def _split_dot(mask_bf, x):
    w = x.shape[1]
    hi = _bf(x)
    r1 = x - hi.astype(F32)
    mid = _bf(r1)
    lo = _bf(r1 - mid.astype(F32))
    p = _dot(mask_bf, jnp.concatenate([hi, mid, lo], axis=-1))
    return p[:, :w] + p[:, w:2 * w] + p[:, 2 * w:]


def _rollb(x, n, rev):
    if n == 0:
        return x
    t = x.shape[0]
    return pltpu.roll(x, (t - n) if rev else n, 0)


def _iota2(t):
    return (lax.broadcasted_iota(jnp.int32, (t, t), 0), lax.broadcasted_iota(jnp.int32, (t, t), 1))


def _mod_kernel(c_ref, w_ref, b_ref, o_ref):
    c = c_ref[...]
    s = c * _sigmoid(c)
    o_ref[0] = jnp.dot(s, w_ref[0], precision=lax.Precision.HIGHEST,
                       preferred_element_type=F32) + b_ref[0]


def _modulation(c_all, ada_w, ada_b):
    depth, d, n6 = ada_w.shape
    tn = 1024
    return pl.pallas_call(
        _mod_kernel,
        grid=(depth, n6 // tn),
        in_specs=[pl.BlockSpec((SUB, d), lambda l, j: (0, 0)),
                  pl.BlockSpec((1, d, tn), lambda l, j: (l, 0, j)),
                  pl.BlockSpec((1, 1, tn), lambda l, j: (l, 0, j))],
        out_specs=pl.BlockSpec((1, SUB, tn), lambda l, j: (l, 0, j)),
        out_shape=jax.ShapeDtypeStruct((depth, SUB, n6), F32),
        compiler_params=_cparams(("parallel", "parallel")),
        name="adaln_modulation",
    )(c_all, ada_w, ada_b.reshape(depth, 1, n6))


def _rms_mod(x, g, shift, scale):
    y = x * lax.rsqrt(jnp.mean(x * x, axis=-1, keepdims=True) + NORM_EPS) * g
    return y * (1.0 + scale) + shift


def _mod_row_map(tm, n_lat, seq, b):
    def f(i, *_):
        return (jnp.where(i * tm < n_lat, (i * tm) // seq, b), 0, 0)
    return f


def _resnorm_kernel(*refs, residual):
    if residual:
        x_ref, y0_ref, y1_ref, wt_ref, pmod_ref, g_ref, mod_ref, xn_ref, h_ref = refs
        y = (wt_ref[:, 0:1] * y0_ref[...].astype(F32) + wt_ref[:, 1:2] * y1_ref[...].astype(F32))
        x = x_ref[...] + pmod_ref[0, 5:6, :] * y
        xn_ref[...] = x
    else:
        x_ref, g_ref, mod_ref, h_ref = refs
        x = x_ref[...]
    h_ref[...] = _bf(_rms_mod(x, g_ref[...], mod_ref[0, 0:1, :], mod_ref[0, 1:2, :]))


def _res_norm(x, ymoe, pmod, g1, mod, *, n_lat, seq, b):
    ntot, d = x.shape
    tm = 512
    residual = ymoe is not None
    row = lambda i: (i, 0)
    mrow = _mod_row_map(tm, n_lat, seq, b)
    in_specs = [pl.BlockSpec((tm, d), row)]
    args = [x]
    if residual:
        y0, y1, wts = ymoe
        in_specs += [pl.BlockSpec((tm, d), row), pl.BlockSpec((tm, d), row),
                     pl.BlockSpec((tm, TOP_K), row), pl.BlockSpec((1, 6, d), mrow)]
        args += [y0, y1, wts, pmod]
    in_specs += [pl.BlockSpec((1, d), lambda i: (0, 0)), pl.BlockSpec((1, 6, d), mrow)]
    args += [g1, mod]
    out_specs = [pl.BlockSpec((tm, d), row)]
    out_shape = [jax.ShapeDtypeStruct((ntot, d), BF16)]
    if residual:
        out_specs = [pl.BlockSpec((tm, d), row)] + out_specs
        out_shape = [jax.ShapeDtypeStruct((ntot, d), F32)] + out_shape
    res = pl.pallas_call(
        functools.partial(_resnorm_kernel, residual=residual),
        grid=(ntot // tm,),
        in_specs=in_specs, out_specs=out_specs, out_shape=out_shape,
        compiler_params=_cparams(("parallel",)),
        name="residual_norm1",
    )(*args)
    return (res[0], res[1]) if residual else (x, res[0])


def _inproj_kernel(h_ref, w_ref, u_ref):
    u_ref[...] = _dot(h_ref[...], w_ref[0])


def _in_proj(h, w_tiles):
    ntot, d = h.shape
    nt, _, tn = w_tiles.shape
    tm = next(c for c in range(IN_PROJ_TM_MAX, 0, -MOE_BLOCK) if ntot % c == 0)
    return pl.pallas_call(
        _inproj_kernel,
        grid=(ntot // tm, nt),
        in_specs=[pl.BlockSpec((tm, d), lambda i, j: (i, 0)),
                  pl.BlockSpec((1, d, tn), lambda i, j: (j, 0, 0))],
        out_specs=pl.BlockSpec((tm, tn), lambda i, j: (i, j)),
        out_shape=jax.ShapeDtypeStruct((ntot, nt * tn), F32),
        compiler_params=_cparams(("parallel", "arbitrary")),
        name="in_proj",
    )(h, w_tiles)


def _chunk_nat(i, rev, nc, nl):
    is_ctx = i < nc
    if rev:
        c = jnp.where(is_ctx, nc - 1 - i, nl - 1 - (i - nc))
    else:
        c = jnp.where(is_ctx, i, i - nc)
    return is_ctx, c


def _row_block(b, i, *, rev, nc, nl, nb):
    is_ctx, c = _chunk_nat(i, rev, nc, nl)
    return jnp.where(is_ctx, nb * nl + b * nc + c, b * nl + c)


def _col_spec(t, width, col, rb):
    cb = col // width
    return pl.BlockSpec((t, width), lambda b, i: (rb(b, i), cb))


def _hg_kernel(*refs, rev, finish):
    if finish:
        q_ref, v_ref, z_ref, lb_ref, g_ref, of_ref, o_ref, st_scr = refs
    else:
        q_ref, v_ref, z_ref, lb_ref, o_ref, st_scr = refs
    t = HG_T_CHUNK
    dk = HG_DK
    w = q_ref.shape[1]

    @pl.when(pl.program_id(1) == 0)
    def _():
        st_scr[...] = jnp.zeros_like(st_scr)

    lb = lb_ref[0]
    row, col = _iota2(t)
    valid = (col >= row) if rev else (col <= row)
    lvl = jnp.where(valid, 31 - lax.clz(row ^ col), -2)
    valid_bf = _bf(valid.astype(F32))
    tpos = lax.broadcasted_iota(jnp.int32, (t, w), 0)
    r8 = lax.broadcasted_iota(jnp.int32, (t // SUB, SUB, w), 1)
    for rs in _sub_chunks(q_ref.shape[0], rev, t):
        _hg_chunk(rs, rev, finish, refs, lb, valid_bf, lvl, tpos, r8)


def _sub_chunks(block_rows, rev, t=T_CHUNK):
    n = block_rows // t
    order = range(n - 1, -1, -1) if rev else range(n)
    return [slice(i * t, (i + 1) * t) for i in order]


def _hg_chunk(rs, rev, finish, refs, lb, valid_bf, lvl, tpos, r8):
    if finish:
        q_ref, v_ref, z_ref, lb_ref, g_ref, of_ref, o_ref, st_scr = refs
    else:
        q_ref, v_ref, z_ref, lb_ref, o_ref, st_scr = refs
    t = HG_T_CHUNK
    dk = HG_DK
    w = q_ref.shape[1]
    sig = _sigmoid(z_ref[rs, :])
    f = lb + (1.0 - lb) * sig
    lf = jnp.log(f)
    k = (1.0 - lb) * (1.0 - sig)
    q = q_ref[rs, :]
    v = v_ref[rs, :]

    b = _split_dot(valid_bf, lf)
    total = jnp.sum(lf, axis=0, keepdims=True)

    b8 = b.reshape(t // SUB, SUB, w)
    ops = [(-1, _bf(q), _bf(k))]
    h = 1
    while h < t:
        blk = 2 * h
        lg = int(math.log2(h))
        is_q = ((tpos & h) == 0) if rev else ((tpos & h) != 0)
        if h == 1:
            e = f
        else:
            off = h if rev else h - 1
            if blk <= SUB:
                ref = b8[:, off:off + 1, :]
                for sb in range(1, SUB // blk):
                    ref = jnp.where(r8 >= sb * blk, b8[:, sb * blk + off:sb * blk + off + 1, :], ref)
                ref = jnp.broadcast_to(ref, b8.shape).reshape(t, w)
            else:
                b3 = b.reshape(t // blk, blk, w)
                ref = jnp.broadcast_to(b3[:, off:off + 1, :], b3.shape).reshape(t, w)
            e = jnp.exp(-jnp.abs(b - ref))
        if h == 1:
            m = _bf(jnp.where(is_q, q * e, k))
        else:
            m = _bf(jnp.where(is_q, q, k) * e)
        ops.append((lg, m, m))
        h = blk

    qb = _bf(q * jnp.exp(b))
    kw = _bf(k * jnp.exp(total - b))
    vb = _bf(v)
    dec_tot = jnp.exp(total)

    masks = [lvl == lg for lg, _, _ in ops]
    outs = []
    for hd in range(HG_HEADS):
        sl = slice(hd * dk, (hd + 1) * dk)
        a = jnp.zeros((t, t), F32)
        for msk, (lg, mq, mk) in zip(masks, ops):
            a = jnp.where(msk, _dot_nt(mq[:, sl], mk[:, sl]), a)
        st = st_scr[hd]
        o = _dot(_bf(a), vb[:, sl]) + _dot_nt(qb[:, sl], _bf(st))
        st_scr[hd] = st * dec_tot[:, sl] + _dot_tn(vb[:, sl], kw[:, sl])
        outs.append(o)

    if finish:
        for hd in range(HG_HEADS):
            sl = slice(hd * dk, (hd + 1) * dk)
            tot = outs[hd] + of_ref[rs, sl]
            y = tot * lax.rsqrt(jnp.mean(tot * tot, axis=-1, keepdims=True) + NORM_EPS)
            g = g_ref[rs, sl]
            o_ref[rs, sl] = (y * (g * _sigmoid(g))).astype(o_ref.dtype)
    else:
        for hd in range(HG_HEADS):
            o_ref[rs, hd * dk:(hd + 1) * dk] = outs[hd]


def _hg_pass(u, lb, o_f, *, rev, nb, nc, nl):
    t = T_BLOCK
    ntot = u.shape[0]
    finish = o_f is not None
    rb = functools.partial(_row_block, rev=rev, nc=nc, nl=nl, nb=nb)
    d = 1 if rev else 0
    in_specs = [_col_spec(t, GW, C_HG_Q, rb), _col_spec(t, GW, C_HG_I, rb),
                _col_spec(t, GW, C_HG_ZB if rev else C_HG_ZF, rb),
                pl.BlockSpec((1, 1, GW), lambda b, i: (d, 0, 0))]
    args = [u, u, u, lb]
    if finish:
        in_specs += [_col_spec(t, GW, C_HG_G, rb), _col_spec(t, GW, 0, rb)]
        args += [u, o_f]
    return pl.pallas_call(
        functools.partial(_hg_kernel, rev=rev, finish=finish),
        grid=(nb, nc + nl),
        in_specs=in_specs,
        out_specs=_col_spec(t, GW, 0, rb),
        out_shape=jax.ShapeDtypeStruct((ntot, GW), BF16 if finish else F32),
        scratch_shapes=[pltpu.VMEM((HG_HEADS, HG_DK, HG_DK), F32)],
        compiler_params=_cparams(("parallel", "arbitrary")),
        name="hgrn2_bwd_finish" if rev else "hgrn2_fwd",
    )(*args)


def _ml_kernel(*refs, rev, finish):
    ct_scr, m_scr = refs[-2:]
    t = T_CHUNK

    @pl.when(pl.program_id(1) == 0)
    def _():
        ct_scr[...] = jnp.zeros_like(ct_scr)
        m_scr[...] = jnp.full(m_scr.shape, NEG, F32)

    row, col = _iota2(t)
    valid = (col >= row) if rev else (col <= row)
    valid_bf = _bf(valid.astype(F32))
    for rs in _sub_chunks(refs[0].shape[0], rev):
        _ml_chunk(rs, rev, finish, refs, valid, valid_bf)


def _ml_chunk(rs, rev, finish, refs, valid, valid_bf):
    if finish:
        q_ref, k_ref, v_ref, sm_ref, sb_ref, og_ref, of_ref, o_ref, ct_scr, m_scr = refs
    else:
        q_ref, k_ref, v_ref, sm_ref, sb_ref, o_ref, ct_scr, m_scr = refs
    t = T_CHUNK
    dh = ML_DH
    d = 1 if rev else 0

    sm = sm_ref[rs, :] + sb_ref[...]
    lsg = jnp.minimum(sm, 0.0) - jnp.log1p(jnp.exp(-jnp.abs(sm)))
    bcol = _split_dot(valid_bf, lsg)
    tot_row = jnp.sum(lsg, axis=0, keepdims=True)
    b_t = bcol.T
    sm_t = sm.T

    ks = k_ref[rs, :] * (dh ** -0.5)
    qb, kb, vb = _bf(q_ref[rs, :]), _bf(ks), _bf(v_ref[rs, :])
    ones = jnp.ones((t, dh), BF16)

    for hd in range(ML_HEADS):
        sl = slice(hd * dh, (hd + 1) * dh)
        ci = SM_IG + ML_HEADS * d + hd
        cf = SM_FG + ML_HEADS * d + hd
        bc = bcol[:, cf:cf + 1]
        br = b_t[cf:cf + 1, :]
        lir = sm_t[ci:ci + 1, :]
        lic = sm[:, ci:ci + 1]
        m_prev = m_scr[hd:hd + 1, 0:1]
        dmat = jnp.where(valid, bc - br + lir, NEG)
        g = bc + m_prev
        mt = jnp.maximum(g, jnp.max(dmat, axis=-1, keepdims=True))
        w_intra = jnp.exp(dmat - mt)
        w_inter = jnp.exp(g - mt)
        s = _dot_nt(qb[:, sl], kb[:, sl]) * w_intra
        vaug = jnp.concatenate([vb[:, sl], ones], axis=-1)
        ct = ct_scr[hd]
        nd = _dot(_bf(s), vaug) + w_inter * _dot_nt(qb[:, sl], _bf(ct))
        hout = nd[:, :dh] / jnp.maximum(jnp.abs(nd[:, dh:]), jnp.exp(-mt))
        tot = tot_row[:, cf:cf + 1]
        dl = tot - bc + lic
        m_new = jnp.maximum(tot + m_prev, jnp.max(dl, axis=0, keepdims=True))
        ws = jnp.exp(dl - m_new)
        wc = jnp.exp(tot + m_prev - m_new)
        ct_scr[hd] = wc * ct + _dot_tn(vaug, _bf(ks[:, sl] * ws))
        m_scr[hd:hd + 1, :] = jnp.broadcast_to(m_new, (1, m_scr.shape[1]))
        if finish:
            tot_o = hout + of_ref[rs, sl]
            y = tot_o * lax.rsqrt(jnp.mean(tot_o * tot_o, axis=-1, keepdims=True) + NORM_EPS)
            o_ref[rs, sl] = (y * _sigmoid(og_ref[rs, sl])).astype(o_ref.dtype)
        else:
            o_ref[rs, sl] = hout


def _ml_pass(u, sbias, o_f, *, rev, nb, nc, nl):
    t = T_BLOCK
    ntot = u.shape[0]
    finish = o_f is not None
    rb = functools.partial(_row_block, rev=rev, nc=nc, nl=nl, nb=nb)
    in_specs = [_col_spec(t, GW, C_ML_Q, rb), _col_spec(t, GW, C_ML_K, rb),
                _col_spec(t, GW, C_ML_V, rb), _col_spec(t, LANE, C_SMALL, rb),
                pl.BlockSpec((1, LANE), lambda b, i: (0, 0))]
    args = [u, u, u, u, sbias]
    if finish:
        in_specs += [_col_spec(t, GW, C_ML_O, rb), _col_spec(t, GW, 0, rb)]
        args += [u, o_f]
    return pl.pallas_call(
        functools.partial(_ml_kernel, rev=rev, finish=finish),
        grid=(nb, nc + nl),
        in_specs=in_specs,
        out_specs=_col_spec(t, GW, 0, rb),
        out_shape=jax.ShapeDtypeStruct((ntot, GW), BF16 if finish else F32),
        scratch_shapes=[pltpu.VMEM((ML_HEADS, 2 * ML_DH, ML_DH), F32),
                        pltpu.VMEM((SUB, LANE), F32)],
        compiler_params=_cparams(("parallel", "arbitrary")),
        name="mlstm_bwd_finish" if rev else "mlstm_fwd",
    )(*args)


def _conv4(xe_scr, x, prev8, next8, w, first, last):
    t = x.shape[0]
    xe_scr[0:SUB, :] = jnp.where(first, 0.0, prev8)
    xe_scr[SUB:SUB + t, :] = x
    xe_scr[SUB + t:SUB + t + SUB, :] = jnp.where(last, 0.0, next8)
    out = w[0:1, :] * xe_scr[SUB - 2:SUB - 2 + t, :]
    for j in range(1, 4):
        out = out + w[j:j + 1, :] * xe_scr[SUB - 2 + j:SUB - 2 + j + t, :]
    return out


def _ssd_kernel(*refs, rev, finish, nc, nl):
    x_ref, xp_ref, xn_ref = refs[:3]
    cw_ref = refs[6]
    st_scr, xe_scr, xc_scr = refs[-3:]
    t = T_CHUNK
    i = pl.program_id(1)

    @pl.when(i == 0)
    def _():
        st_scr[...] = jnp.zeros_like(st_scr)

    is_ctx, c = _chunk_nat(i, rev, nc, nl)
    first = c == 0
    last = c == jnp.where(is_ctx, nc, nl) - 1
    xc = _conv4(xe_scr, x_ref[...], xp_ref[...], xn_ref[...], cw_ref[...], first, last)
    xc_scr[...] = xc * _sigmoid(xc)

    row, col = _iota2(t)
    valid = (col >= row) if rev else (col <= row)
    valid_bf = _bf(valid.astype(F32))
    for rs in _sub_chunks(x_ref.shape[0], rev):
        _ssd_chunk(rs, rev, finish, refs, valid, valid_bf)


def _ssd_chunk(rs, rev, finish, refs, valid, valid_bf):
    if finish:
        (x_ref, xp_ref, xn_ref, sm_ref, sb_ref, arow_ref, cw_ref,
         z_ref, of_ref, dsk_ref, ng_ref, o_ref, st_scr, xe_scr, xc_scr) = refs
    else:
        x_ref, xp_ref, xn_ref, sm_ref, sb_ref, arow_ref, cw_ref, o_ref, st_scr, xe_scr, xc_scr = refs
    p = SSD_HEADDIM
    n = SSD_STATE
    d = 1 if rev else 0
    xs = xc_scr[rs, :SSD_HEADS * p]
    bm = xc_scr[rs, SSD_HEADS * p:SSD_HEADS * p + SSD_GROUPS * n]
    cm = xc_scr[rs, SSD_HEADS * p + SSD_GROUPS * n:]

    dl = _softplus(sm_ref[rs, :] + sb_ref[...])
    lfm = dl * arow_ref[...]
    bcol = _split_dot(valid_bf, lfm)
    tot_row = jnp.sum(lfm, axis=0, keepdims=True)
    b_t = bcol.T
    dl_t = dl.T

    xsb = _bf(xs)
    rep = SSD_HEADS // SSD_GROUPS
    outs = []
    for gi in range(SSD_GROUPS):
        bg = bm[:, gi * n:(gi + 1) * n]
        cgb = _bf(cm[:, gi * n:(gi + 1) * n])
        gmat = _dot_nt(cgb, _bf(bg))
        for hh in range(rep):
            hd = gi * rep + hh
            cc = SM_DT + SSD_HEADS * d + hd
            bc = bcol[:, cc:cc + 1]
            br = b_t[cc:cc + 1, :]
            dr = dl_t[cc:cc + 1, :]
            dcol = dl[:, cc:cc + 1]
            tot = tot_row[:, cc:cc + 1]
            att = gmat * jnp.exp(jnp.where(valid, bc - br, NEG)) * dr
            xh = xsb[:, hd * p:(hd + 1) * p]
            st = st_scr[hd]
            o = _dot(_bf(att), xh) + jnp.exp(bc) * _dot_nt(cgb, _bf(st))
            wk = dcol * jnp.exp(tot - bc)
            st_scr[hd] = jnp.exp(tot) * st + _dot_tn(xh, _bf(bg * wk))
            outs.append(o)
    o_all = jnp.concatenate(outs, axis=-1)

    if finish:
        y = (o_all + of_ref[rs, :] + dsk_ref[...] * xs)
        z = z_ref[rs, :]
        y = y * (z * _sigmoid(z))
        gw = SSD_HEADS * p // SSD_GROUPS
        for gi in range(SSD_GROUPS):
            sl = slice(gi * gw, (gi + 1) * gw)
            yg = y[:, sl]
            yg = yg * lax.rsqrt(jnp.mean(yg * yg, axis=-1, keepdims=True) + NORM_EPS)
            o_ref[rs, sl] = (yg * ng_ref[:, sl]).astype(o_ref.dtype)
    else:
        o_ref[rs, :] = o_all


def _ssd_pass(u, sbias, arow, conv_w, z_extra, o_f, *, rev, nb, nc, nl):
    t = T_BLOCK
    ntot = u.shape[0]
    finish = o_f is not None
    rb = functools.partial(_row_block, rev=rev, nc=nc, nl=nl, nb=nb)
    r8 = t // SUB
    nblk8 = ntot // SUB
    cb = C_SSD_XBC // 1024

    def prev_map(b, i):
        return (jnp.maximum(rb(b, i) * r8 - 1, 0), cb)

    def next_map(b, i):
        return (jnp.minimum((rb(b, i) + 1) * r8, nblk8 - 1), cb)

    const2 = lambda b, i: (0, 0)
    in_specs = [_col_spec(t, 1024, C_SSD_XBC, rb),
                pl.BlockSpec((SUB, 1024), prev_map), pl.BlockSpec((SUB, 1024), next_map),
                _col_spec(t, LANE, C_SMALL, rb),
                pl.BlockSpec((1, LANE), const2), pl.BlockSpec((1, LANE), const2),
                pl.BlockSpec((4, 1024), const2)]
    args = [u, u, u, u, sbias, arow, conv_w]
    if finish:
        dsk, ng = z_extra
        in_specs += [_col_spec(t, GW, C_SSD_Z, rb), _col_spec(t, GW, 0, rb),
                     pl.BlockSpec((1, GW), const2), pl.BlockSpec((1, GW), const2)]
        args += [u, o_f, dsk, ng]
    return pl.pallas_call(
        functools.partial(_ssd_kernel, rev=rev, finish=finish, nc=nc, nl=nl),
        grid=(nb, nc + nl),
        in_specs=in_specs,
        out_specs=_col_spec(t, GW, 0, rb),
        out_shape=jax.ShapeDtypeStruct((ntot, GW), BF16 if finish else F32),
        scratch_shapes=[pltpu.VMEM((SSD_HEADS, SSD_HEADDIM, SSD_STATE), F32),
                        pltpu.VMEM((t + 2 * SUB, 1024), F32), pltpu.VMEM((t, 1024), F32)],
        compiler_params=_cparams(("parallel", "arbitrary")),
        name="ssd_bwd_finish" if rev else "ssd_fwd",
    )(*args)


def _lru_scan(xr, w2, b2, crow, h, rev):
    t, w = xr.shape
    ng = t // SUB
    pre = _dot(_bf(xr), w2) + b2
    r = _sigmoid(pre[:, :w])
    ig = _sigmoid(pre[:, w:])
    log_a = crow * r
    a = jnp.exp(log_a)
    y = jnp.tanh(-log_a) * (1.0 + a * a)
    bb = jnp.where(y > 0.0, y * lax.rsqrt(y), 0.0) * (ig * xr)

    a3 = a.reshape(ng, SUB, w)
    b3 = bb.reshape(ng, SUB, w)
    r8 = lax.broadcasted_iota(jnp.int32, (ng, SUB, w), 1)
    pos = (SUB - 1 - r8) if rev else r8
    step = 1
    while step < SUB:
        sh = (SUB - step) if rev else step
        a_s = pltpu.roll(a3, sh, 1)
        b_s = pltpu.roll(b3, sh, 1)
        ok = pos >= step
        b3 = jnp.where(ok, a3 * b_s + b3, b3)
        a3 = jnp.where(ok, a3 * a_s, a3)
        step *= 2
    groups = [None] * ng
    for g in (range(ng - 1, -1, -1) if rev else range(ng)):
        hg = b3[g] + a3[g] * h
        groups[g] = hg
        h = hg[0:1, :] if rev else hg[SUB - 1:SUB, :]
    return jnp.concatenate(groups, axis=0), h


def _gelu_tanh(g):
    return 0.5 * g * (1.0 + jnp.tanh(math.sqrt(2.0 / math.pi) * (g + 0.044715 * (g * g * g))))


LRU_COLS_PER_STEP = SUB


def _lru_kernel(*refs, rev, finish, latent, nsteps):
    if finish:
        (x_ref, xp_ref, xn_ref, cw_ref, w2_ref, b2_ref, crow_ref, h0_ref,
         gate_ref, of_ref, o_ref, ht_ref, h_scr, xe_scr) = refs
    else:
        x_ref, xp_ref, xn_ref, cw_ref, w2_ref, b2_ref, crow_ref, h0_ref, o_ref, ht_ref, h_scr, xe_scr = refs
    i = pl.program_id(1)

    @pl.when(i == 0)
    def _():
        h_scr[...] = h0_ref[0]

    c = (nsteps - 1 - i) if rev else i
    cw, w2, b2, crow = cw_ref[...], w2_ref[0], b2_ref[0], crow_ref[0]
    h = h_scr[...]
    if not latent:
        xr = _conv4(xe_scr, x_ref[...], xp_ref[...], xn_ref[...], cw, c == 0, c == nsteps - 1)
        hs, h = _lru_scan(xr, w2, b2, crow, h, rev)
        if finish:
            o_ref[...] = ((hs + of_ref[...]) * _gelu_tanh(gate_ref[...])).astype(o_ref.dtype)
        else:
            o_ref[...] = hs
    else:
        rows, ncol = x_ref.shape[0], x_ref.shape[1]
        for cl in (range(ncol - 1, -1, -1) if rev else range(ncol)):
            prev8 = x_ref[rows - SUB:rows, cl - 1, :] if cl > 0 else xp_ref[:, ncol - 1, :]
            next8 = x_ref[0:SUB, cl + 1, :] if cl < ncol - 1 else xn_ref[:, 0, :]
            first = (c == 0) if cl == 0 else False
            last = (c == nsteps - 1) if cl == ncol - 1 else False
            xr = _conv4(xe_scr, x_ref[:, cl, :], prev8, next8, cw, first, last)
            hs, h = _lru_scan(xr, w2, b2, crow, h, rev)
            if finish:
                o_ref[:, cl, :] = ((hs + of_ref[:, cl, :]) * _gelu_tanh(gate_ref[:, cl, :])).astype(o_ref.dtype)
            else:
                o_ref[:, cl, :] = hs
    h_scr[...] = h

    @pl.when(i == nsteps - 1)
    def _():
        ht_ref[0] = h


def _lru_pass(arr, views, consts, h0, o_f, *, rev, nb, nsteps, latent, out_shape, rows=None):
    finish = o_f is not None
    d = 1 if rev else 0
    main_map, prev_map, next_map, gate_map, out_map = views
    cw, w2, b2, crow = consts
    if latent:
        blk, halo = (rows, LRU_COLS_PER_STEP, GW), (SUB, LRU_COLS_PER_STEP, GW)
        t = rows
    else:
        blk, halo = (T_BLOCK, GW), (SUB, GW)
        t = T_BLOCK
    const2 = lambda b, i: (0, 0)
    dsel = lambda b, i: (d, 0, 0)
    hsel = lambda b, i: (b, 0, 0)
    in_specs = [pl.BlockSpec(blk, main_map), pl.BlockSpec(halo, prev_map), pl.BlockSpec(halo, next_map),
                pl.BlockSpec((4, GW), const2),
                pl.BlockSpec((1, GW, 2 * GW), dsel), pl.BlockSpec((1, 1, 2 * GW), dsel),
                pl.BlockSpec((1, 1, GW), dsel), pl.BlockSpec((1, 1, GW), hsel)]
    args = [arr, arr, arr, cw, w2, b2, crow, h0]
    if finish:
        in_specs += [pl.BlockSpec(blk, gate_map), pl.BlockSpec(blk, out_map)]
        args += [arr, o_f]
    seg = "lat" if latent else "ctx"
    return pl.pallas_call(
        functools.partial(_lru_kernel, rev=rev, finish=finish, latent=latent, nsteps=nsteps),
        grid=(nb, nsteps),
        in_specs=in_specs,
        out_specs=[pl.BlockSpec(blk, out_map), pl.BlockSpec((1, 1, GW), hsel)],
        out_shape=[jax.ShapeDtypeStruct(out_shape, BF16 if (finish and not latent) else F32),
                   jax.ShapeDtypeStruct((nb, 1, GW), F32)],
        scratch_shapes=[pltpu.VMEM((1, GW), F32), pltpu.VMEM((t + 2 * SUB, GW), F32)],
        compiler_params=_cparams(("parallel", "arbitrary")),
        name=f"rglru_{seg}_bwd_finish" if rev else f"rglru_{seg}_fwd",
    )(*args)


def _lru_mixer(u, consts, *, nb, seq, ctx):
    ntot, ncols = u.shape
    n_lat = nb * seq
    rows = seq // GRID_W
    ncg = GRID_W // LRU_COLS_PER_STEP
    ncb = ctx // T_BLOCK
    xr_cb, gate_cb = C_LRU_XR // GW, C_LRU_GATE // GW
    u3 = u.reshape(ntot // GRID_W, GRID_W, ncols)
    r8, c8 = rows // SUB, T_BLOCK // SUB
    base, base8 = n_lat // T_BLOCK, n_lat // SUB
    h = jnp.zeros((nb, 1, GW), F32)
    o_c = o_l = None
    for rev in (False, True):
        nat_c = (lambda i: ncb - 1 - i) if rev else (lambda i: i)
        nat_l = (lambda i: ncg - 1 - i) if rev else (lambda i: i)
        cviews = (lambda b, i: (base + b * ncb + nat_c(i), xr_cb),
                  lambda b, i: (jnp.maximum(base8 + (b * ncb + nat_c(i)) * c8 - 1, 0), xr_cb),
                  lambda b, i: (jnp.minimum(base8 + (b * ncb + nat_c(i) + 1) * c8, ntot // SUB - 1), xr_cb),
                  lambda b, i: (base + b * ncb + nat_c(i), gate_cb),
                  lambda b, i: (b * ncb + nat_c(i), 0))
        o_c, h_c = _lru_pass(u, cviews, consts, h, o_c, rev=rev, nb=nb, nsteps=ncb, latent=False,
                             out_shape=(nb * ctx, GW))
        lviews = (lambda b, i: (b, nat_l(i), xr_cb),
                  lambda b, i: (b * r8 + r8 - 1, jnp.maximum(nat_l(i) - 1, 0), xr_cb),
                  lambda b, i: (b * r8, jnp.minimum(nat_l(i) + 1, ncg - 1), xr_cb),
                  lambda b, i: (b, nat_l(i), gate_cb),
                  lambda b, i: (b, nat_l(i), 0))
        o_l, _ = _lru_pass(u3, lviews, consts, h_c, o_l, rev=rev, nb=nb, nsteps=ncg, latent=True,
                           out_shape=(n_lat // GRID_W, GRID_W, GW), rows=rows)
    return jnp.concatenate([_bf(o_l).reshape(n_lat, GW), o_c], axis=0)


def _outproj_kernel(yh_ref, ys_ref, yl_ref, ym_ref, w_ref, x_ref, mod_ref, g_ref, wr_ref,
                    xn_ref, h_ref, lg_ref):
    acc = _dot(yh_ref[...], w_ref[0])
    acc = acc + _dot(ys_ref[...], w_ref[1])
    acc = acc + _dot(yl_ref[...], w_ref[2])
    acc = acc + _dot(ym_ref[...], w_ref[3])
    xn = x_ref[...] + mod_ref[0, 2:3, :] * acc
    xn_ref[...] = xn
    h = _rms_mod(xn, g_ref[...], mod_ref[0, 3:4, :], mod_ref[0, 4:5, :])
    h1 = _bf(h)
    tm, d = h.shape
    bits = lax.bitcast_convert_type(h1.astype(F32), jnp.uint32)
    packed = (bits[:, :d // 2] >> 16) | (bits[:, d // 2:] & jnp.uint32(0xFFFF0000))
    nch = d // 2 // LANE
    for j in range(nch):
        h_ref[pl.ds(j, tm, stride=nch), :] = packed[:, j * LANE:(j + 1) * LANE]
    h2 = _bf(h - h1.astype(F32))
    p = _dot(h1, wr_ref[...])
    lg_ref[...] = p[:, :LANE] + p[:, LANE:] + _dot(h2, wr_ref[:, :LANE])


def _out_proj(ys, w_out4, x, mod, g2, w_router, *, n_lat, seq, b):
    ntot, d = x.shape
    tm = 512
    row = lambda i: (i, 0)
    mrow = _mod_row_map(tm, n_lat, seq, b)
    in_specs = [pl.BlockSpec((tm, GW), row)] * 4 + [
        pl.BlockSpec((4, GW, d), lambda i: (0, 0, 0)),
        pl.BlockSpec((tm, d), row), pl.BlockSpec((1, 6, d), mrow),
        pl.BlockSpec((1, d), lambda i: (0, 0)), pl.BlockSpec((d, 2 * LANE), lambda i: (0, 0))]
    return pl.pallas_call(
        _outproj_kernel,
        grid=(ntot // tm,),
        in_specs=in_specs,
        out_specs=[pl.BlockSpec((tm, d), row), pl.BlockSpec((tm * (d // 2 // LANE), LANE), row),
                   pl.BlockSpec((tm, LANE), row)],
        out_shape=[jax.ShapeDtypeStruct((ntot, d), F32),
                   jax.ShapeDtypeStruct((ntot * (d // 2 // LANE), LANE), jnp.uint32),
                   jax.ShapeDtypeStruct((ntot, LANE), F32)],
        compiler_params=_cparams(("parallel",)),
        name="out_proj_norm2_router",
    )(*ys, w_out4, x, mod, g2, w_router)


def _row_copy(h_hbm, xbuf, sem, tok, slot, r):
    nch = xbuf.shape[1] // MOE_BLOCK
    return pltpu.make_async_copy(h_hbm.at[pl.ds(tok * nch, nch), :], xbuf.at[slot, pl.ds(r * nch, nch), :],
                                 sem.at[slot])


MOE_DMA_GROUP = 32


def _moe_kernel(be_ref, tok_ref, nv_ref, na_ref, h_hbm, wg_ref, wu_ref, wd_ref, o_ref,
                xbuf, sem, wgu_scr, wd_scr):
    i = pl.program_id(0)
    n_active = na_ref[0]
    de = wg_ref.shape[-1]
    slot = i % 2
    nch = xbuf.shape[1] // MOE_BLOCK

    def for_filled_groups(blk, fn):
        for g0 in range(0, MOE_BLOCK, MOE_DMA_GROUP):
            @pl.when(g0 < nv_ref[blk])
            def _():
                for r in range(g0, g0 + MOE_DMA_GROUP):
                    fn(r)

    def start_gather(blk, s):
        for_filled_groups(
            blk, lambda r: _row_copy(h_hbm, xbuf, sem, tok_ref[blk * MOE_BLOCK + r], s, r).start())

    @pl.when(i == 0)
    def _():
        xbuf[...] = jnp.zeros_like(xbuf)
        start_gather(0, 0)

    @pl.when(i < n_active)
    def _():
        @pl.when(jnp.logical_or(i == 0, be_ref[i] != be_ref[jnp.maximum(i - 1, 0)]))
        def _():
            wgu_scr[:, :de] = _bf(wg_ref[0])
            wgu_scr[:, de:] = _bf(wu_ref[0])
            wd_scr[...] = _bf(wd_ref[0])

        for_filled_groups(i, lambda r: _row_copy(h_hbm, xbuf, sem, 0, slot, r).wait())

        @pl.when(i + 1 < n_active)
        def _():
            start_gather(i + 1, 1 - slot)

        lo, hi = [], []
        for j in range(nch):
            wv = xbuf[slot, pl.ds(j, MOE_BLOCK, stride=nch), :]
            lo.append(_bf(lax.bitcast_convert_type(wv << 16, F32)))
            hi.append(_bf(lax.bitcast_convert_type(wv & jnp.uint32(0xFFFF0000), F32)))
        xb = jnp.concatenate(lo + hi, axis=-1)
        gu = _dot(xb, wgu_scr[...])
        g = gu[:, :de]
        hid = (g * _sigmoid(g)) * gu[:, de:]
        o_ref[...] = _bf(_dot(_bf(hid), wd_scr[...]))

    @pl.when(i >= n_active)
    def _():
        o_ref[...] = jnp.zeros_like(o_ref)


def _moe_blocks(blk_exp, slot_tok, blk_fill, n_active, h, wg, wu, wd):
    nslot = slot_tok.shape[0]
    d = wg.shape[1]
    nch = d // 2 // LANE
    de = wg.shape[-1]
    nblk = nslot // MOE_BLOCK
    return pl.pallas_call(
        _moe_kernel,
        grid_spec=pltpu.PrefetchScalarGridSpec(
            num_scalar_prefetch=4,
            grid=(nblk,),
            in_specs=[pl.BlockSpec(memory_space=pl.ANY),
                      pl.BlockSpec((1, d, de), lambda i, be, tk, nv, na: (be[i], 0, 0)),
                      pl.BlockSpec((1, d, de), lambda i, be, tk, nv, na: (be[i], 0, 0)),
                      pl.BlockSpec((1, de, d), lambda i, be, tk, nv, na: (be[i], 0, 0))],
            out_specs=pl.BlockSpec((MOE_BLOCK, d), lambda i, be, tk, nv, na: (i, 0)),
            scratch_shapes=[pltpu.VMEM((2, MOE_BLOCK * nch, LANE), jnp.uint32), pltpu.SemaphoreType.DMA((2,)),
                            pltpu.VMEM((d, 2 * de), BF16), pltpu.VMEM((de, d), BF16)]),
        out_shape=jax.ShapeDtypeStruct((nslot, d), BF16),
        compiler_params=_cparams(("arbitrary",)),
        name="moe_expert_blocks",
    )(blk_exp, slot_tok, blk_fill, n_active, h, wg, wu, wd)


def _route(logits):
    n_tok = logits.shape[0]
    n_exp = N_EXPERT_GROUPS * EXPERTS_PER_GROUP
    g_logits = logits[:, :N_EXPERT_GROUPS]
    g_prob = jax.nn.softmax(g_logits, axis=-1)
    g_sel = jnp.argmax(g_logits, axis=-1)
    g_hot = g_sel[:, None] == jnp.arange(N_EXPERT_GROUPS)[None, :]
    e_logits = logits[:, N_EXPERT_GROUPS:N_EXPERT_GROUPS + n_exp].reshape(
        n_tok, N_EXPERT_GROUPS, EXPERTS_PER_GROUP)
    within = jnp.sum(jnp.where(g_hot[:, :, None], e_logits, 0.0), axis=1)
    top_v, top_i = lax.top_k(within, TOP_K)
    wts = jax.nn.softmax(top_v, axis=-1) * jnp.sum(jnp.where(g_hot, g_prob, 0.0), axis=1, keepdims=True)
    eid = (g_sel[:, None] * EXPERTS_PER_GROUP + top_i).astype(jnp.int32)

    n_assign = n_tok * TOP_K
    flat_e = eid.reshape(-1)
    onehot = (flat_e[:, None] == jnp.arange(n_exp, dtype=jnp.int32)[None, :]).astype(jnp.int32)
    csum = jnp.cumsum(onehot, axis=0)
    counts = csum[-1]
    padded = (counts + MOE_BLOCK - 1) // MOE_BLOCK * MOE_BLOCK
    pend = jnp.cumsum(padded)
    pstart = pend - padded
    dest = jnp.sum(onehot * (csum - 1 + pstart[None, :]), axis=1)
    n_blk = -(-n_assign // MOE_BLOCK) + n_exp
    tok = jnp.arange(n_assign, dtype=jnp.int32) // TOP_K
    slot_tok = jnp.zeros((n_blk * MOE_BLOCK,), jnp.int32).at[dest].set(
        tok, unique_indices=True, mode="promise_in_bounds")
    blk_start = jnp.arange(n_blk, dtype=jnp.int32) * MOE_BLOCK
    blk_exp = jnp.minimum(jnp.sum(blk_start[:, None] >= pend[None, :], axis=1), n_exp - 1).astype(jnp.int32)
    n_active = (pend[-1:] // MOE_BLOCK).astype(jnp.int32)
    blk_fill = jnp.clip(counts[blk_exp] + pstart[blk_exp] - blk_start, 0, MOE_BLOCK).astype(jnp.int32)
    return wts, dest, slot_tok, blk_exp, blk_fill, n_active


def _moe(h2, logits, layer, wg, wu, wd):
    n_tok = logits.shape[0]
    n_exp = wg.shape[1]
    wts, dest, slot_tok, blk_exp, blk_fill, n_active = _route(logits)
    flat = lambda w: w.reshape((w.shape[0] * n_exp,) + w.shape[2:])
    yb = _moe_blocks(blk_exp + layer * n_exp, slot_tok, blk_fill, n_active, h2, flat(wg), flat(wu), flat(wd))
    dest = dest.reshape(n_tok, TOP_K)
    take = functools.partial(jnp.take, axis=0, mode="clip")
    return take(yb, dest[:, 0]), take(yb, dest[:, 1]), wts


def _final_kernel(x_ref, y0_ref, y1_ref, wt_ref, pmod_ref, g_ref, o_ref):
    y = wt_ref[:, 0:1] * y0_ref[...].astype(F32) + wt_ref[:, 1:2] * y1_ref[...].astype(F32)
    x = x_ref[...] + pmod_ref[0, 5:6, :] * y
    o_ref[...] = x * lax.rsqrt(jnp.mean(x * x, axis=-1, keepdims=True) + NORM_EPS) * g_ref[...]


def _final(x, ymoe, pmod, g, *, n_lat, seq, b):
    d = x.shape[1]
    tm = 512
    row = lambda i: (i, 0)
    y0, y1, wts = ymoe
    return pl.pallas_call(
        _final_kernel,
        grid=(n_lat // tm,),
        in_specs=[pl.BlockSpec((tm, d), row), pl.BlockSpec((tm, d), row), pl.BlockSpec((tm, d), row),
                  pl.BlockSpec((tm, TOP_K), row),
                  pl.BlockSpec((1, 6, d), _mod_row_map(tm, n_lat, seq, b)),
                  pl.BlockSpec((1, d), lambda i: (0, 0))],
        out_specs=pl.BlockSpec((tm, d), row),
        out_shape=jax.ShapeDtypeStruct((n_lat, d), F32),
        compiler_params=_cparams(("parallel",)),
        name="final_residual_norm",
    )(x, y0, y1, wts, pmod, g)


def _arrange_w_in(w_in_l):
    d = w_in_l.shape[0]
    hg = w_in_l[:, :5 * GW]
    o = 5 * GW
    ssd_z = w_in_l[:, o:o + GW]
    ssd_xbc = w_in_l[:, o + GW:o + GW + 1024]
    ssd_dt = w_in_l[:, o + GW + 1024:o + GW + 1024 + 2 * SSD_HEADS]
    o += GW + 1024 + 2 * SSD_HEADS
    lru = w_in_l[:, o:o + 2 * GW]
    o += 2 * GW
    ml_main = w_in_l[:, o:o + 4 * GW]
    ml_gates = w_in_l[:, o + 4 * GW:o + 4 * GW + 4 * ML_HEADS]
    small = jnp.concatenate([ssd_dt, ml_gates], axis=1)
    pad = jnp.zeros((d, NCOLS - C_SMALL - small.shape[1]), w_in_l.dtype)
    return jnp.concatenate([hg, ssd_z, ssd_xbc, lru, ml_main, small, pad], axis=1)


def _blockdiag_dense(w):
    n, bw, _ = w.shape
    eye = jnp.eye(n, dtype=w.dtype)
    return (eye[:, None, :, None] * w[:, :, None, :]).reshape(n * bw, n * bw)


def _pad_lanes(v):
    return jnp.concatenate([v, jnp.zeros((LANE - v.shape[0],), v.dtype)])[None, :]


def kernel(x, c, ctx, c_ctx, ada_w, ada_b, norm1_g, norm2_g, w_in, w_out, hg_lb_logits, ssd_conv_w, ssd_a_log, ssd_dt_bias, ssd_d, ssd_norm_g, lru_conv_w, lru_wa, lru_ba, lru_wx, lru_bx, lru_lambda, ml_b_i, ml_b_f, router_group_w, router_expert_w, moe_w_gate, moe_w_up, moe_w_down, final_norm_g):
    nb, seq, d = x.shape
    nctx = ctx.shape[1]
    depth = ada_w.shape[0]
    n_lat = nb * seq
    assert nctx % T_BLOCK == 0 and seq % T_BLOCK == 0
    nc, nl = nctx // T_BLOCK, seq // T_BLOCK
    kw = dict(nb=nb, nc=nc, nl=nl)
    lay = dict(n_lat=n_lat, seq=seq, b=nb)

    p = jax.nn.softmax(hg_lb_logits.astype(F32), axis=0)
    lb_all = (jnp.cumsum(p, axis=0) - p[:1])[:, :, None, :]

    c_all = jnp.concatenate([c, c_ctx[None, :], jnp.zeros((SUB - nb - 1, d), F32)], axis=0)
    mods = _modulation(c_all, ada_w, ada_b).reshape(depth, SUB, 6, d)

    xall = jnp.concatenate([x.reshape(n_lat, d), ctx.reshape(nb * nctx, d)], axis=0)
    y_moe = None
    tn = 512
    for l in range(depth):
        w_l = _bf(_arrange_w_in(w_in[l])).reshape(d, NCOLS // tn, tn).transpose(1, 0, 2)
        xall, h1 = _res_norm(xall, y_moe, mods[l - 1] if l else None, norm1_g[l][None, :], mods[l], **lay)
        u = _in_proj(h1, w_l)

        o = _hg_pass(u, lb_all[l], None, rev=False, **kw)
        y_hg = _hg_pass(u, lb_all[l], o, rev=True, **kw)

        sbias = _pad_lanes(jnp.concatenate([ssd_dt_bias[l, 0], ssd_dt_bias[l, 1],
                                            ml_b_i[l, 0], ml_b_i[l, 1], ml_b_f[l, 0], ml_b_f[l, 1]]))
        arow = _pad_lanes(jnp.concatenate([-jnp.exp(ssd_a_log[l, 0]), -jnp.exp(ssd_a_log[l, 1])]))
        dsk = jnp.repeat(ssd_d[l], SSD_HEADDIM)[None, :]
        o = _ssd_pass(u, sbias, arow, ssd_conv_w[l], None, None, rev=False, **kw)
        y_ssd = _ssd_pass(u, sbias, arow, ssd_conv_w[l], (dsk, ssd_norm_g[l][None, :]), o, rev=True, **kw)

        w2 = _bf(jnp.stack([jnp.concatenate([_blockdiag_dense(lru_wa[l, dd]), _blockdiag_dense(lru_wx[l, dd])],
                                            axis=1) for dd in range(2)]))
        b2 = jnp.stack([jnp.concatenate([lru_ba[l, dd], lru_bx[l, dd]])[None, :] for dd in range(2)])
        crow = (-LRU_C * jax.nn.softplus(-lru_lambda[l]))[:, None, :]
        y_lru = _lru_mixer(u, (lru_conv_w[l], w2, b2, crow), nb=nb, seq=seq, ctx=nctx)

        o = _ml_pass(u, sbias, None, rev=False, **kw)
        y_ml = _ml_pass(u, sbias, o, rev=True, **kw)

        w_router = jnp.concatenate(
            [router_group_w[l], router_expert_w[l],
             jnp.zeros((d, LANE - N_EXPERT_GROUPS * (1 + EXPERTS_PER_GROUP)), F32)], axis=1)
        wr1 = _bf(w_router)
        wr12 = jnp.concatenate([wr1, _bf(w_router - wr1.astype(F32))], axis=1)
        xall, h2, logits = _out_proj((y_hg, y_ssd, y_lru, y_ml), _bf(w_out[l]).reshape(4, GW, d), xall,
                                     mods[l], norm2_g[l][None, :], wr12, **lay)
        y_moe = _moe(h2, logits, l, moe_w_gate, moe_w_up, moe_w_down)

    out = _final(xall, y_moe, mods[depth - 1], final_norm_g[None, :], **lay)
    return out.reshape(nb, seq, d)
```

```python
import functools
import math

import jax
import jax.numpy as jnp
from jax import lax
from jax.experimental import pallas as pl
from jax.experimental.pallas import tpu as pltpu

F32 = jnp.float32
BF16 = jnp.bfloat16

GRID_W = 64
HG_HEADS, HG_DK = 4, 128
SSD_HEADS, SSD_HEADDIM, SSD_GROUPS, SSD_STATE = 8, 64, 2, 128
LRU_C = 8.0
ML_HEADS, ML_DH = 4, 128
N_EXPERT_GROUPS, EXPERTS_PER_GROUP, TOP_K = 4, 8, 2
MOE_BLOCK = 256
NORM_EPS = 1e-6
NEG = -1e30

GW = 512
C_HG_Q, C_HG_I, C_HG_G, C_HG_ZF, C_HG_ZB = 0, 512, 1024, 1536, 2048
C_SSD_Z, C_SSD_XBC = 2560, 3072
C_LRU_GATE, C_LRU_XR = 4096, 4608
C_ML_Q, C_ML_K, C_ML_V, C_ML_O = 5120, 5632, 6144, 6656
C_SMALL = 7168
SM_DT, SM_IG, SM_FG = 0, 16, 24
NCOLS = 7680

LANE = 128
SUB = 8
VMEM_LIMIT = 48 * 1024 * 1024

T_CHUNK = 128
T_BLOCK = 2 * T_CHUNK
HG_T_CHUNK = T_CHUNK
IN_PROJ_TM_MAX = 2816


def _cparams(sem):
    return pltpu.CompilerParams(dimension_semantics=sem, vmem_limit_bytes=VMEM_LIMIT)


def _dot(a, b):
    return jnp.dot(a, b, preferred_element_type=F32)


def _dot_nt(a, b):
    return lax.dot_general(a, b, (((1,), (1,)), ((), ())), preferred_element_type=F32)


def _dot_tn(a, b):
    return lax.dot_general(a, b, (((0,), (0,)), ((), ())), preferred_element_type=F32)


def _bf(x):
    return x.astype(BF16)


def _sigmoid(x):
    return 0.5 * jnp.tanh(0.5 * x) + 0.5


def _softplus(x):
    return jnp.maximum(x, 0.0) + jnp.log1p(jnp.exp(-jnp.abs(x)))


def _split_dot(mask_bf, x):
    w = x.shape[1]
    hi = _bf(x)
    r1 = x - hi.astype(F32)
    mid = _bf(r1)
    lo = _bf(r1 - mid.astype(F32))
    p = _dot(mask_bf, jnp.concatenate([hi, mid, lo], axis=-1))
    return p[:, :w] + p[:, w:2 * w] + p[:, 2 * w:]


def _iota2(t):
    return (lax.broadcasted_iota(jnp.int32, (t, t), 0), lax.broadcasted_iota(jnp.int32, (t, t), 1))


def _mod_kernel(c_ref, w_ref, b_ref, o_ref):
    c = c_ref[...]
    s = c * _sigmoid(c)
    o_ref[0] = jnp.dot(s, w_ref[0], precision=lax.Precision.HIGHEST,
                       preferred_element_type=F32) + b_ref[0]


def _modulation(c_all, ada_w, ada_b):
    depth, d, n6 = ada_w.shape
    tn = 1024
    return pl.pallas_call(
        _mod_kernel,
        grid=(depth, n6 // tn),
        in_specs=[pl.BlockSpec((SUB, d), lambda l, j: (0, 0)),
                  pl.BlockSpec((1, d, tn), lambda l, j: (l, 0, j)),
                  pl.BlockSpec((1, 1, tn), lambda l, j: (l, 0, j))],
        out_specs=pl.BlockSpec((1, SUB, tn), lambda l, j: (l, 0, j)),
        out_shape=jax.ShapeDtypeStruct((depth, SUB, n6), F32),
        compiler_params=_cparams(("parallel", "parallel")),
        name="adaln_modulation",
    )(c_all, ada_w, ada_b.reshape(depth, 1, n6))


def _rms_mod(x, g, shift, scale):
    y = x * lax.rsqrt(jnp.mean(x * x, axis=-1, keepdims=True) + NORM_EPS) * g
    return y * (1.0 + scale) + shift


def _mod_row_map(tm, n_lat, seq, b):
    def f(i, *_):
        return (jnp.where(i * tm < n_lat, (i * tm) // seq, b), 0, 0)
    return f


def _resnorm_kernel(*refs, residual):
    if residual:
        x_ref, y0_ref, y1_ref, wt_ref, pmod_ref, g_ref, mod_ref, xn_ref, h_ref = refs
        y = (wt_ref[:, 0:1] * y0_ref[...].astype(F32) + wt_ref[:, 1:2] * y1_ref[...].astype(F32))
        x = x_ref[...] + pmod_ref[0, 5:6, :] * y
        xn_ref[...] = x
    else:
        x_ref, g_ref, mod_ref, h_ref = refs
        x = x_ref[...]
    h_ref[...] = _bf(_rms_mod(x, g_ref[...], mod_ref[0, 0:1, :], mod_ref[0, 1:2, :]))


def _res_norm(x, ymoe, pmod, g1, mod, *, n_lat, seq, b):
    ntot, d = x.shape
    tm = 512
    residual = ymoe is not None
    row = lambda i: (i, 0)
    mrow = _mod_row_map(tm, n_lat, seq, b)
    in_specs = [pl.BlockSpec((tm, d), row)]
    args = [x]
    if residual:
        y0, y1, wts = ymoe
        in_specs += [pl.BlockSpec((tm, d), row), pl.BlockSpec((tm, d), row),
                     pl.BlockSpec((tm, TOP_K), row), pl.BlockSpec((1, 6, d), mrow)]
        args += [y0, y1, wts, pmod]
    in_specs += [pl.BlockSpec((1, d), lambda i: (0, 0)), pl.BlockSpec((1, 6, d), mrow)]
    args += [g1, mod]
    out_specs = [pl.BlockSpec((tm, d), row)]
    out_shape = [jax.ShapeDtypeStruct((ntot, d), BF16)]
    if residual:
        out_specs = [pl.BlockSpec((tm, d), row)] + out_specs
        out_shape = [jax.ShapeDtypeStruct((ntot, d), F32)] + out_shape
    res = pl.pallas_call(
        functools.partial(_resnorm_kernel, residual=residual),
        grid=(ntot // tm,),
        in_specs=in_specs, out_specs=out_specs, out_shape=out_shape,
        compiler_params=_cparams(("parallel",)),
        name="residual_norm1",
    )(*args)
    return (res[0], res[1]) if residual else (x, res[0])


def _inproj_kernel(h_ref, w_ref, u_ref):
    u_ref[...] = _dot(h_ref[...], w_ref[0])


def _in_proj(h, w_tiles):
    ntot, d = h.shape
    nt, _, tn = w_tiles.shape
    tm = next(c for c in range(IN_PROJ_TM_MAX, 0, -MOE_BLOCK) if ntot % c == 0)
    return pl.pallas_call(
        _inproj_kernel,
        grid=(ntot // tm, nt),
        in_specs=[pl.BlockSpec((tm, d), lambda i, j: (i, 0)),
                  pl.BlockSpec((1, d, tn), lambda i, j: (j, 0, 0))],
        out_specs=pl.BlockSpec((tm, tn), lambda i, j: (i, j)),
        out_shape=jax.ShapeDtypeStruct((ntot, nt * tn), F32),
        compiler_params=_cparams(("parallel", "arbitrary")),
        name="in_proj",
    )(h, w_tiles)


def _chunk_nat(i, rev, nc, nl):
    is_ctx = i < nc
    if rev:
        c = jnp.where(is_ctx, nc - 1 - i, nl - 1 - (i - nc))
    else:
        c = jnp.where(is_ctx, i, i - nc)
    return is_ctx, c


def _row_block(b, i, *, rev, nc, nl, nb):
    is_ctx, c = _chunk_nat(i, rev, nc, nl)
    return jnp.where(is_ctx, nb * nl + b * nc + c, b * nl + c)


def _col_spec(t, width, col, rb):
    cb = col // width
    return pl.BlockSpec((t, width), lambda b, i: (rb(b, i), cb))


def _hg_kernel(*refs, rev, finish):
    if finish:
        q_ref, v_ref, z_ref, lb_ref, g_ref, of_ref, o_ref, st_scr = refs
    else:
        q_ref, v_ref, z_ref, lb_ref, o_ref, st_scr = refs
    t = HG_T_CHUNK
    dk = HG_DK
    w = q_ref.shape[1]

    @pl.when(pl.program_id(1) == 0)
    def _():
        st_scr[...] = jnp.zeros_like(st_scr)

    lb = lb_ref[0]
    row, col = _iota2(t)
    valid = (col >= row) if rev else (col <= row)
    lvl = jnp.where(valid, 31 - lax.clz(row ^ col), -2)
    valid_bf = _bf(valid.astype(F32))
    tpos = lax.broadcasted_iota(jnp.int32, (t, w), 0)
    r8 = lax.broadcasted_iota(jnp.int32, (t // SUB, SUB, w), 1)
    for rs in _sub_chunks(q_ref.shape[0], rev, t):
        _hg_chunk(rs, rev, finish, refs, lb, valid_bf, lvl, tpos, r8)


def _sub_chunks(block_rows, rev, t=T_CHUNK):
    n = block_rows // t
    order = range(n - 1, -1, -1) if rev else range(n)
    return [slice(i * t, (i + 1) * t) for i in order]


def _hg_chunk(rs, rev, finish, refs, lb, valid_bf, lvl, tpos, r8):
    if finish:
        q_ref, v_ref, z_ref, lb_ref, g_ref, of_ref, o_ref, st_scr = refs
    else:
        q_ref, v_ref, z_ref, lb_ref, o_ref, st_scr = refs
    t = HG_T_CHUNK
    dk = HG_DK
    w = q_ref.shape[1]
    sig = _sigmoid(z_ref[rs, :])
    f = lb + (1.0 - lb) * sig
    lf = jnp.log(f)
    k = (1.0 - lb) * (1.0 - sig)
    q = q_ref[rs, :]
    v = v_ref[rs, :]

    b = _split_dot(valid_bf, lf)
    total = jnp.sum(lf, axis=0, keepdims=True)

    b8 = b.reshape(t // SUB, SUB, w)
    ops = [(-1, _bf(q), _bf(k))]
    h = 1
    while h < t:
        blk = 2 * h
        lg = int(math.log2(h))
        is_q = ((tpos & h) == 0) if rev else ((tpos & h) != 0)
        if h == 1:
            e = f
        else:
            off = h if rev else h - 1
            if blk <= SUB:
                ref = b8[:, off:off + 1, :]
                for sb in range(1, SUB // blk):
                    ref = jnp.where(r8 >= sb * blk, b8[:, sb * blk + off:sb * blk + off + 1, :], ref)
                ref = jnp.broadcast_to(ref, b8.shape).reshape(t, w)
            else:
                b3 = b.reshape(t // blk, blk, w)
                ref = jnp.broadcast_to(b3[:, off:off + 1, :], b3.shape).reshape(t, w)
            e = jnp.exp(-jnp.abs(b - ref))
        if h == 1:
            m = _bf(jnp.where(is_q, q * e, k))
        else:
            m = _bf(jnp.where(is_q, q, k) * e)
        ops.append((lg, m, m))
        h = blk

    qb = _bf(q * jnp.exp(b))
    kw = _bf(k * jnp.exp(total - b))
    vb = _bf(v)
    dec_tot = jnp.exp(total)

    masks = [lvl == lg for lg, _, _ in ops]
    outs = []
    for hd in range(HG_HEADS):
        sl = slice(hd * dk, (hd + 1) * dk)
        a = jnp.zeros((t, t), F32)
        for msk, (lg, mq, mk) in zip(masks, ops):
            a = jnp.where(msk, _dot_nt(mq[:, sl], mk[:, sl]), a)
        st = st_scr[hd]
        o = _dot(_bf(a), vb[:, sl]) + _dot_nt(qb[:, sl], _bf(st))
        st_scr[hd] = st * dec_tot[:, sl] + _dot_tn(vb[:, sl], kw[:, sl])
        outs.append(o)

    if finish:
        for hd in range(HG_HEADS):
            sl = slice(hd * dk, (hd + 1) * dk)
            tot = outs[hd] + of_ref[rs, sl]
            y = tot * lax.rsqrt(jnp.mean(tot * tot, axis=-1, keepdims=True) + NORM_EPS)
            g = g_ref[rs, sl]
            o_ref[rs, sl] = (y * (g * _sigmoid(g))).astype(o_ref.dtype)
    else:
        for hd in range(HG_HEADS):
            o_ref[rs, hd * dk:(hd + 1) * dk] = outs[hd]


def _hg_pass(u, lb, o_f, *, rev, nb, nc, nl):
    t = T_BLOCK
    ntot = u.shape[0]
    finish = o_f is not None
    rb = functools.partial(_row_block, rev=rev, nc=nc, nl=nl, nb=nb)
    d = 1 if rev else 0
    in_specs = [_col_spec(t, GW, C_HG_Q, rb), _col_spec(t, GW, C_HG_I, rb),
                _col_spec(t, GW, C_HG_ZB if rev else C_HG_ZF, rb),
                pl.BlockSpec((1, 1, GW), lambda b, i: (d, 0, 0))]
    args = [u, u, u, lb]
    if finish:
        in_specs += [_col_spec(t, GW, C_HG_G, rb), _col_spec(t, GW, 0, rb)]
        args += [u, o_f]
    return pl.pallas_call(
        functools.partial(_hg_kernel, rev=rev, finish=finish),
        grid=(nb, nc + nl),
        in_specs=in_specs,
        out_specs=_col_spec(t, GW, 0, rb),
        out_shape=jax.ShapeDtypeStruct((ntot, GW), BF16 if finish else F32),
        scratch_shapes=[pltpu.VMEM((HG_HEADS, HG_DK, HG_DK), F32)],
        compiler_params=_cparams(("parallel", "arbitrary")),
        name="hgrn2_bwd_finish" if rev else "hgrn2_fwd",
    )(*args)


def _ml_kernel(*refs, rev, finish):
    ct_scr, m_scr = refs[-2:]
    t = T_CHUNK

    @pl.when(pl.program_id(1) == 0)
    def _():
        ct_scr[...] = jnp.zeros_like(ct_scr)
        m_scr[...] = jnp.full(m_scr.shape, NEG, F32)

    row, col = _iota2(t)
    valid = (col >= row) if rev else (col <= row)
    valid_bf = _bf(valid.astype(F32))
    for rs in _sub_chunks(refs[0].shape[0], rev):
        _ml_chunk(rs, rev, finish, refs, valid, valid_bf)


def _ml_chunk(rs, rev, finish, refs, valid, valid_bf):
    if finish:
        q_ref, k_ref, v_ref, sm_ref, sb_ref, og_ref, of_ref, o_ref, ct_scr, m_scr = refs
    else:
        q_ref, k_ref, v_ref, sm_ref, sb_ref, o_ref, ct_scr, m_scr = refs
    t = T_CHUNK
    dh = ML_DH
    d = 1 if rev else 0

    sm = sm_ref[rs, :] + sb_ref[...]
    lsg = jnp.minimum(sm, 0.0) - jnp.log1p(jnp.exp(-jnp.abs(sm)))
    bcol = _split_dot(valid_bf, lsg)
    tot_row = jnp.sum(lsg, axis=0, keepdims=True)
    b_t = bcol.T
    sm_t = sm.T

    ks = k_ref[rs, :] * (dh ** -0.5)
    qb, kb, vb = _bf(q_ref[rs, :]), _bf(ks), _bf(v_ref[rs, :])
    ones = jnp.ones((t, dh), BF16)

    for hd in range(ML_HEADS):
        sl = slice(hd * dh, (hd + 1) * dh)
        ci = SM_IG + ML_HEADS * d + hd
        cf = SM_FG + ML_HEADS * d + hd
        bc = bcol[:, cf:cf + 1]
        br = b_t[cf:cf + 1, :]
        lir = sm_t[ci:ci + 1, :]
        lic = sm[:, ci:ci + 1]
        m_prev = m_scr[hd:hd + 1, 0:1]
        dmat = jnp.where(valid, bc - br + lir, NEG)
        g = bc + m_prev
        mt = jnp.maximum(g, jnp.max(dmat, axis=-1, keepdims=True))
        w_intra = jnp.exp(dmat - mt)
        w_inter = jnp.exp(g - mt)
        s = _dot_nt(qb[:, sl], kb[:, sl]) * w_intra
        vaug = jnp.concatenate([vb[:, sl], ones], axis=-1)
        ct = ct_scr[hd]
        nd = _dot(_bf(s), vaug) + w_inter * _dot_nt(qb[:, sl], _bf(ct))
        hout = nd[:, :dh] / jnp.maximum(jnp.abs(nd[:, dh:]), jnp.exp(-mt))
        tot = tot_row[:, cf:cf + 1]
        dl = tot - bc + lic
        m_new = jnp.maximum(tot + m_prev, jnp.max(dl, axis=0, keepdims=True))
        ws = jnp.exp(dl - m_new)
        wc = jnp.exp(tot + m_prev - m_new)
        ct_scr[hd] = wc * ct + _dot_tn(vaug, _bf(ks[:, sl] * ws))
        m_scr[hd:hd + 1, :] = jnp.broadcast_to(m_new, (1, m_scr.shape[1]))
        if finish:
            tot_o = hout + of_ref[rs, sl]
            y = tot_o * lax.rsqrt(jnp.mean(tot_o * tot_o, axis=-1, keepdims=True) + NORM_EPS)
            o_ref[rs, sl] = (y * _sigmoid(og_ref[rs, sl])).astype(o_ref.dtype)
        else:
            o_ref[rs, sl] = hout


def _ml_pass(u, sbias, o_f, *, rev, nb, nc, nl):
    t = T_BLOCK
    ntot = u.shape[0]
    finish = o_f is not None
    rb = functools.partial(_row_block, rev=rev, nc=nc, nl=nl, nb=nb)
    in_specs = [_col_spec(t, GW, C_ML_Q, rb), _col_spec(t, GW, C_ML_K, rb),
                _col_spec(t, GW, C_ML_V, rb), _col_spec(t, LANE, C_SMALL, rb),
                pl.BlockSpec((1, LANE), lambda b, i: (0, 0))]
    args = [u, u, u, u, sbias]
    if finish:
        in_specs += [_col_spec(t, GW, C_ML_O, rb), _col_spec(t, GW, 0, rb)]
        args += [u, o_f]
    return pl.pallas_call(
        functools.partial(_ml_kernel, rev=rev, finish=finish),
        grid=(nb, nc + nl),
        in_specs=in_specs,
        out_specs=_col_spec(t, GW, 0, rb),
        out_shape=jax.ShapeDtypeStruct((ntot, GW), BF16 if finish else F32),
        scratch_shapes=[pltpu.VMEM((ML_HEADS, 2 * ML_DH, ML_DH), F32),
                        pltpu.VMEM((SUB, LANE), F32)],
        compiler_params=_cparams(("parallel", "arbitrary")),
        name="mlstm_bwd_finish" if rev else "mlstm_fwd",
    )(*args)


def _conv4(xe_scr, x, prev8, next8, w, first, last):
    t = x.shape[0]
    xe_scr[0:SUB, :] = jnp.where(first, 0.0, prev8)
    xe_scr[SUB:SUB + t, :] = x
    xe_scr[SUB + t:SUB + t + SUB, :] = jnp.where(last, 0.0, next8)
    out = w[0:1, :] * xe_scr[SUB - 2:SUB - 2 + t, :]
    for j in range(1, 4):
        out = out + w[j:j + 1, :] * xe_scr[SUB - 2 + j:SUB - 2 + j + t, :]
    return out


def _ssd_kernel(*refs, rev, finish, nc, nl):
    x_ref, xp_ref, xn_ref = refs[:3]
    cw_ref = refs[6]
    st_scr, xe_scr, xc_scr = refs[-3:]
    t = T_CHUNK
    i = pl.program_id(1)

    @pl.when(i == 0)
    def _():
        st_scr[...] = jnp.zeros_like(st_scr)

    is_ctx, c = _chunk_nat(i, rev, nc, nl)
    first = c == 0
    last = c == jnp.where(is_ctx, nc, nl) - 1
    xc = _conv4(xe_scr, x_ref[...], xp_ref[...], xn_ref[...], cw_ref[...], first, last)
    xc_scr[...] = xc * _sigmoid(xc)

    row, col = _iota2(t)
    valid = (col >= row) if rev else (col <= row)
    valid_bf = _bf(valid.astype(F32))
    for rs in _sub_chunks(x_ref.shape[0], rev):
        _ssd_chunk(rs, rev, finish, refs, valid, valid_bf)


def _ssd_chunk(rs, rev, finish, refs, valid, valid_bf):
    if finish:
        (x_ref, xp_ref, xn_ref, sm_ref, sb_ref, arow_ref, cw_ref,
         z_ref, of_ref, dsk_ref, ng_ref, o_ref, st_scr, xe_scr, xc_scr) = refs
    else:
        x_ref, xp_ref, xn_ref, sm_ref, sb_ref, arow_ref, cw_ref, o_ref, st_scr, xe_scr, xc_scr = refs
    p = SSD_HEADDIM
    n = SSD_STATE
    d = 1 if rev else 0
    xs = xc_scr[rs, :SSD_HEADS * p]
    bm = xc_scr[rs, SSD_HEADS * p:SSD_HEADS * p + SSD_GROUPS * n]
    cm = xc_scr[rs, SSD_HEADS * p + SSD_GROUPS * n:]

    dl = _softplus(sm_ref[rs, :] + sb_ref[...])
    lfm = dl * arow_ref[...]
    bcol = _split_dot(valid_bf, lfm)
    tot_row = jnp.sum(lfm, axis=0, keepdims=True)
    b_t = bcol.T
    dl_t = dl.T

    xsb = _bf(xs)
    rep = SSD_HEADS // SSD_GROUPS
    outs = []
    for gi in range(SSD_GROUPS):
        bg = bm[:, gi * n:(gi + 1) * n]
        cgb = _bf(cm[:, gi * n:(gi + 1) * n])
        gmat = _dot_nt(cgb, _bf(bg))
        for hh in range(rep):
            hd = gi * rep + hh
            cc = SM_DT + SSD_HEADS * d + hd
            bc = bcol[:, cc:cc + 1]
            br = b_t[cc:cc + 1, :]
            dr = dl_t[cc:cc + 1, :]
            dcol = dl[:, cc:cc + 1]
            tot = tot_row[:, cc:cc + 1]
            att = gmat * jnp.exp(jnp.where(valid, bc - br, NEG)) * dr
            xh = xsb[:, hd * p:(hd + 1) * p]
            st = st_scr[hd]
            o = _dot(_bf(att), xh) + jnp.exp(bc) * _dot_nt(cgb, _bf(st))
            wk = dcol * jnp.exp(tot - bc)
            st_scr[hd] = jnp.exp(tot) * st + _dot_tn(xh, _bf(bg * wk))
            outs.append(o)
    o_all = jnp.concatenate(outs, axis=-1)

    if finish:
        y = (o_all + of_ref[rs, :] + dsk_ref[...] * xs)
        z = z_ref[rs, :]
        y = y * (z * _sigmoid(z))
        gw = SSD_HEADS * p // SSD_GROUPS
        for gi in range(SSD_GROUPS):
            sl = slice(gi * gw, (gi + 1) * gw)
            yg = y[:, sl]
            yg = yg * lax.rsqrt(jnp.mean(yg * yg, axis=-1, keepdims=True) + NORM_EPS)
            o_ref[rs, sl] = (yg * ng_ref[:, sl]).astype(o_ref.dtype)
    else:
        o_ref[rs, :] = o_all


def _ssd_pass(u, sbias, arow, conv_w, z_extra, o_f, *, rev, nb, nc, nl):
    t = T_BLOCK
    ntot = u.shape[0]
    finish = o_f is not None
    rb = functools.partial(_row_block, rev=rev, nc=nc, nl=nl, nb=nb)
    r8 = t // SUB
    nblk8 = ntot // SUB
    cb = C_SSD_XBC // 1024

    def prev_map(b, i):
        return (jnp.maximum(rb(b, i) * r8 - 1, 0), cb)

    def next_map(b, i):
        return (jnp.minimum((rb(b, i) + 1) * r8, nblk8 - 1), cb)

    const2 = lambda b, i: (0, 0)
    in_specs = [_col_spec(t, 1024, C_SSD_XBC, rb),
                pl.BlockSpec((SUB, 1024), prev_map), pl.BlockSpec((SUB, 1024), next_map),
                _col_spec(t, LANE, C_SMALL, rb),
                pl.BlockSpec((1, LANE), const2), pl.BlockSpec((1, LANE), const2),
                pl.BlockSpec((4, 1024), const2)]
    args = [u, u, u, u, sbias, arow, conv_w]
    if finish:
        dsk, ng = z_extra
        in_specs += [_col_spec(t, GW, C_SSD_Z, rb), _col_spec(t, GW, 0, rb),
                     pl.BlockSpec((1, GW), const2), pl.BlockSpec((1, GW), const2)]
        args += [u, o_f, dsk, ng]
    return pl.pallas_call(
        functools.partial(_ssd_kernel, rev=rev, finish=finish, nc=nc, nl=nl),
        grid=(nb, nc + nl),
        in_specs=in_specs,
        out_specs=_col_spec(t, GW, 0, rb),
        out_shape=jax.ShapeDtypeStruct((ntot, GW), BF16 if finish else F32),
        scratch_shapes=[pltpu.VMEM((SSD_HEADS, SSD_HEADDIM, SSD_STATE), F32),
                        pltpu.VMEM((t + 2 * SUB, 1024), F32), pltpu.VMEM((t, 1024), F32)],
        compiler_params=_cparams(("parallel", "arbitrary")),
        name="ssd_bwd_finish" if rev else "ssd_fwd",
    )(*args)


def _lru_scan(xr, w2, b2, crow, h, rev):
    t, w = xr.shape
    ng = t // SUB
    pre = _dot(_bf(xr), w2) + b2
    r = _sigmoid(pre[:, :w])
    ig = _sigmoid(pre[:, w:])
    log_a = crow * r
    a = jnp.exp(log_a)
    y = jnp.tanh(-log_a) * (1.0 + a * a)
    bb = jnp.where(y > 0.0, y * lax.rsqrt(y), 0.0) * (ig * xr)

    a3 = a.reshape(ng, SUB, w)
    b3 = bb.reshape(ng, SUB, w)
    r8 = lax.broadcasted_iota(jnp.int32, (ng, SUB, w), 1)
    pos = (SUB - 1 - r8) if rev else r8
    step = 1
    while step < SUB:
        sh = (SUB - step) if rev else step
        a_s = pltpu.roll(a3, sh, 1)
        b_s = pltpu.roll(b3, sh, 1)
        ok = pos >= step
        b3 = jnp.where(ok, a3 * b_s + b3, b3)
        a3 = jnp.where(ok, a3 * a_s, a3)
        step *= 2
    groups = [None] * ng
    for g in (range(ng - 1, -1, -1) if rev else range(ng)):
        hg = b3[g] + a3[g] * h
        groups[g] = hg
        h = hg[0:1, :] if rev else hg[SUB - 1:SUB, :]
    return jnp.concatenate(groups, axis=0), h


def _gelu_tanh(g):
    return 0.5 * g * (1.0 + jnp.tanh(math.sqrt(2.0 / math.pi) * (g + 0.044715 * (g * g * g))))


LRU_COLS_PER_STEP = SUB


def _lru_kernel(*refs, rev, finish, latent, nsteps):
    if finish:
        (x_ref, xp_ref, xn_ref, cw_ref, w2_ref, b2_ref, crow_ref, h0_ref,
         gate_ref, of_ref, o_ref, ht_ref, h_scr, xe_scr) = refs
    else:
        x_ref, xp_ref, xn_ref, cw_ref, w2_ref, b2_ref, crow_ref, h0_ref, o_ref, ht_ref, h_scr, xe_scr = refs
    i = pl.program_id(1)

    @pl.when(i == 0)
    def _():
        h_scr[...] = h0_ref[0]

    c = (nsteps - 1 - i) if rev else i
    cw, w2, b2, crow = cw_ref[...], w2_ref[0], b2_ref[0], crow_ref[0]
    h = h_scr[...]
    if not latent:
        xr = _conv4(xe_scr, x_ref[...], xp_ref[...], xn_ref[...], cw, c == 0, c == nsteps - 1)
        hs, h = _lru_scan(xr, w2, b2, crow, h, rev)
        if finish:
            o_ref[...] = ((hs + of_ref[...]) * _gelu_tanh(gate_ref[...])).astype(o_ref.dtype)
        else:
            o_ref[...] = hs
    else:
        rows, ncol = x_ref.shape[0], x_ref.shape[1]
        for cl in (range(ncol - 1, -1, -1) if rev else range(ncol)):
            prev8 = x_ref[rows - SUB:rows, cl - 1, :] if cl > 0 else xp_ref[:, ncol - 1, :]
            next8 = x_ref[0:SUB, cl + 1, :] if cl < ncol - 1 else xn_ref[:, 0, :]
            first = (c == 0) if cl == 0 else False
            last = (c == nsteps - 1) if cl == ncol - 1 else False
            xr = _conv4(xe_scr, x_ref[:, cl, :], prev8, next8, cw, first, last)
            hs, h = _lru_scan(xr, w2, b2, crow, h, rev)
            if finish:
                o_ref[:, cl, :] = ((hs + of_ref[:, cl, :]) * _gelu_tanh(gate_ref[:, cl, :])).astype(o_ref.dtype)
            else:
                o_ref[:, cl, :] = hs
    h_scr[...] = h

    @pl.when(i == nsteps - 1)
    def _():
        ht_ref[0] = h


def _lru_pass(arr, views, consts, h0, o_f, *, rev, nb, nsteps, latent, out_shape, rows=None):
    finish = o_f is not None
    d = 1 if rev else 0
    main_map, prev_map, next_map, gate_map, out_map = views
    cw, w2, b2, crow = consts
    if latent:
        blk, halo = (rows, LRU_COLS_PER_STEP, GW), (SUB, LRU_COLS_PER_STEP, GW)
        t = rows
    else:
        blk, halo = (T_BLOCK, GW), (SUB, GW)
        t = T_BLOCK
    const2 = lambda b, i: (0, 0)
    dsel = lambda b, i: (d, 0, 0)
    hsel = lambda b, i: (b, 0, 0)
    in_specs = [pl.BlockSpec(blk, main_map), pl.BlockSpec(halo, prev_map), pl.BlockSpec(halo, next_map),
                pl.BlockSpec((4, GW), const2),
                pl.BlockSpec((1, GW, 2 * GW), dsel), pl.BlockSpec((1, 1, 2 * GW), dsel),
                pl.BlockSpec((1, 1, GW), dsel), pl.BlockSpec((1, 1, GW), hsel)]
    args = [arr, arr, arr, cw, w2, b2, crow, h0]
    if finish:
        in_specs += [pl.BlockSpec(blk, gate_map), pl.BlockSpec(blk, out_map)]
        args += [arr, o_f]
    seg = "lat" if latent else "ctx"
    return pl.pallas_call(
        functools.partial(_lru_kernel, rev=rev, finish=finish, latent=latent, nsteps=nsteps),
        grid=(nb, nsteps),
        in_specs=in_specs,
        out_specs=[pl.BlockSpec(blk, out_map), pl.BlockSpec((1, 1, GW), hsel)],
        out_shape=[jax.ShapeDtypeStruct(out_shape, BF16 if (finish and not latent) else F32),
                   jax.ShapeDtypeStruct((nb, 1, GW), F32)],
        scratch_shapes=[pltpu.VMEM((1, GW), F32), pltpu.VMEM((t + 2 * SUB, GW), F32)],
        compiler_params=_cparams(("parallel", "arbitrary")),
        name=f"rglru_{seg}_bwd_finish" if rev else f"rglru_{seg}_fwd",
    )(*args)


def _lru_mixer(u, consts, *, nb, seq, ctx):
    ntot, ncols = u.shape
    n_lat = nb * seq
    rows = seq // GRID_W
    ncg = GRID_W // LRU_COLS_PER_STEP
    ncb = ctx // T_BLOCK
    xr_cb, gate_cb = C_LRU_XR // GW, C_LRU_GATE // GW
    u3 = u.reshape(ntot // GRID_W, GRID_W, ncols)
    r8, c8 = rows // SUB, T_BLOCK // SUB
    base, base8 = n_lat // T_BLOCK, n_lat // SUB
    h = jnp.zeros((nb, 1, GW), F32)
    o_c = o_l = None
    for rev in (False, True):
        nat_c = (lambda i: ncb - 1 - i) if rev else (lambda i: i)
        nat_l = (lambda i: ncg - 1 - i) if rev else (lambda i: i)
        cviews = (lambda b, i: (base + b * ncb + nat_c(i), xr_cb),
                  lambda b, i: (jnp.maximum(base8 + (b * ncb + nat_c(i)) * c8 - 1, 0), xr_cb),
                  lambda b, i: (jnp.minimum(base8 + (b * ncb + nat_c(i) + 1) * c8, ntot // SUB - 1), xr_cb),
                  lambda b, i: (base + b * ncb + nat_c(i), gate_cb),
                  lambda b, i: (b * ncb + nat_c(i), 0))
        o_c, h_c = _lru_pass(u, cviews, consts, h, o_c, rev=rev, nb=nb, nsteps=ncb, latent=False,
                             out_shape=(nb * ctx, GW))
        lviews = (lambda b, i: (b, nat_l(i), xr_cb),
                  lambda b, i: (b * r8 + r8 - 1, jnp.maximum(nat_l(i) - 1, 0), xr_cb),
                  lambda b, i: (b * r8, jnp.minimum(nat_l(i) + 1, ncg - 1), xr_cb),
                  lambda b, i: (b, nat_l(i), gate_cb),
                  lambda b, i: (b, nat_l(i), 0))
        o_l, _ = _lru_pass(u3, lviews, consts, h_c, o_l, rev=rev, nb=nb, nsteps=ncg, latent=True,
                           out_shape=(n_lat // GRID_W, GRID_W, GW), rows=rows)
    return jnp.concatenate([_bf(o_l).reshape(n_lat, GW), o_c], axis=0)


def _outproj_kernel(yh_ref, ys_ref, yl_ref, ym_ref, w_ref, x_ref, mod_ref, g_ref, wr_ref,
                    xn_ref, h_ref, lg_ref):
    acc = _dot(yh_ref[...], w_ref[0])
    acc = acc + _dot(ys_ref[...], w_ref[1])
    acc = acc + _dot(yl_ref[...], w_ref[2])
    acc = acc + _dot(ym_ref[...], w_ref[3])
    xn = x_ref[...] + mod_ref[0, 2:3, :] * acc
    xn_ref[...] = xn
    h = _rms_mod(xn, g_ref[...], mod_ref[0, 3:4, :], mod_ref[0, 4:5, :])
    h1 = _bf(h)
    tm, d = h.shape
    bits = lax.bitcast_convert_type(h1.astype(F32), jnp.uint32)
    packed = (bits[:, :d // 2] >> 16) | (bits[:, d // 2:] & jnp.uint32(0xFFFF0000))
    nch = d // 2 // LANE
    for j in range(nch):
        h_ref[pl.ds(j, tm, stride=nch), :] = packed[:, j * LANE:(j + 1) * LANE]
    h2 = _bf(h - h1.astype(F32))
    p = _dot(h1, wr_ref[...])
    lg_ref[...] = p[:, :LANE] + p[:, LANE:] + _dot(h2, wr_ref[:, :LANE])


def _out_proj(ys, w_out4, x, mod, g2, w_router, *, n_lat, seq, b):
    ntot, d = x.shape
    tm = 512
    row = lambda i: (i, 0)
    mrow = _mod_row_map(tm, n_lat, seq, b)
    in_specs = [pl.BlockSpec((tm, GW), row)] * 4 + [
        pl.BlockSpec((4, GW, d), lambda i: (0, 0, 0)),
        pl.BlockSpec((tm, d), row), pl.BlockSpec((1, 6, d), mrow),
        pl.BlockSpec((1, d), lambda i: (0, 0)), pl.BlockSpec((d, 2 * LANE), lambda i: (0, 0))]
    return pl.pallas_call(
        _outproj_kernel,
        grid=(ntot // tm,),
        in_specs=in_specs,
        out_specs=[pl.BlockSpec((tm, d), row), pl.BlockSpec((tm * (d // 2 // LANE), LANE), row),
                   pl.BlockSpec((tm, LANE), row)],
        out_shape=[jax.ShapeDtypeStruct((ntot, d), F32),
                   jax.ShapeDtypeStruct((ntot * (d // 2 // LANE), LANE), jnp.uint32),
                   jax.ShapeDtypeStruct((ntot, LANE), F32)],
        compiler_params=_cparams(("parallel",)),
        name="out_proj_norm2_router",
    )(*ys, w_out4, x, mod, g2, w_router)


def _row_copy(h_hbm, xbuf, sem, tok, slot, r):
    nch = xbuf.shape[1] // MOE_BLOCK
    return pltpu.make_async_copy(h_hbm.at[pl.ds(tok * nch, nch), :], xbuf.at[slot, pl.ds(r * nch, nch), :],
                                 sem.at[slot])


MOE_DMA_GROUP = 32


def _moe_kernel(be_ref, tok_ref, nv_ref, na_ref, h_hbm, wg_ref, wu_ref, wd_ref, o_ref,
                xbuf, sem, wgu_scr, wd_scr):
    i = pl.program_id(0)
    n_active = na_ref[0]
    de = wg_ref.shape[-1]
    slot = i % 2
    nch = xbuf.shape[1] // MOE_BLOCK

    def for_filled_groups(blk, fn):
        for g0 in range(0, MOE_BLOCK, MOE_DMA_GROUP):
            @pl.when(g0 < nv_ref[blk])
            def _():
                for r in range(g0, g0 + MOE_DMA_GROUP):
                    fn(r)

    def start_gather(blk, s):
        for_filled_groups(
            blk, lambda r: _row_copy(h_hbm, xbuf, sem, tok_ref[blk * MOE_BLOCK + r], s, r).start())

    @pl.when(i == 0)
    def _():
        xbuf[...] = jnp.zeros_like(xbuf)
        start_gather(0, 0)

    @pl.when(i < n_active)
    def _():
        @pl.when(jnp.logical_or(i == 0, be_ref[i] != be_ref[jnp.maximum(i - 1, 0)]))
        def _():
            wgu_scr[:, :de] = _bf(wg_ref[0])
            wgu_scr[:, de:] = _bf(wu_ref[0])
            wd_scr[...] = _bf(wd_ref[0])

        for_filled_groups(i, lambda r: _row_copy(h_hbm, xbuf, sem, 0, slot, r).wait())

        @pl.when(i + 1 < n_active)
        def _():
            start_gather(i + 1, 1 - slot)

        lo, hi = [], []
        for j in range(nch):
            wv = xbuf[slot, pl.ds(j, MOE_BLOCK, stride=nch), :]
            lo.append(_bf(lax.bitcast_convert_type(wv << 16, F32)))
            hi.append(_bf(lax.bitcast_convert_type(wv & jnp.uint32(0xFFFF0000), F32)))
        xb = jnp.concatenate(lo + hi, axis=-1)
        gu = _dot(xb, wgu_scr[...])
        g = gu[:, :de]
        hid = (g * _sigmoid(g)) * gu[:, de:]
        o_ref[...] = _bf(_dot(_bf(hid), wd_scr[...]))

    @pl.when(i >= n_active)
    def _():
        o_ref[...] = jnp.zeros_like(o_ref)


def _moe_blocks(blk_exp, slot_tok, blk_fill, n_active, h, wg, wu, wd):
    nslot = slot_tok.shape[0]
    d = wg.shape[1]
    nch = d // 2 // LANE
    de = wg.shape[-1]
    nblk = nslot // MOE_BLOCK
    return pl.pallas_call(
        _moe_kernel,
        grid_spec=pltpu.PrefetchScalarGridSpec(
            num_scalar_prefetch=4,
            grid=(nblk,),
            in_specs=[pl.BlockSpec(memory_space=pl.ANY),
                      pl.BlockSpec((1, d, de), lambda i, be, tk, nv, na: (be[i], 0, 0)),
                      pl.BlockSpec((1, d, de), lambda i, be, tk, nv, na: (be[i], 0, 0)),
                      pl.BlockSpec((1, de, d), lambda i, be, tk, nv, na: (be[i], 0, 0))],
            out_specs=pl.BlockSpec((MOE_BLOCK, d), lambda i, be, tk, nv, na: (i, 0)),
            scratch_shapes=[pltpu.VMEM((2, MOE_BLOCK * nch, LANE), jnp.uint32), pltpu.SemaphoreType.DMA((2,)),
                            pltpu.VMEM((d, 2 * de), BF16), pltpu.VMEM((de, d), BF16)]),
        out_shape=jax.ShapeDtypeStruct((nslot, d), BF16),
        compiler_params=_cparams(("arbitrary",)),
        name="moe_expert_blocks",
    )(blk_exp, slot_tok, blk_fill, n_active, h, wg, wu, wd)


def _route(logits):
    n_tok = logits.shape[0]
    n_exp = N_EXPERT_GROUPS * EXPERTS_PER_GROUP
    g_logits = logits[:, :N_EXPERT_GROUPS]
    g_prob = jax.nn.softmax(g_logits, axis=-1)
    g_sel = jnp.argmax(g_logits, axis=-1)
    g_hot = g_sel[:, None] == jnp.arange(N_EXPERT_GROUPS)[None, :]
    e_logits = logits[:, N_EXPERT_GROUPS:N_EXPERT_GROUPS + n_exp].reshape(
        n_tok, N_EXPERT_GROUPS, EXPERTS_PER_GROUP)
    within = jnp.sum(jnp.where(g_hot[:, :, None], e_logits, 0.0), axis=1)
    top_v, top_i = lax.top_k(within, TOP_K)
    wts = jax.nn.softmax(top_v, axis=-1) * jnp.sum(jnp.where(g_hot, g_prob, 0.0), axis=1, keepdims=True)
    eid = (g_sel[:, None] * EXPERTS_PER_GROUP + top_i).astype(jnp.int32)

    n_assign = n_tok * TOP_K
    flat_e = eid.reshape(-1)
    onehot = (flat_e[:, None] == jnp.arange(n_exp, dtype=jnp.int32)[None, :]).astype(jnp.int32)
    csum = jnp.cumsum(onehot, axis=0)
    counts = csum[-1]
    padded = (counts + MOE_BLOCK - 1) // MOE_BLOCK * MOE_BLOCK
    pend = jnp.cumsum(padded)
    pstart = pend - padded
    dest = jnp.sum(onehot * (csum - 1 + pstart[None, :]), axis=1)
    n_blk = -(-n_assign // MOE_BLOCK) + n_exp
    tok = jnp.arange(n_assign, dtype=jnp.int32) // TOP_K
    slot_tok = jnp.zeros((n_blk * MOE_BLOCK,), jnp.int32).at[dest].set(
        tok, unique_indices=True, mode="promise_in_bounds")
    blk_start = jnp.arange(n_blk, dtype=jnp.int32) * MOE_BLOCK
    blk_exp = jnp.minimum(jnp.sum(blk_start[:, None] >= pend[None, :], axis=1), n_exp - 1).astype(jnp.int32)
    n_active = (pend[-1:] // MOE_BLOCK).astype(jnp.int32)
    blk_fill = jnp.clip(counts[blk_exp] + pstart[blk_exp] - blk_start, 0, MOE_BLOCK).astype(jnp.int32)
    return wts, dest, slot_tok, blk_exp, blk_fill, n_active


def _moe(h2, logits, layer, wg, wu, wd):
    n_tok = logits.shape[0]
    n_exp = wg.shape[1]
    wts, dest, slot_tok, blk_exp, blk_fill, n_active = _route(logits)
    flat = lambda w: w.reshape((w.shape[0] * n_exp,) + w.shape[2:])
    yb = _moe_blocks(blk_exp + layer * n_exp, slot_tok, blk_fill, n_active, h2, flat(wg), flat(wu), flat(wd))
    dest = dest.reshape(n_tok, TOP_K)
    take = functools.partial(jnp.take, axis=0, mode="clip")
    return take(yb, dest[:, 0]), take(yb, dest[:, 1]), wts


def _final_kernel(x_ref, y0_ref, y1_ref, wt_ref, pmod_ref, g_ref, o_ref):
    y = wt_ref[:, 0:1] * y0_ref[...].astype(F32) + wt_ref[:, 1:2] * y1_ref[...].astype(F32)
    x = x_ref[...] + pmod_ref[0, 5:6, :] * y
    o_ref[...] = x * lax.rsqrt(jnp.mean(x * x, axis=-1, keepdims=True) + NORM_EPS) * g_ref[...]


def _final(x, ymoe, pmod, g, *, n_lat, seq, b):
    d = x.shape[1]
    tm = 512
    row = lambda i: (i, 0)
    y0, y1, wts = ymoe
    return pl.pallas_call(
        _final_kernel,
        grid=(n_lat // tm,),
        in_specs=[pl.BlockSpec((tm, d), row), pl.BlockSpec((tm, d), row), pl.BlockSpec((tm, d), row),
                  pl.BlockSpec((tm, TOP_K), row),
                  pl.BlockSpec((1, 6, d), _mod_row_map(tm, n_lat, seq, b)),
                  pl.BlockSpec((1, d), lambda i: (0, 0))],
        out_specs=pl.BlockSpec((tm, d), row),
        out_shape=jax.ShapeDtypeStruct((n_lat, d), F32),
        compiler_params=_cparams(("parallel",)),
        name="final_residual_norm",
    )(x, y0, y1, wts, pmod, g)


def _arrange_w_in(w_in_l):
    d = w_in_l.shape[0]
    hg = w_in_l[:, :5 * GW]
    o = 5 * GW
    ssd_z = w_in_l[:, o:o + GW]
    ssd_xbc = w_in_l[:, o + GW:o + GW + 1024]
    ssd_dt = w_in_l[:, o + GW + 1024:o + GW + 1024 + 2 * SSD_HEADS]
    o += GW + 1024 + 2 * SSD_HEADS
    lru = w_in_l[:, o:o + 2 * GW]
    o += 2 * GW
    ml_main = w_in_l[:, o:o + 4 * GW]
    ml_gates = w_in_l[:, o + 4 * GW:o + 4 * GW + 4 * ML_HEADS]
    small = jnp.concatenate([ssd_dt, ml_gates], axis=1)
    pad = jnp.zeros((d, NCOLS - C_SMALL - small.shape[1]), w_in_l.dtype)
    return jnp.concatenate([hg, ssd_z, ssd_xbc, lru, ml_main, small, pad], axis=1)


def _blockdiag_dense(w):
    n, bw, _ = w.shape
    eye = jnp.eye(n, dtype=w.dtype)
    return (eye[:, None, :, None] * w[:, :, None, :]).reshape(n * bw, n * bw)


def _pad_lanes(v):
    return jnp.concatenate([v, jnp.zeros((LANE - v.shape[0],), v.dtype)])[None, :]


def kernel(x, c, ctx, c_ctx, ada_w, ada_b, norm1_g, norm2_g, w_in, w_out, hg_lb_logits, ssd_conv_w, ssd_a_log, ssd_dt_bias, ssd_d, ssd_norm_g, lru_conv_w, lru_wa, lru_ba, lru_wx, lru_bx, lru_lambda, ml_b_i, ml_b_f, router_group_w, router_expert_w, moe_w_gate, moe_w_up, moe_w_down, final_norm_g):
    nb, seq, d = x.shape
    nctx = ctx.shape[1]
    depth = ada_w.shape[0]
    n_lat = nb * seq
    assert nctx % T_BLOCK == 0 and seq % T_BLOCK == 0
    nc, nl = nctx // T_BLOCK, seq // T_BLOCK
    kw = dict(nb=nb, nc=nc, nl=nl)
    lay = dict(n_lat=n_lat, seq=seq, b=nb)

    p = jax.nn.softmax(hg_lb_logits.astype(F32), axis=0)
    lb_all = (jnp.cumsum(p, axis=0) - p[:1])[:, :, None, :]

    c_all = jnp.concatenate([c, c_ctx[None, :], jnp.zeros((SUB - nb - 1, d), F32)], axis=0)
    mods = _modulation(c_all, ada_w, ada_b).reshape(depth, SUB, 6, d)

    xall = jnp.concatenate([x.reshape(n_lat, d), ctx.reshape(nb * nctx, d)], axis=0)
    y_moe = None
    tn = GW
    for l in range(depth):
        w_l = _bf(_arrange_w_in(w_in[l])).reshape(d, NCOLS // tn, tn).transpose(1, 0, 2)
        xall, h1 = _res_norm(xall, y_moe, mods[l - 1] if l else None, norm1_g[l][None, :], mods[l], **lay)
        u = _in_proj(h1, w_l)

        o = _hg_pass(u, lb_all[l], None, rev=False, **kw)
        y_hg = _hg_pass(u, lb_all[l], o, rev=True, **kw)

        sbias = _pad_lanes(jnp.concatenate([ssd_dt_bias[l, 0], ssd_dt_bias[l, 1],
                                            ml_b_i[l, 0], ml_b_i[l, 1], ml_b_f[l, 0], ml_b_f[l, 1]]))
        arow = _pad_lanes(jnp.concatenate([-jnp.exp(ssd_a_log[l, 0]), -jnp.exp(ssd_a_log[l, 1])]))
        dsk = jnp.repeat(ssd_d[l], SSD_HEADDIM)[None, :]
        o = _ssd_pass(u, sbias, arow, ssd_conv_w[l], None, None, rev=False, **kw)
        y_ssd = _ssd_pass(u, sbias, arow, ssd_conv_w[l], (dsk, ssd_norm_g[l][None, :]), o, rev=True, **kw)

        w2 = _bf(jnp.stack([jnp.concatenate([_blockdiag_dense(lru_wa[l, dd]), _blockdiag_dense(lru_wx[l, dd])],
                                            axis=1) for dd in range(2)]))
        b2 = jnp.stack([jnp.concatenate([lru_ba[l, dd], lru_bx[l, dd]])[None, :] for dd in range(2)])
        crow = (-LRU_C * jax.nn.softplus(-lru_lambda[l]))[:, None, :]
        y_lru = _lru_mixer(u, (lru_conv_w[l], w2, b2, crow), nb=nb, seq=seq, ctx=nctx)

        o = _ml_pass(u, sbias, None, rev=False, **kw)
        y_ml = _ml_pass(u, sbias, o, rev=True, **kw)

        w_router = jnp.concatenate(
            [router_group_w[l], router_expert_w[l],
             jnp.zeros((d, LANE - N_EXPERT_GROUPS * (1 + EXPERTS_PER_GROUP)), F32)], axis=1)
        wr1 = _bf(w_router)
        wr12 = jnp.concatenate([wr1, _bf(w_router - wr1.astype(F32))], axis=1)
        xall, h2, logits = _out_proj((y_hg, y_ssd, y_lru, y_ml), _bf(w_out[l]).reshape(4, GW, d), xall,
                                     mods[l], norm2_g[l][None, :], wr12, **lay)
        y_moe = _moe(h2, logits, l, moe_w_gate, moe_w_up, moe_w_down)

    out = _final(xall, y_moe, mods[depth - 1], final_norm_g[None, :], **lay)
    return out.reshape(nb, seq, d)
```

```python
import functools
import math

import jax
import jax.numpy as jnp
from jax import lax
from jax.experimental import pallas as pl
from jax.experimental.pallas import tpu as pltpu

F32 = jnp.float32
BF16 = jnp.bfloat16

GRID_W = 64
HG_HEADS, HG_DK = 4, 128
SSD_HEADS, SSD_HEADDIM, SSD_GROUPS, SSD_STATE = 8, 64, 2, 128
LRU_C = 8.0
ML_HEADS, ML_DH = 4, 128
N_EXPERT_GROUPS, EXPERTS_PER_GROUP, TOP_K = 4, 8, 2
MOE_BLOCK = 256
NORM_EPS = 1e-6
NEG = -1e30

GW = 512
C_HG_Q, C_HG_I, C_HG_G, C_HG_ZF, C_HG_ZB = 0, 512, 1024, 1536, 2048
C_SSD_Z, C_SSD_XBC = 2560, 3072
C_LRU_GATE, C_LRU_XR = 4096, 4608
C_ML_Q, C_ML_K, C_ML_V, C_ML_O = 5120, 5632, 6144, 6656
C_SMALL = 7168
SM_DT, SM_IG, SM_FG = 0, 16, 24
NCOLS = 7680

LANE = 128
SUB = 8
VMEM_LIMIT = 48 * 1024 * 1024

T_CHUNK = 128
T_BLOCK = 2 * T_CHUNK
HG_T_CHUNK = T_CHUNK
IN_PROJ_TM_MAX = 2816


def _cparams(sem):
    return pltpu.CompilerParams(dimension_semantics=sem, vmem_limit_bytes=VMEM_LIMIT)


def _dot(a, b):
    return jnp.dot(a, b, preferred_element_type=F32)


def _dot_nt(a, b):
    return lax.dot_general(a, b, (((1,), (1,)), ((), ())), preferred_element_type=F32)


def _dot_tn(a, b):
    return lax.dot_general(a, b, (((0,), (0,)), ((), ())), preferred_element_type=F32)


def _bf(x):
    return x.astype(BF16)


def _sigmoid(x):
    return 0.5 * jnp.tanh(0.5 * x) + 0.5


def _softplus(x):
    return jnp.maximum(x, 0.0) + jnp.log1p(jnp.exp(-jnp.abs(x)))


def _split_dot(mask_bf, x):
    w = x.shape[1]
    hi = _bf(x)
    r1 = x - hi.astype(F32)
    mid = _bf(r1)
    lo = _bf(r1 - mid.astype(F32))
    p = _dot(mask_bf, jnp.concatenate([hi, mid, lo], axis=-1))
    return p[:, :w] + p[:, w:2 * w] + p[:, 2 * w:]


def _iota2(t):
    return (lax.broadcasted_iota(jnp.int32, (t, t), 0), lax.broadcasted_iota(jnp.int32, (t, t), 1))


def _mod_kernel(c_ref, w_ref, b_ref, o_ref):
    c = c_ref[...]
    s = c * _sigmoid(c)
    o_ref[0] = jnp.dot(s, w_ref[0], precision=lax.Precision.HIGHEST,
                       preferred_element_type=F32) + b_ref[0]


def _modulation(c_all, ada_w, ada_b):
    depth, d, n6 = ada_w.shape
    tn = 1024
    return pl.pallas_call(
        _mod_kernel,
        grid=(depth, n6 // tn),
        in_specs=[pl.BlockSpec((SUB, d), lambda l, j: (0, 0)),
                  pl.BlockSpec((1, d, tn), lambda l, j: (l, 0, j)),
                  pl.BlockSpec((1, 1, tn), lambda l, j: (l, 0, j))],
        out_specs=pl.BlockSpec((1, SUB, tn), lambda l, j: (l, 0, j)),
        out_shape=jax.ShapeDtypeStruct((depth, SUB, n6), F32),
        compiler_params=_cparams(("parallel", "parallel")),
        name="adaln_modulation",
    )(c_all, ada_w, ada_b.reshape(depth, 1, n6))


def _rms_mod(x, g, shift, scale):
    y = x * lax.rsqrt(jnp.mean(x * x, axis=-1, keepdims=True) + NORM_EPS) * g
    return y * (1.0 + scale) + shift


def _mod_row_map(tm, n_lat, seq, b):
    def f(i, *_):
        return (jnp.where(i * tm < n_lat, (i * tm) // seq, b), 0, 0)
    return f


def _resnorm_kernel(*refs, residual):
    if residual:
        x_ref, y0_ref, y1_ref, wt_ref, pmod_ref, g_ref, mod_ref, xn_ref, h_ref = refs
        y = (wt_ref[:, 0:1] * y0_ref[...].astype(F32) + wt_ref[:, 1:2] * y1_ref[...].astype(F32))
        x = x_ref[...] + pmod_ref[0, 5:6, :] * y
        xn_ref[...] = x
    else:
        x_ref, g_ref, mod_ref, h_ref = refs
        x = x_ref[...]
    h_ref[...] = _bf(_rms_mod(x, g_ref[...], mod_ref[0, 0:1, :], mod_ref[0, 1:2, :]))


def _res_norm(x, ymoe, pmod, g1, mod, *, n_lat, seq, b):
    ntot, d = x.shape
    tm = 512
    residual = ymoe is not None
    row = lambda i: (i, 0)
    mrow = _mod_row_map(tm, n_lat, seq, b)
    in_specs = [pl.BlockSpec((tm, d), row)]
    args = [x]
    if residual:
        y0, y1, wts = ymoe
        in_specs += [pl.BlockSpec((tm, d), row), pl.BlockSpec((tm, d), lambda i: (i + ntot // tm, 0)),
                     pl.BlockSpec((tm, TOP_K), row), pl.BlockSpec((1, 6, d), mrow)]
        args += [y0, y1, wts, pmod]
    in_specs += [pl.BlockSpec((1, d), lambda i: (0, 0)), pl.BlockSpec((1, 6, d), mrow)]
    args += [g1, mod]
    out_specs = [pl.BlockSpec((tm, d), row)]
    out_shape = [jax.ShapeDtypeStruct((ntot, d), BF16)]
    if residual:
        out_specs = [pl.BlockSpec((tm, d), row)] + out_specs
        out_shape = [jax.ShapeDtypeStruct((ntot, d), F32)] + out_shape
    res = pl.pallas_call(
        functools.partial(_resnorm_kernel, residual=residual),
        grid=(ntot // tm,),
        in_specs=in_specs, out_specs=out_specs, out_shape=out_shape,
        compiler_params=_cparams(("parallel",)),
        name="residual_norm1",
    )(*args)
    return (res[0], res[1]) if residual else (x, res[0])


def _inproj_kernel(h_ref, w_ref, u_ref):
    u_ref[...] = _dot(h_ref[...], w_ref[0])


def _in_proj(h, w_tiles):
    ntot, d = h.shape
    nt, _, tn = w_tiles.shape
    tm = next(c for c in range(IN_PROJ_TM_MAX, 0, -MOE_BLOCK) if ntot % c == 0)
    return pl.pallas_call(
        _inproj_kernel,
        grid=(ntot // tm, nt),
        in_specs=[pl.BlockSpec((tm, d), lambda i, j: (i, 0)),
                  pl.BlockSpec((1, d, tn), lambda i, j: (j, 0, 0))],
        out_specs=pl.BlockSpec((tm, tn), lambda i, j: (i, j)),
        out_shape=jax.ShapeDtypeStruct((ntot, nt * tn), F32),
        compiler_params=_cparams(("parallel", "arbitrary")),
        name="in_proj",
    )(h, w_tiles)


def _chunk_nat(i, rev, nc, nl):
    is_ctx = i < nc
    if rev:
        c = jnp.where(is_ctx, nc - 1 - i, nl - 1 - (i - nc))
    else:
        c = jnp.where(is_ctx, i, i - nc)
    return is_ctx, c


def _row_block(b, i, *, rev, nc, nl, nb):
    is_ctx, c = _chunk_nat(i, rev, nc, nl)
    return jnp.where(is_ctx, nb * nl + b * nc + c, b * nl + c)


def _col_spec(t, width, col, rb):
    cb = col // width
    return pl.BlockSpec((t, width), lambda b, i: (rb(b, i), cb))


def _hg_kernel(*refs, rev, finish):
    if finish:
        q_ref, v_ref, z_ref, lb_ref, g_ref, of_ref, o_ref, st_scr = refs
    else:
        q_ref, v_ref, z_ref, lb_ref, o_ref, st_scr = refs
    t = HG_T_CHUNK
    dk = HG_DK
    w = q_ref.shape[1]

    @pl.when(pl.program_id(1) == 0)
    def _():
        st_scr[...] = jnp.zeros_like(st_scr)

    lb = lb_ref[0]
    row, col = _iota2(t)
    valid = (col >= row) if rev else (col <= row)
    lvl = jnp.where(valid, 31 - lax.clz(row ^ col), -2)
    valid_bf = _bf(valid.astype(F32))
    tpos = lax.broadcasted_iota(jnp.int32, (t, w), 0)
    r8 = lax.broadcasted_iota(jnp.int32, (t // SUB, SUB, w), 1)
    for rs in _sub_chunks(q_ref.shape[0], rev, t):
        _hg_chunk(rs, rev, finish, refs, lb, valid_bf, lvl, tpos, r8)


def _sub_chunks(block_rows, rev, t=T_CHUNK):
    n = block_rows // t
    order = range(n - 1, -1, -1) if rev else range(n)
    return [slice(i * t, (i + 1) * t) for i in order]


def _hg_chunk(rs, rev, finish, refs, lb, valid_bf, lvl, tpos, r8):
    if finish:
        q_ref, v_ref, z_ref, lb_ref, g_ref, of_ref, o_ref, st_scr = refs
    else:
        q_ref, v_ref, z_ref, lb_ref, o_ref, st_scr = refs
    t = HG_T_CHUNK
    dk = HG_DK
    w = q_ref.shape[1]
    sig = _sigmoid(z_ref[rs, :])
    f = lb + (1.0 - lb) * sig
    lf = jnp.log(f)
    k = (1.0 - lb) * (1.0 - sig)
    q = q_ref[rs, :]
    v = v_ref[rs, :]

    b = _split_dot(valid_bf, lf)
    total = jnp.sum(lf, axis=0, keepdims=True)

    b8 = b.reshape(t // SUB, SUB, w)
    ops = [(-1, _bf(q), _bf(k))]
    h = 1
    while h < t:
        blk = 2 * h
        lg = int(math.log2(h))
        is_q = ((tpos & h) == 0) if rev else ((tpos & h) != 0)
        if h == 1:
            e = f
        else:
            off = h if rev else h - 1
            if blk <= SUB:
                ref = b8[:, off:off + 1, :]
                for sb in range(1, SUB // blk):
                    ref = jnp.where(r8 >= sb * blk, b8[:, sb * blk + off:sb * blk + off + 1, :], ref)
                ref = jnp.broadcast_to(ref, b8.shape).reshape(t, w)
            else:
                b3 = b.reshape(t // blk, blk, w)
                ref = jnp.broadcast_to(b3[:, off:off + 1, :], b3.shape).reshape(t, w)
            e = jnp.exp(-jnp.abs(b - ref))
        if h == 1:
            m = _bf(jnp.where(is_q, q * e, k))
        else:
            m = _bf(jnp.where(is_q, q, k) * e)
        ops.append((lg, m, m))
        h = blk

    qb = _bf(q * jnp.exp(b))
    kw = _bf(k * jnp.exp(total - b))
    vb = _bf(v)
    dec_tot = jnp.exp(total)

    masks = [lvl == lg for lg, _, _ in ops]
    outs = []
    for hd in range(HG_HEADS):
        sl = slice(hd * dk, (hd + 1) * dk)
        a = jnp.zeros((t, t), F32)
        for msk, (lg, mq, mk) in zip(masks, ops):
            a = jnp.where(msk, _dot_nt(mq[:, sl], mk[:, sl]), a)
        st = st_scr[hd]
        o = _dot(_bf(a), vb[:, sl]) + _dot_nt(qb[:, sl], _bf(st))
        st_scr[hd] = st * dec_tot[:, sl] + _dot_tn(vb[:, sl], kw[:, sl])
        outs.append(o)

    if finish:
        for hd in range(HG_HEADS):
            sl = slice(hd * dk, (hd + 1) * dk)
            tot = outs[hd] + of_ref[rs, sl]
            y = tot * lax.rsqrt(jnp.mean(tot * tot, axis=-1, keepdims=True) + NORM_EPS)
            g = g_ref[rs, sl]
            o_ref[rs, sl] = (y * (g * _sigmoid(g))).astype(o_ref.dtype)
    else:
        for hd in range(HG_HEADS):
            o_ref[rs, hd * dk:(hd + 1) * dk] = outs[hd]


def _hg_pass(u, lb, o_f, *, rev, nb, nc, nl):
    t = T_BLOCK
    ntot = u.shape[0]
    finish = o_f is not None
    rb = functools.partial(_row_block, rev=rev, nc=nc, nl=nl, nb=nb)
    d = 1 if rev else 0
    in_specs = [_col_spec(t, GW, C_HG_Q, rb), _col_spec(t, GW, C_HG_I, rb),
                _col_spec(t, GW, C_HG_ZB if rev else C_HG_ZF, rb),
                pl.BlockSpec((1, 1, GW), lambda b, i: (d, 0, 0))]
    args = [u, u, u, lb]
    if finish:
        in_specs += [_col_spec(t, GW, C_HG_G, rb), _col_spec(t, GW, 0, rb)]
        args += [u, o_f]
    return pl.pallas_call(
        functools.partial(_hg_kernel, rev=rev, finish=finish),
        grid=(nb, nc + nl),
        in_specs=in_specs,
        out_specs=_col_spec(t, GW, 0, rb),
        out_shape=jax.ShapeDtypeStruct((ntot, GW), BF16 if finish else F32),
        scratch_shapes=[pltpu.VMEM((HG_HEADS, HG_DK, HG_DK), F32)],
        compiler_params=_cparams(("parallel", "arbitrary")),
        name="hgrn2_bwd_finish" if rev else "hgrn2_fwd",
    )(*args)


def _ml_kernel(*refs, rev, finish):
    ct_scr, m_scr = refs[-2:]
    t = T_CHUNK

    @pl.when(pl.program_id(1) == 0)
    def _():
        ct_scr[...] = jnp.zeros_like(ct_scr)
        m_scr[...] = jnp.full(m_scr.shape, NEG, F32)

    row, col = _iota2(t)
    valid = (col >= row) if rev else (col <= row)
    valid_bf = _bf(valid.astype(F32))
    for rs in _sub_chunks(refs[0].shape[0], rev):
        _ml_chunk(rs, rev, finish, refs, valid, valid_bf)


def _ml_chunk(rs, rev, finish, refs, valid, valid_bf):
    if finish:
        q_ref, k_ref, v_ref, sm_ref, sb_ref, og_ref, of_ref, o_ref, ct_scr, m_scr = refs
    else:
        q_ref, k_ref, v_ref, sm_ref, sb_ref, o_ref, ct_scr, m_scr = refs
    t = T_CHUNK
    dh = ML_DH
    d = 1 if rev else 0

    sm = sm_ref[rs, :] + sb_ref[...]
    lsg = jnp.minimum(sm, 0.0) - jnp.log1p(jnp.exp(-jnp.abs(sm)))
    bcol = _split_dot(valid_bf, lsg)
    tot_row = jnp.sum(lsg, axis=0, keepdims=True)
    b_t = bcol.T
    sm_t = sm.T

    ks = k_ref[rs, :] * (dh ** -0.5)
    qb, kb, vb = _bf(q_ref[rs, :]), _bf(ks), _bf(v_ref[rs, :])
    ones = jnp.ones((t, dh), BF16)

    for hd in range(ML_HEADS):
        sl = slice(hd * dh, (hd + 1) * dh)
        ci = SM_IG + ML_HEADS * d + hd
        cf = SM_FG + ML_HEADS * d + hd
        bc = bcol[:, cf:cf + 1]
        br = b_t[cf:cf + 1, :]
        lir = sm_t[ci:ci + 1, :]
        lic = sm[:, ci:ci + 1]
        m_prev = m_scr[hd:hd + 1, 0:1]
        dmat = jnp.where(valid, bc - br + lir, NEG)
        g = bc + m_prev
        mt = jnp.maximum(g, jnp.max(dmat, axis=-1, keepdims=True))
        w_intra = jnp.exp(dmat - mt)
        w_inter = jnp.exp(g - mt)
        s = _dot_nt(qb[:, sl], kb[:, sl]) * w_intra
        vaug = jnp.concatenate([vb[:, sl], ones], axis=-1)
        ct = ct_scr[hd]
        nd = _dot(_bf(s), vaug) + w_inter * _dot_nt(qb[:, sl], _bf(ct))
        hout = nd[:, :dh] / jnp.maximum(jnp.abs(nd[:, dh:]), jnp.exp(-mt))
        tot = tot_row[:, cf:cf + 1]
        dl = tot - bc + lic
        m_new = jnp.maximum(tot + m_prev, jnp.max(dl, axis=0, keepdims=True))
        ws = jnp.exp(dl - m_new)
        wc = jnp.exp(tot + m_prev - m_new)
        ct_scr[hd] = wc * ct + _dot_tn(vaug, _bf(ks[:, sl] * ws))
        m_scr[hd:hd + 1, :] = jnp.broadcast_to(m_new, (1, m_scr.shape[1]))
        if finish:
            tot_o = hout + of_ref[rs, sl]
            y = tot_o * lax.rsqrt(jnp.mean(tot_o * tot_o, axis=-1, keepdims=True) + NORM_EPS)
            o_ref[rs, sl] = (y * _sigmoid(og_ref[rs, sl])).astype(o_ref.dtype)
        else:
            o_ref[rs, sl] = hout


def _ml_pass(u, sbias, o_f, *, rev, nb, nc, nl):
    t = T_BLOCK
    ntot = u.shape[0]
    finish = o_f is not None
    rb = functools.partial(_row_block, rev=rev, nc=nc, nl=nl, nb=nb)
    in_specs = [_col_spec(t, GW, C_ML_Q, rb), _col_spec(t, GW, C_ML_K, rb),
                _col_spec(t, GW, C_ML_V, rb), _col_spec(t, LANE, C_SMALL, rb),
                pl.BlockSpec((1, LANE), lambda b, i: (0, 0))]
    args = [u, u, u, u, sbias]
    if finish:
        in_specs += [_col_spec(t, GW, C_ML_O, rb), _col_spec(t, GW, 0, rb)]
        args += [u, o_f]
    return pl.pallas_call(
        functools.partial(_ml_kernel, rev=rev, finish=finish),
        grid=(nb, nc + nl),
        in_specs=in_specs,
        out_specs=_col_spec(t, GW, 0, rb),
        out_shape=jax.ShapeDtypeStruct((ntot, GW), BF16 if finish else F32),
        scratch_shapes=[pltpu.VMEM((ML_HEADS, 2 * ML_DH, ML_DH), F32),
                        pltpu.VMEM((SUB, LANE), F32)],
        compiler_params=_cparams(("parallel", "arbitrary")),
        name="mlstm_bwd_finish" if rev else "mlstm_fwd",
    )(*args)


def _conv4(xe_scr, x, prev8, next8, w, first, last):
    t = x.shape[0]
    xe_scr[0:SUB, :] = jnp.where(first, 0.0, prev8)
    xe_scr[SUB:SUB + t, :] = x
    xe_scr[SUB + t:SUB + t + SUB, :] = jnp.where(last, 0.0, next8)
    out = w[0:1, :] * xe_scr[SUB - 2:SUB - 2 + t, :]
    for j in range(1, 4):
        out = out + w[j:j + 1, :] * xe_scr[SUB - 2 + j:SUB - 2 + j + t, :]
    return out


def _ssd_kernel(*refs, rev, finish, nc, nl):
    x_ref, xp_ref, xn_ref = refs[:3]
    cw_ref = refs[6]
    st_scr, xe_scr, xc_scr = refs[-3:]
    t = T_CHUNK
    i = pl.program_id(1)

    @pl.when(i == 0)
    def _():
        st_scr[...] = jnp.zeros_like(st_scr)

    is_ctx, c = _chunk_nat(i, rev, nc, nl)
    first = c == 0
    last = c == jnp.where(is_ctx, nc, nl) - 1
    xc = _conv4(xe_scr, x_ref[...], xp_ref[...], xn_ref[...], cw_ref[...], first, last)
    xc_scr[...] = xc * _sigmoid(xc)

    row, col = _iota2(t)
    valid = (col >= row) if rev else (col <= row)
    valid_bf = _bf(valid.astype(F32))
    for rs in _sub_chunks(x_ref.shape[0], rev):
        _ssd_chunk(rs, rev, finish, refs, valid, valid_bf)


def _ssd_chunk(rs, rev, finish, refs, valid, valid_bf):
    if finish:
        (x_ref, xp_ref, xn_ref, sm_ref, sb_ref, arow_ref, cw_ref,
         z_ref, of_ref, dsk_ref, ng_ref, o_ref, st_scr, xe_scr, xc_scr) = refs
    else:
        x_ref, xp_ref, xn_ref, sm_ref, sb_ref, arow_ref, cw_ref, o_ref, st_scr, xe_scr, xc_scr = refs
    p = SSD_HEADDIM
    n = SSD_STATE
    d = 1 if rev else 0
    xs = xc_scr[rs, :SSD_HEADS * p]
    bm = xc_scr[rs, SSD_HEADS * p:SSD_HEADS * p + SSD_GROUPS * n]
    cm = xc_scr[rs, SSD_HEADS * p + SSD_GROUPS * n:]

    dl = _softplus(sm_ref[rs, :] + sb_ref[...])
    lfm = dl * arow_ref[...]
    bcol = _split_dot(valid_bf, lfm)
    tot_row = jnp.sum(lfm, axis=0, keepdims=True)
    b_t = bcol.T
    dl_t = dl.T

    xsb = _bf(xs)
    rep = SSD_HEADS // SSD_GROUPS
    outs = []
    for gi in range(SSD_GROUPS):
        bg = bm[:, gi * n:(gi + 1) * n]
        cgb = _bf(cm[:, gi * n:(gi + 1) * n])
        gmat = _dot_nt(cgb, _bf(bg))
        for hh in range(rep):
            hd = gi * rep + hh
            cc = SM_DT + SSD_HEADS * d + hd
            bc = bcol[:, cc:cc + 1]
            br = b_t[cc:cc + 1, :]
            dr = dl_t[cc:cc + 1, :]
            dcol = dl[:, cc:cc + 1]
            tot = tot_row[:, cc:cc + 1]
            att = gmat * jnp.exp(jnp.where(valid, bc - br, NEG)) * dr
            xh = xsb[:, hd * p:(hd + 1) * p]
            st = st_scr[hd]
            o = _dot(_bf(att), xh) + jnp.exp(bc) * _dot_nt(cgb, _bf(st))
            wk = dcol * jnp.exp(tot - bc)
            st_scr[hd] = jnp.exp(tot) * st + _dot_tn(xh, _bf(bg * wk))
            outs.append(o)
    o_all = jnp.concatenate(outs, axis=-1)

    if finish:
        y = (o_all + of_ref[rs, :] + dsk_ref[...] * xs)
        z = z_ref[rs, :]
        y = y * (z * _sigmoid(z))
        gw = SSD_HEADS * p // SSD_GROUPS
        for gi in range(SSD_GROUPS):
            sl = slice(gi * gw, (gi + 1) * gw)
            yg = y[:, sl]
            yg = yg * lax.rsqrt(jnp.mean(yg * yg, axis=-1, keepdims=True) + NORM_EPS)
            o_ref[rs, sl] = (yg * ng_ref[:, sl]).astype(o_ref.dtype)
    else:
        o_ref[rs, :] = o_all


def _ssd_pass(u, sbias, arow, conv_w, z_extra, o_f, *, rev, nb, nc, nl):
    t = T_BLOCK
    ntot = u.shape[0]
    finish = o_f is not None
    rb = functools.partial(_row_block, rev=rev, nc=nc, nl=nl, nb=nb)
    r8 = t // SUB
    nblk8 = ntot // SUB
    cb = C_SSD_XBC // 1024

    def prev_map(b, i):
        return (jnp.maximum(rb(b, i) * r8 - 1, 0), cb)

    def next_map(b, i):
        return (jnp.minimum((rb(b, i) + 1) * r8, nblk8 - 1), cb)

    const2 = lambda b, i: (0, 0)
    in_specs = [_col_spec(t, 1024, C_SSD_XBC, rb),
                pl.BlockSpec((SUB, 1024), prev_map), pl.BlockSpec((SUB, 1024), next_map),
                _col_spec(t, LANE, C_SMALL, rb),
                pl.BlockSpec((1, LANE), const2), pl.BlockSpec((1, LANE), const2),
                pl.BlockSpec((4, 1024), const2)]
    args = [u, u, u, u, sbias, arow, conv_w]
    if finish:
        dsk, ng = z_extra
        in_specs += [_col_spec(t, GW, C_SSD_Z, rb), _col_spec(t, GW, 0, rb),
                     pl.BlockSpec((1, GW), const2), pl.BlockSpec((1, GW), const2)]
        args += [u, o_f, dsk, ng]
    return pl.pallas_call(
        functools.partial(_ssd_kernel, rev=rev, finish=finish, nc=nc, nl=nl),
        grid=(nb, nc + nl),
        in_specs=in_specs,
        out_specs=_col_spec(t, GW, 0, rb),
        out_shape=jax.ShapeDtypeStruct((ntot, GW), BF16 if finish else F32),
        scratch_shapes=[pltpu.VMEM((SSD_HEADS, SSD_HEADDIM, SSD_STATE), F32),
                        pltpu.VMEM((t + 2 * SUB, 1024), F32), pltpu.VMEM((t, 1024), F32)],
        compiler_params=_cparams(("parallel", "arbitrary")),
        name="ssd_bwd_finish" if rev else "ssd_fwd",
    )(*args)


def _lru_scan(xr, w2, b2, crow, h, rev):
    t, w = xr.shape
    ng = t // SUB
    pre = _dot(_bf(xr), w2) + b2
    r = _sigmoid(pre[:, :w])
    ig = _sigmoid(pre[:, w:])
    log_a = crow * r
    a = jnp.exp(log_a)
    y = jnp.tanh(-log_a) * (1.0 + a * a)
    bb = jnp.where(y > 0.0, y * lax.rsqrt(y), 0.0) * (ig * xr)

    a3 = a.reshape(ng, SUB, w)
    b3 = bb.reshape(ng, SUB, w)
    r8 = lax.broadcasted_iota(jnp.int32, (ng, SUB, w), 1)
    pos = (SUB - 1 - r8) if rev else r8
    step = 1
    while step < SUB:
        sh = (SUB - step) if rev else step
        a_s = pltpu.roll(a3, sh, 1)
        b_s = pltpu.roll(b3, sh, 1)
        ok = pos >= step
        b3 = jnp.where(ok, a3 * b_s + b3, b3)
        a3 = jnp.where(ok, a3 * a_s, a3)
        step *= 2
    groups = [None] * ng
    for g in (range(ng - 1, -1, -1) if rev else range(ng)):
        hg = b3[g] + a3[g] * h
        groups[g] = hg
        h = hg[0:1, :] if rev else hg[SUB - 1:SUB, :]
    return jnp.concatenate(groups, axis=0), h


def _gelu_tanh(g):
    return 0.5 * g * (1.0 + jnp.tanh(math.sqrt(2.0 / math.pi) * (g + 0.044715 * (g * g * g))))


LRU_COLS_PER_STEP = SUB


def _lru_kernel(*refs, rev, finish, latent, nsteps):
    if finish:
        (x_ref, xp_ref, xn_ref, cw_ref, w2_ref, b2_ref, crow_ref, h0_ref,
         gate_ref, of_ref, o_ref, ht_ref, h_scr, xe_scr) = refs
    else:
        x_ref, xp_ref, xn_ref, cw_ref, w2_ref, b2_ref, crow_ref, h0_ref, o_ref, ht_ref, h_scr, xe_scr = refs
    i = pl.program_id(1)

    @pl.when(i == 0)
    def _():
        h_scr[...] = h0_ref[0]

    c = (nsteps - 1 - i) if rev else i
    cw, w2, b2, crow = cw_ref[...], w2_ref[0], b2_ref[0], crow_ref[0]
    h = h_scr[...]
    if not latent:
        xr = _conv4(xe_scr, x_ref[...], xp_ref[...], xn_ref[...], cw, c == 0, c == nsteps - 1)
        hs, h = _lru_scan(xr, w2, b2, crow, h, rev)
        if finish:
            o_ref[...] = ((hs + of_ref[...]) * _gelu_tanh(gate_ref[...])).astype(o_ref.dtype)
        else:
            o_ref[...] = hs
    else:
        rows, ncol = x_ref.shape[0], x_ref.shape[1]
        for cl in (range(ncol - 1, -1, -1) if rev else range(ncol)):
            prev8 = x_ref[rows - SUB:rows, cl - 1, :] if cl > 0 else xp_ref[:, ncol - 1, :]
            next8 = x_ref[0:SUB, cl + 1, :] if cl < ncol - 1 else xn_ref[:, 0, :]
            first = (c == 0) if cl == 0 else False
            last = (c == nsteps - 1) if cl == ncol - 1 else False
            xr = _conv4(xe_scr, x_ref[:, cl, :], prev8, next8, cw, first, last)
            hs, h = _lru_scan(xr, w2, b2, crow, h, rev)
            if finish:
                o_ref[:, cl, :] = ((hs + of_ref[:, cl, :]) * _gelu_tanh(gate_ref[:, cl, :])).astype(o_ref.dtype)
            else:
                o_ref[:, cl, :] = hs
    h_scr[...] = h

    @pl.when(i == nsteps - 1)
    def _():
        ht_ref[0] = h


def _lru_pass(arr, views, consts, h0, o_f, *, rev, nb, nsteps, latent, out_shape, rows=None):
    finish = o_f is not None
    d = 1 if rev else 0
    main_map, prev_map, next_map, gate_map, out_map = views
    cw, w2, b2, crow = consts
    if latent:
        blk, halo = (rows, LRU_COLS_PER_STEP, GW), (SUB, LRU_COLS_PER_STEP, GW)
        t = rows
    else:
        blk, halo = (T_BLOCK, GW), (SUB, GW)
        t = T_BLOCK
    const2 = lambda b, i: (0, 0)
    dsel = lambda b, i: (d, 0, 0)
    hsel = lambda b, i: (b, 0, 0)
    in_specs = [pl.BlockSpec(blk, main_map), pl.BlockSpec(halo, prev_map), pl.BlockSpec(halo, next_map),
                pl.BlockSpec((4, GW), const2),
                pl.BlockSpec((1, GW, 2 * GW), dsel), pl.BlockSpec((1, 1, 2 * GW), dsel),
                pl.BlockSpec((1, 1, GW), dsel), pl.BlockSpec((1, 1, GW), hsel)]
    args = [arr, arr, arr, cw, w2, b2, crow, h0]
    if finish:
        in_specs += [pl.BlockSpec(blk, gate_map), pl.BlockSpec(blk, out_map)]
        args += [arr, o_f]
    seg = "lat" if latent else "ctx"
    return pl.pallas_call(
        functools.partial(_lru_kernel, rev=rev, finish=finish, latent=latent, nsteps=nsteps),
        grid=(nb, nsteps),
        in_specs=in_specs,
        out_specs=[pl.BlockSpec(blk, out_map), pl.BlockSpec((1, 1, GW), hsel)],
        out_shape=[jax.ShapeDtypeStruct(out_shape, BF16 if (finish and not latent) else F32),
                   jax.ShapeDtypeStruct((nb, 1, GW), F32)],
        scratch_shapes=[pltpu.VMEM((1, GW), F32), pltpu.VMEM((t + 2 * SUB, GW), F32)],
        compiler_params=_cparams(("parallel", "arbitrary")),
        name=f"rglru_{seg}_bwd_finish" if rev else f"rglru_{seg}_fwd",
    )(*args)


def _lru_mixer(u, consts, *, nb, seq, ctx):
    ntot, ncols = u.shape
    n_lat = nb * seq
    rows = seq // GRID_W
    ncg = GRID_W // LRU_COLS_PER_STEP
    ncb = ctx // T_BLOCK
    xr_cb, gate_cb = C_LRU_XR // GW, C_LRU_GATE // GW
    u3 = u.reshape(ntot // GRID_W, GRID_W, ncols)
    r8, c8 = rows // SUB, T_BLOCK // SUB
    base, base8 = n_lat // T_BLOCK, n_lat // SUB
    h = jnp.zeros((nb, 1, GW), F32)
    o_c = o_l = None
    for rev in (False, True):
        nat_c = (lambda i: ncb - 1 - i) if rev else (lambda i: i)
        nat_l = (lambda i: ncg - 1 - i) if rev else (lambda i: i)
        cviews = (lambda b, i: (base + b * ncb + nat_c(i), xr_cb),
                  lambda b, i: (jnp.maximum(base8 + (b * ncb + nat_c(i)) * c8 - 1, 0), xr_cb),
                  lambda b, i: (jnp.minimum(base8 + (b * ncb + nat_c(i) + 1) * c8, ntot // SUB - 1), xr_cb),
                  lambda b, i: (base + b * ncb + nat_c(i), gate_cb),
                  lambda b, i: (b * ncb + nat_c(i), 0))
        o_c, h_c = _lru_pass(u, cviews, consts, h, o_c, rev=rev, nb=nb, nsteps=ncb, latent=False,
                             out_shape=(nb * ctx, GW))
        lviews = (lambda b, i: (b, nat_l(i), xr_cb),
                  lambda b, i: (b * r8 + r8 - 1, jnp.maximum(nat_l(i) - 1, 0), xr_cb),
                  lambda b, i: (b * r8, jnp.minimum(nat_l(i) + 1, ncg - 1), xr_cb),
                  lambda b, i: (b, nat_l(i), gate_cb),
                  lambda b, i: (b, nat_l(i), 0))
        o_l, _ = _lru_pass(u3, lviews, consts, h_c, o_l, rev=rev, nb=nb, nsteps=ncg, latent=True,
                           out_shape=(n_lat // GRID_W, GRID_W, GW), rows=rows)
    return jnp.concatenate([_bf(o_l).reshape(n_lat, GW), o_c], axis=0)


def _outproj_kernel(yh_ref, ys_ref, yl_ref, ym_ref, w_ref, x_ref, mod_ref, g_ref, wr_ref,
                    xn_ref, h_ref, lg_ref):
    acc = _dot(yh_ref[...], w_ref[0])
    acc = acc + _dot(ys_ref[...], w_ref[1])
    acc = acc + _dot(yl_ref[...], w_ref[2])
    acc = acc + _dot(ym_ref[...], w_ref[3])
    xn = x_ref[...] + mod_ref[0, 2:3, :] * acc
    xn_ref[...] = xn
    h = _rms_mod(xn, g_ref[...], mod_ref[0, 3:4, :], mod_ref[0, 4:5, :])
    h1 = _bf(h)
    tm, d = h.shape
    bits = lax.bitcast_convert_type(h1.astype(F32), jnp.uint32)
    packed = (bits[:, :d // 2] >> 16) | (bits[:, d // 2:] & jnp.uint32(0xFFFF0000))
    nch = d // 2 // LANE
    for j in range(nch):
        h_ref[pl.ds(j, tm, stride=nch), :] = packed[:, j * LANE:(j + 1) * LANE]
    h2 = _bf(h - h1.astype(F32))
    p = _dot(h1, wr_ref[...])
    lg_ref[...] = p[:, :LANE] + p[:, LANE:] + _dot(h2, wr_ref[:, :LANE])


def _out_proj(ys, w_out4, x, mod, g2, w_router, *, n_lat, seq, b):
    ntot, d = x.shape
    tm = 512
    row = lambda i: (i, 0)
    mrow = _mod_row_map(tm, n_lat, seq, b)
    in_specs = [pl.BlockSpec((tm, GW), row)] * 4 + [
        pl.BlockSpec((4, GW, d), lambda i: (0, 0, 0)),
        pl.BlockSpec((tm, d), row), pl.BlockSpec((1, 6, d), mrow),
        pl.BlockSpec((1, d), lambda i: (0, 0)), pl.BlockSpec((d, 2 * LANE), lambda i: (0, 0))]
    return pl.pallas_call(
        _outproj_kernel,
        grid=(ntot // tm,),
        in_specs=in_specs,
        out_specs=[pl.BlockSpec((tm, d), row), pl.BlockSpec((tm * (d // 2 // LANE), LANE), row),
                   pl.BlockSpec((tm, LANE), row)],
        out_shape=[jax.ShapeDtypeStruct((ntot, d), F32),
                   jax.ShapeDtypeStruct((ntot * (d // 2 // LANE), LANE), jnp.uint32),
                   jax.ShapeDtypeStruct((ntot, LANE), F32)],
        compiler_params=_cparams(("parallel",)),
        name="out_proj_norm2_router",
    )(*ys, w_out4, x, mod, g2, w_router)


def _row_copy(h_hbm, xbuf, sem, tok, slot, r):
    nch = xbuf.shape[1] // MOE_BLOCK
    return pltpu.make_async_copy(h_hbm.at[pl.ds(tok * nch, nch), :], xbuf.at[slot, pl.ds(r * nch, nch), :],
                                 sem.at[slot])


MOE_DMA_GROUP = 32


def _moe_kernel(be_ref, tok_ref, nv_ref, na_ref, h_hbm, wg_ref, wu_ref, wd_ref, o_ref,
                xbuf, sem, wgu_scr, wd_scr):
    i = pl.program_id(0)
    n_active = na_ref[0]
    de = wg_ref.shape[-1]
    slot = i % 2
    nch = xbuf.shape[1] // MOE_BLOCK

    def for_filled_groups(blk, fn):
        for g0 in range(0, MOE_BLOCK, MOE_DMA_GROUP):
            @pl.when(g0 < nv_ref[blk])
            def _():
                for r in range(g0, g0 + MOE_DMA_GROUP):
                    fn(r)

    def start_gather(blk, s):
        for_filled_groups(
            blk, lambda r: _row_copy(h_hbm, xbuf, sem, tok_ref[blk * MOE_BLOCK + r], s, r).start())

    @pl.when(i == 0)
    def _():
        xbuf[...] = jnp.zeros_like(xbuf)
        start_gather(0, 0)

    @pl.when(i < n_active)
    def _():
        @pl.when(jnp.logical_or(i == 0, be_ref[i] != be_ref[jnp.maximum(i - 1, 0)]))
        def _():
            wgu_scr[:, :de] = _bf(wg_ref[0])
            wgu_scr[:, de:] = _bf(wu_ref[0])
            wd_scr[...] = _bf(wd_ref[0])

        for_filled_groups(i, lambda r: _row_copy(h_hbm, xbuf, sem, 0, slot, r).wait())

        @pl.when(i + 1 < n_active)
        def _():
            start_gather(i + 1, 1 - slot)

        lo, hi = [], []
        for j in range(nch):
            wv = xbuf[slot, pl.ds(j, MOE_BLOCK, stride=nch), :]
            lo.append(_bf(lax.bitcast_convert_type(wv << 16, F32)))
            hi.append(_bf(lax.bitcast_convert_type(wv & jnp.uint32(0xFFFF0000), F32)))
        xb = jnp.concatenate(lo + hi, axis=-1)
        gu = _dot(xb, wgu_scr[...])
        g = gu[:, :de]
        hid = (g * _sigmoid(g)) * gu[:, de:]
        o_ref[...] = _bf(_dot(_bf(hid), wd_scr[...]))

    @pl.when(i >= n_active)
    def _():
        o_ref[...] = jnp.zeros_like(o_ref)


def _moe_blocks(blk_exp, slot_tok, blk_fill, n_active, h, wg, wu, wd):
    nslot = slot_tok.shape[0]
    d = wg.shape[1]
    nch = d // 2 // LANE
    de = wg.shape[-1]
    nblk = nslot // MOE_BLOCK
    return pl.pallas_call(
        _moe_kernel,
        grid_spec=pltpu.PrefetchScalarGridSpec(
            num_scalar_prefetch=4,
            grid=(nblk,),
            in_specs=[pl.BlockSpec(memory_space=pl.ANY),
                      pl.BlockSpec((1, d, de), lambda i, be, tk, nv, na: (be[i], 0, 0)),
                      pl.BlockSpec((1, d, de), lambda i, be, tk, nv, na: (be[i], 0, 0)),
                      pl.BlockSpec((1, de, d), lambda i, be, tk, nv, na: (be[i], 0, 0))],
            out_specs=pl.BlockSpec((MOE_BLOCK, d), lambda i, be, tk, nv, na: (i, 0)),
            scratch_shapes=[pltpu.VMEM((2, MOE_BLOCK * nch, LANE), jnp.uint32), pltpu.SemaphoreType.DMA((2,)),
                            pltpu.VMEM((d, 2 * de), BF16), pltpu.VMEM((de, d), BF16)]),
        out_shape=jax.ShapeDtypeStruct((nslot, d), BF16),
        compiler_params=_cparams(("arbitrary",)),
        name="moe_expert_blocks",
    )(blk_exp, slot_tok, blk_fill, n_active, h, wg, wu, wd)


def _route(logits):
    n_tok = logits.shape[0]
    n_exp = N_EXPERT_GROUPS * EXPERTS_PER_GROUP
    g_logits = logits[:, :N_EXPERT_GROUPS]
    g_prob = jax.nn.softmax(g_logits, axis=-1)
    g_sel = jnp.argmax(g_logits, axis=-1)
    g_hot = g_sel[:, None] == jnp.arange(N_EXPERT_GROUPS)[None, :]
    e_logits = logits[:, N_EXPERT_GROUPS:N_EXPERT_GROUPS + n_exp].reshape(
        n_tok, N_EXPERT_GROUPS, EXPERTS_PER_GROUP)
    within = jnp.sum(jnp.where(g_hot[:, :, None], e_logits, 0.0), axis=1)
    top_v, top_i = lax.top_k(within, TOP_K)
    wts = jax.nn.softmax(top_v, axis=-1) * jnp.sum(jnp.where(g_hot, g_prob, 0.0), axis=1, keepdims=True)
    eid = (g_sel[:, None] * EXPERTS_PER_GROUP + top_i).astype(jnp.int32)

    n_assign = n_tok * TOP_K
    flat_e = eid.reshape(-1)
    onehot = (flat_e[:, None] == jnp.arange(n_exp, dtype=jnp.int32)[None, :]).astype(jnp.int32)
    csum = jnp.cumsum(onehot, axis=0)
    counts = csum[-1]
    padded = (counts + MOE_BLOCK - 1) // MOE_BLOCK * MOE_BLOCK
    pend = jnp.cumsum(padded)
    pstart = pend - padded
    dest = jnp.sum(onehot * (csum - 1 + pstart[None, :]), axis=1)
    n_blk = -(-n_assign // MOE_BLOCK) + n_exp
    tok = jnp.arange(n_assign, dtype=jnp.int32) // TOP_K
    slot_tok = jnp.zeros((n_blk * MOE_BLOCK,), jnp.int32).at[dest].set(
        tok, unique_indices=True, mode="promise_in_bounds")
    blk_start = jnp.arange(n_blk, dtype=jnp.int32) * MOE_BLOCK
    blk_exp = jnp.minimum(jnp.sum(blk_start[:, None] >= pend[None, :], axis=1), n_exp - 1).astype(jnp.int32)
    n_active = (pend[-1:] // MOE_BLOCK).astype(jnp.int32)
    blk_fill = jnp.clip(counts[blk_exp] + pstart[blk_exp] - blk_start, 0, MOE_BLOCK).astype(jnp.int32)
    return wts, dest, slot_tok, blk_exp, blk_fill, n_active


def _moe(h2, logits, layer, wg, wu, wd):
    n_tok = logits.shape[0]
    n_exp = wg.shape[1]
    wts, dest, slot_tok, blk_exp, blk_fill, n_active = _route(logits)
    flat = lambda w: w.reshape((w.shape[0] * n_exp,) + w.shape[2:])
    yb = _moe_blocks(blk_exp + layer * n_exp, slot_tok, blk_fill, n_active, h2, flat(wg), flat(wu), flat(wd))
    yk = jnp.take(yb, dest.reshape(n_tok, TOP_K).T.reshape(-1), axis=0, mode="clip")
    return yk, yk, wts


def _final_kernel(x_ref, y0_ref, y1_ref, wt_ref, pmod_ref, g_ref, o_ref):
    y = wt_ref[:, 0:1] * y0_ref[...].astype(F32) + wt_ref[:, 1:2] * y1_ref[...].astype(F32)
    x = x_ref[...] + pmod_ref[0, 5:6, :] * y
    o_ref[...] = x * lax.rsqrt(jnp.mean(x * x, axis=-1, keepdims=True) + NORM_EPS) * g_ref[...]


def _final(x, ymoe, pmod, g, *, n_lat, seq, b):
    d = x.shape[1]
    tm = 512
    row = lambda i: (i, 0)
    y0, y1, wts = ymoe
    return pl.pallas_call(
        _final_kernel,
        grid=(n_lat // tm,),
        in_specs=[pl.BlockSpec((tm, d), row), pl.BlockSpec((tm, d), row),
                  pl.BlockSpec((tm, d), lambda i: (i + y1.shape[0] // 2 // tm, 0)),
                  pl.BlockSpec((tm, TOP_K), row),
                  pl.BlockSpec((1, 6, d), _mod_row_map(tm, n_lat, seq, b)),
                  pl.BlockSpec((1, d), lambda i: (0, 0))],
        out_specs=pl.BlockSpec((tm, d), row),
        out_shape=jax.ShapeDtypeStruct((n_lat, d), F32),
        compiler_params=_cparams(("parallel",)),
        name="final_residual_norm",
    )(x, y0, y1, wts, pmod, g)


def _arrange_w_in(w_in_l):
    d = w_in_l.shape[0]
    hg = w_in_l[:, :5 * GW]
    o = 5 * GW
    ssd_z = w_in_l[:, o:o + GW]
    ssd_xbc = w_in_l[:, o + GW:o + GW + 1024]
    ssd_dt = w_in_l[:, o + GW + 1024:o + GW + 1024 + 2 * SSD_HEADS]
    o += GW + 1024 + 2 * SSD_HEADS
    lru = w_in_l[:, o:o + 2 * GW]
    o += 2 * GW
    ml_main = w_in_l[:, o:o + 4 * GW]
    ml_gates = w_in_l[:, o + 4 * GW:o + 4 * GW + 4 * ML_HEADS]
    small = jnp.concatenate([ssd_dt, ml_gates], axis=1)
    pad = jnp.zeros((d, NCOLS - C_SMALL - small.shape[1]), w_in_l.dtype)
    return jnp.concatenate([hg, ssd_z, ssd_xbc, lru, ml_main, small, pad], axis=1)


def _blockdiag_dense(w):
    n, bw, _ = w.shape
    eye = jnp.eye(n, dtype=w.dtype)
    return (eye[:, None, :, None] * w[:, :, None, :]).reshape(n * bw, n * bw)


def _pad_lanes(v):
    return jnp.concatenate([v, jnp.zeros((LANE - v.shape[0],), v.dtype)])[None, :]


def kernel(x, c, ctx, c_ctx, ada_w, ada_b, norm1_g, norm2_g, w_in, w_out, hg_lb_logits, ssd_conv_w, ssd_a_log, ssd_dt_bias, ssd_d, ssd_norm_g, lru_conv_w, lru_wa, lru_ba, lru_wx, lru_bx, lru_lambda, ml_b_i, ml_b_f, router_group_w, router_expert_w, moe_w_gate, moe_w_up, moe_w_down, final_norm_g):
    nb, seq, d = x.shape
    nctx = ctx.shape[1]
    depth = ada_w.shape[0]
    n_lat = nb * seq
    assert nctx % T_BLOCK == 0 and seq % T_BLOCK == 0
    nc, nl = nctx // T_BLOCK, seq // T_BLOCK
    kw = dict(nb=nb, nc=nc, nl=nl)
    lay = dict(n_lat=n_lat, seq=seq, b=nb)

    p = jax.nn.softmax(hg_lb_logits.astype(F32), axis=0)
    lb_all = (jnp.cumsum(p, axis=0) - p[:1])[:, :, None, :]

    c_all = jnp.concatenate([c, c_ctx[None, :], jnp.zeros((SUB - nb - 1, d), F32)], axis=0)
    mods = _modulation(c_all, ada_w, ada_b).reshape(depth, SUB, 6, d)

    xall = jnp.concatenate([x.reshape(n_lat, d), ctx.reshape(nb * nctx, d)], axis=0)
    y_moe = None
    tn = GW
    for l in range(depth):
        w_l = _bf(_arrange_w_in(w_in[l])).reshape(d, NCOLS // tn, tn).transpose(1, 0, 2)
        xall, h1 = _res_norm(xall, y_moe, mods[l - 1] if l else None, norm1_g[l][None, :], mods[l], **lay)
        u = _in_proj(h1, w_l)

        o = _hg_pass(u, lb_all[l], None, rev=False, **kw)
        y_hg = _hg_pass(u, lb_all[l], o, rev=True, **kw)

        sbias = _pad_lanes(jnp.concatenate([ssd_dt_bias[l, 0], ssd_dt_bias[l, 1],
                                            ml_b_i[l, 0], ml_b_i[l, 1], ml_b_f[l, 0], ml_b_f[l, 1]]))
        arow = _pad_lanes(jnp.concatenate([-jnp.exp(ssd_a_log[l, 0]), -jnp.exp(ssd_a_log[l, 1])]))
        dsk = jnp.repeat(ssd_d[l], SSD_HEADDIM)[None, :]
        o = _ssd_pass(u, sbias, arow, ssd_conv_w[l], None, None, rev=False, **kw)
        y_ssd = _ssd_pass(u, sbias, arow, ssd_conv_w[l], (dsk, ssd_norm_g[l][None, :]), o, rev=True, **kw)

        w2 = _bf(jnp.stack([jnp.concatenate([_blockdiag_dense(lru_wa[l, dd]), _blockdiag_dense(lru_wx[l, dd])],
                                            axis=1) for dd in range(2)]))
        b2 = jnp.stack([jnp.concatenate([lru_ba[l, dd], lru_bx[l, dd]])[None, :] for dd in range(2)])
        crow = (-LRU_C * jax.nn.softplus(-lru_lambda[l]))[:, None, :]
        y_lru = _lru_mixer(u, (lru_conv_w[l], w2, b2, crow), nb=nb, seq=seq, ctx=nctx)

        o = _ml_pass(u, sbias, None, rev=False, **kw)
        y_ml = _ml_pass(u, sbias, o, rev=True, **kw)

        w_router = jnp.concatenate(
            [router_group_w[l], router_expert_w[l],
             jnp.zeros((d, LANE - N_EXPERT_GROUPS * (1 + EXPERTS_PER_GROUP)), F32)], axis=1)
        wr1 = _bf(w_router)
        wr12 = jnp.concatenate([wr1, _bf(w_router - wr1.astype(F32))], axis=1)
        xall, h2, logits = _out_proj((y_hg, y_ssd, y_lru, y_ml), _bf(w_out[l]).reshape(4, GW, d), xall,
                                     mods[l], norm2_g[l][None, :], wr12, **lay)
        y_moe = _moe(h2, logits, l, moe_w_gate, moe_w_up, moe_w_down)

    out = _final(xall, y_moe, mods[depth - 1], final_norm_g[None, :], **lay)
    return out.reshape(nb, seq, d)
```
